```python
import jax, jax.numpy as jnp
from jax import lax
import numpy as np

D_MODEL = 2048
BATCH = 4
SEQ = 4096
DEPTH = 1
DEC_BATCH = 128
DEC_SEQ = 1
PAST_LEN = 16384
PAGE_SIZE = 128

N_HEADS = 16
N_KV_HEADS = 4
HEAD_DIM = 64
GROUP = N_HEADS // N_KV_HEADS
Q_DIM = N_HEADS * HEAD_DIM
KV_DIM = N_KV_HEADS * HEAD_DIM
ROT_DIM = HEAD_DIM // 4
ROPE_THETA = 500000.0
WINDOW = 128
CONV_DIM = D_MODEL // 2
CONV_WIDTH = 3
IN_COLS = Q_DIM + 2 * KV_DIM + 3 * CONV_DIM + 2 * D_MODEL
N_EXPERTS = 64
N_GROUPS = 8
TOPK_GROUPS = 4
TOP_K = 8
EXPERT_DIM = 512
SHARED_DIM = 512
ROUTED_SCALE = 2.5
EXPERT_BLOCK = 128
ALPHA = (2 * DEPTH) ** 0.25
BETA = (8 * DEPTH) ** -0.25
LN_EPS = 1e-5

kernel_name = "hybrid_swa_sink_shortconv_moe_decode_step"


def layer_norm(x, g, b):
    xf = x.astype(jnp.float32)
    mu = xf.mean(-1, keepdims=True)
    var = jnp.mean(jnp.square(xf - mu), -1, keepdims=True)
    return ((xf - mu) * lax.rsqrt(var + LN_EPS) * g.astype(jnp.float32) + b.astype(jnp.float32)).astype(x.dtype)


def apply_rope(x, pos):
    half = ROT_DIM // 2
    inv_freq = jnp.power(jnp.float32(ROPE_THETA), -jnp.arange(half, dtype=jnp.float32) * (2.0 / ROT_DIM))
    ang = pos.astype(jnp.float32)[:, None] * inv_freq[None, :]
    cos = jnp.cos(ang)[:, None, :]
    sin = jnp.sin(ang)[:, None, :]
    xr = x[..., :ROT_DIM].astype(jnp.float32)
    x1, x2 = xr[..., :half], xr[..., half:]
    rot = jnp.concatenate([x1 * cos - x2 * sin, x2 * cos + x1 * sin], axis=-1).astype(x.dtype)
    return jnp.concatenate([rot, x[..., ROT_DIM:]], axis=-1)


def split_projection(z):
    sizes = (Q_DIM, KV_DIM, KV_DIM, CONV_DIM, CONV_DIM, CONV_DIM, D_MODEL, D_MODEL)
    return jnp.split(z, np.cumsum(sizes)[:-1].tolist(), axis=-1)


def mixer_inputs(x, w_in, pos):
    n, s = x.shape[:2]
    q, k, v, b_g, c_g, h_c, g_a, g_c = split_projection(x @ w_in)
    q = apply_rope(q.reshape(n, s, N_HEADS, HEAD_DIM), pos)
    k = apply_rope(k.reshape(n, s, N_KV_HEADS, HEAD_DIM), pos)
    v = v.reshape(n, s, N_KV_HEADS, HEAD_DIM)
    u = c_g * h_c
    return q, k, v, u, b_g, g_a, g_c


def sink_attention(q, k, v, valid, sinks):
    s = jnp.einsum('...qkgd,...ckd->...kgqc', q, k).astype(jnp.float32) * (HEAD_DIM ** -0.5)
    s = jnp.where(valid[..., None, None, :, :], s, -jnp.inf)
    sink = sinks.astype(jnp.float32).reshape(N_KV_HEADS, GROUP, 1, 1)
    m = jnp.maximum(s.max(-1, keepdims=True), sink)
    p = jnp.exp(s - m)
    p = p / (p.sum(-1, keepdims=True) + jnp.exp(sink - m))
    return jnp.einsum('...kgqc,...ckd->...qkgd', p.astype(v.dtype), v)


def window_attention_prompt(q, k, v, sinks):
    n, s = q.shape[:2]
    nb = s // WINDOW
    qb = q.reshape(n, nb, WINDOW, N_KV_HEADS, GROUP, HEAD_DIM)

    def band(t):
        tb = t.reshape(n, nb, WINDOW, N_KV_HEADS, HEAD_DIM)
        prev = jnp.concatenate([jnp.zeros_like(tb[:, :1]), tb[:, :-1]], axis=1)
        return jnp.concatenate([prev, tb], axis=2)

    a = jnp.arange(WINDOW)[:, None]
    c = jnp.arange(2 * WINDOW)[None, :]
    blk = jnp.arange(nb)[:, None, None]
    valid = (c > a) & (c <= a + WINDOW) & ((blk > 0) | (c >= WINDOW))
    out = sink_attention(qb, band(k), band(v), valid, sinks)
    return out.reshape(n, s, Q_DIM)


def window_attention_sample(q, k_all, v_all, sinks):
    n, s = q.shape[:2]
    l = k_all.shape[1]
    qpos = PAST_LEN + jnp.arange(s)
    kpos = PAST_LEN + s - l + jnp.arange(l)
    valid = (kpos[None, :] <= qpos[:, None]) & (kpos[None, :] > qpos[:, None] - WINDOW)
    out = sink_attention(q.reshape(n, s, N_KV_HEADS, GROUP, HEAD_DIM), k_all, v_all, valid, sinks)
    return out.reshape(n, s, Q_DIM)


def causal_conv(u_ext, w):
    l = u_ext.shape[1] - (CONV_WIDTH - 1)
    out = w[0] * u_ext[:, 0:l]
    for j in range(1, CONV_WIDTH):
        out = out + w[j] * u_ext[:, j:j + l]
    return out


def route(h, w_router, router_bias):
    t = h.shape[0]
    per_group = N_EXPERTS // N_GROUPS
    scores = jax.nn.sigmoid((h @ w_router).astype(jnp.float32))
    choice = scores + router_bias.astype(jnp.float32)
    group_score = lax.top_k(choice.reshape(t, N_GROUPS, per_group), 2)[0].sum(-1)
    _, top_groups = lax.top_k(group_score, TOPK_GROUPS)
    group_keep = (top_groups[:, :, None] == jnp.arange(N_GROUPS)[None, None, :]).any(axis=1)
    expert_keep = jnp.repeat(group_keep, per_group, axis=1)
    _, idx = lax.top_k(jnp.where(expert_keep, choice, -jnp.inf), TOP_K)
    w = jnp.take_along_axis(scores, idx, axis=1)
    w = w / w.sum(-1, keepdims=True) * ROUTED_SCALE
    return idx.astype(jnp.int32), w


def routed_experts(h, idx, wts, w_gate, w_up, w_down):
    t, d = h.shape
    m = t * TOP_K
    flat_e = idx.reshape(m)
    order = jnp.argsort(flat_e)
    sorted_e = flat_e[order]
    counts = jnp.zeros((N_EXPERTS,), jnp.int32).at[flat_e].add(1)
    padded = (counts + EXPERT_BLOCK - 1) // EXPERT_BLOCK * EXPERT_BLOCK
    pad_end = jnp.cumsum(padded)
    pad_start = pad_end - padded
    start = jnp.cumsum(counts) - counts
    dest = pad_start[sorted_e] + jnp.arange(m, dtype=jnp.int32) - start[sorted_e]
    n_blocks = (m + N_EXPERTS * (EXPERT_BLOCK - 1) + EXPERT_BLOCK - 1) // EXPERT_BLOCK
    p = n_blocks * EXPERT_BLOCK
    tok_buf = jnp.full((p,), t, jnp.int32).at[dest].set((order // TOP_K).astype(jnp.int32))
    w_buf = jnp.zeros((p,), h.dtype).at[dest].set(wts.reshape(m)[order].astype(h.dtype))
    block_e = jnp.minimum(jnp.searchsorted(pad_end, jnp.arange(n_blocks, dtype=jnp.int32) * EXPERT_BLOCK, side='right'), N_EXPERTS - 1)
    h_pad = jnp.concatenate([h, jnp.zeros((1, d), h.dtype)], axis=0)

    def one_block(args):
        tok, e, w = args
        xb = h_pad[tok]
        act = jax.nn.silu(xb @ w_gate[e]) * (xb @ w_up[e])
        return (act @ w_down[e]) * w[:, None]

    ys = lax.map(one_block, (tok_buf.reshape(n_blocks, EXPERT_BLOCK), block_e, w_buf.reshape(n_blocks, EXPERT_BLOCK)))
    return jnp.zeros_like(h_pad).at[tok_buf].add(ys.reshape(p, d))[:t]


def merge_and_channel_mix(x, attn, conv_branch, g_a, g_c, w_attn_out, w_conv_out, w_o, ln1_g, ln1_b,
                          w_router, router_bias, w_exp_gate, w_exp_up, w_exp_down,
                          w_sh_gate, w_sh_up, w_sh_down, ln2_g, ln2_b):
    mixed = (jax.nn.sigmoid(g_a) * (attn @ w_attn_out) + jax.nn.sigmoid(g_c) * (conv_branch @ w_conv_out)) @ w_o
    h = layer_norm(ALPHA * x + mixed, ln1_g, ln1_b)
    n, s, d = h.shape
    hf = h.reshape(n * s, d)
    idx, wts = route(hf, w_router, router_bias)
    shared = (jax.nn.silu(hf @ w_sh_gate) * (hf @ w_sh_up)) @ w_sh_down
    ffn = routed_experts(hf, idx, wts, w_exp_gate, w_exp_up, w_exp_down) + shared
    return layer_norm(ALPHA * h + ffn.reshape(n, s, d), ln2_g, ln2_b)


def setup_inputs(seed: int = 0) -> dict:
    key = jax.random.key(seed)
    ks = jax.random.split(key, 24)
    f32 = jnp.float32

    def nrm(k, shape, scale):
        return jax.random.normal(k, shape, f32) * scale

    win_buf = min(WINDOW, PAST_LEN)
    return {
        'x_prompt': nrm(ks[0], (BATCH, SEQ, D_MODEL), 1.0),
        'x_sample': nrm(ks[1], (DEC_BATCH, DEC_SEQ, D_MODEL), 1.0),
        'cache_k': nrm(ks[2], (DEPTH, DEC_BATCH, win_buf, N_KV_HEADS, HEAD_DIM), 1.0),
        'cache_v': nrm(ks[3], (DEPTH, DEC_BATCH, win_buf, N_KV_HEADS, HEAD_DIM), 1.0),
        'state_conv': nrm(ks[4], (DEPTH, DEC_BATCH, CONV_WIDTH - 1, CONV_DIM), 1.0),
        'w_in': nrm(ks[5], (DEPTH, D_MODEL, IN_COLS), D_MODEL ** -0.5),
        'attn_sinks': nrm(ks[6], (DEPTH, N_HEADS), 0.5),
        'conv_w': nrm(ks[7], (DEPTH, CONV_WIDTH, CONV_DIM), CONV_WIDTH ** -0.5),
        'w_attn_out': nrm(ks[8], (DEPTH, Q_DIM, D_MODEL), Q_DIM ** -0.5 * BETA),
        'w_conv_out': nrm(ks[9], (DEPTH, CONV_DIM, D_MODEL), CONV_DIM ** -0.5 * BETA),
        'w_o': nrm(ks[10], (DEPTH, D_MODEL, D_MODEL), D_MODEL ** -0.5 * BETA),
        'ln1_g': 1.0 + nrm(ks[11], (DEPTH, D_MODEL), 0.02),
        'ln1_b': nrm(ks[12], (DEPTH, D_MODEL), 0.02),
        'w_router': nrm(ks[13], (DEPTH, D_MODEL, N_EXPERTS), D_MODEL ** -0.5),
        'router_bias': nrm(ks[14], (DEPTH, N_EXPERTS), 0.01),
        'w_exp_gate': nrm(ks[15], (DEPTH, N_EXPERTS, D_MODEL, EXPERT_DIM), D_MODEL ** -0.5),
        'w_exp_up': nrm(ks[16], (DEPTH, N_EXPERTS, D_MODEL, EXPERT_DIM), D_MODEL ** -0.5),
        'w_exp_down': nrm(ks[17], (DEPTH, N_EXPERTS, EXPERT_DIM, D_MODEL), EXPERT_DIM ** -0.5 * BETA),
        'w_sh_gate': nrm(ks[18], (DEPTH, D_MODEL, SHARED_DIM), D_MODEL ** -0.5),
        'w_sh_up': nrm(ks[19], (DEPTH, D_MODEL, SHARED_DIM), D_MODEL ** -0.5),
        'w_sh_down': nrm(ks[20], (DEPTH, SHARED_DIM, D_MODEL), SHARED_DIM ** -0.5 * BETA),
        'ln2_g': 1.0 + nrm(ks[21], (DEPTH, D_MODEL), 0.02),
        'ln2_b': nrm(ks[22], (DEPTH, D_MODEL), 0.02),
    }


def reference(x_prompt, x_sample, cache_k, cache_v, state_conv, w_in, attn_sinks, conv_w,
              w_attn_out, w_conv_out, w_o, ln1_g, ln1_b, w_router, router_bias,
              w_exp_gate, w_exp_up, w_exp_down, w_sh_gate, w_sh_up, w_sh_down, ln2_g, ln2_b):
    yp, ys = x_prompt, x_sample
    p_k, p_v, p_c, s_k, s_v, s_c = [], [], [], [], [], []
    for l in range(DEPTH):
        tail = (w_attn_out[l], w_conv_out[l], w_o[l], ln1_g[l], ln1_b[l], w_router[l], router_bias[l],
                w_exp_gate[l], w_exp_up[l], w_exp_down[l], w_sh_gate[l], w_sh_up[l], w_sh_down[l],
                ln2_g[l], ln2_b[l])
        n, s = yp.shape[:2]
        q, k, v, u, b_g, g_a, g_c = mixer_inputs(yp, w_in[l], jnp.arange(s))
        attn = window_attention_prompt(q, k, v, attn_sinks[l])
        u_ext = jnp.concatenate([jnp.zeros((n, CONV_WIDTH - 1, CONV_DIM), u.dtype), u], axis=1)
        conv_branch = b_g * causal_conv(u_ext, conv_w[l])
        keep = min(WINDOW, s)
        p_k.append(k[:, s - keep:])
        p_v.append(v[:, s - keep:])
        p_c.append(u_ext[:, -(CONV_WIDTH - 1):])
        yp_next = merge_and_channel_mix(yp, attn, conv_branch, g_a, g_c, *tail)
        sd = ys.shape[1]
        q, k, v, u, b_g, g_a, g_c = mixer_inputs(ys, w_in[l], PAST_LEN + jnp.arange(sd))
        k_all = jnp.concatenate([cache_k[l], k], axis=1)
        v_all = jnp.concatenate([cache_v[l], v], axis=1)
        attn = window_attention_sample(q, k_all, v_all, attn_sinks[l])
        u_ext = jnp.concatenate([state_conv[l], u], axis=1)
        conv_branch = b_g * causal_conv(u_ext, conv_w[l])
        buf = cache_k.shape[2]
        s_k.append(k_all[:, -buf:])
        s_v.append(v_all[:, -buf:])
        s_c.append(u_ext[:, -(CONV_WIDTH - 1):])
        ys = merge_and_channel_mix(ys, attn, conv_branch, g_a, g_c, *tail)
        yp = yp_next
    return (yp, ys, jnp.stack(p_k), jnp.stack(p_v), jnp.stack(p_c), jnp.stack(s_k), jnp.stack(s_v), jnp.stack(s_c))
```

```python
import functools

import jax
import jax.numpy as jnp
from jax import lax
from jax.experimental import pallas as pl
from jax.experimental.pallas import tpu as pltpu

N_HEADS = 16
N_KV_HEADS = 4
HEAD_DIM = 64
GROUP = N_HEADS // N_KV_HEADS
Q_DIM = N_HEADS * HEAD_DIM
KV_DIM = N_KV_HEADS * HEAD_DIM
ROT_DIM = HEAD_DIM // 4
ROPE_THETA = 500000.0
WINDOW = 128
CONV_WIDTH = 3
PAST_LEN = 16384
N_EXPERTS = 64
N_GROUPS = 8
TOPK_GROUPS = 4
TOP_K = 8
ROUTED_SCALE = 2.5
EXPERT_BLOCK = 128
LN_EPS = 1e-5

LANES = 128
SUBLANES = 8
VMEM_LIMIT_BYTES = 48 * 1024 * 1024
NEG_BIG = -1e30

F32 = jnp.float32
BF16 = jnp.bfloat16
I32 = jnp.int32


def _tile(n, pref):
    t = pref
    while t > 1 and n % t:
        t //= 2
    return t


def _params(*sem):
    return pltpu.CompilerParams(dimension_semantics=sem, vmem_limit_bytes=VMEM_LIMIT_BYTES)


def _dot(a, b):
    return jnp.dot(a, b, preferred_element_type=F32)


def _dot_nt(a, b):
    return lax.dot_general(a, b, (((1,), (1,)), ((), ())), preferred_element_type=F32)


def _layer_norm(x, g, b):
    mu = jnp.mean(x, axis=-1, keepdims=True)
    xc = x - mu
    var = jnp.mean(xc * xc, axis=-1, keepdims=True)
    return xc * lax.rsqrt(var + LN_EPS) * g + b


def _pack_pair(lo, hi):
    lo32 = lax.bitcast_convert_type(lo.astype(F32), I32)
    hi32 = lax.bitcast_convert_type(hi.astype(F32), I32)
    return lax.shift_right_logical(lo32, 16) | (hi32 & -65536)


def _unpack_pair(p):
    lo = lax.bitcast_convert_type(lax.shift_left(p, 16), F32)
    hi = lax.bitcast_convert_type(p & -65536, F32)
    return lo, hi


def _rope_tables(pos):
    half = ROT_DIM // 2
    n = pos.shape[0]
    inv_freq = jnp.power(jnp.float32(ROPE_THETA), -jnp.arange(half, dtype=F32) * (2.0 / ROT_DIM))
    ang = pos.astype(F32)[:, None] * inv_freq[None, :]
    cos, sin = jnp.cos(ang), jnp.sin(ang)
    rest = HEAD_DIM - ROT_DIM
    cos_h = jnp.concatenate([cos, cos, jnp.ones((n, rest), F32)], axis=1)
    sa_h = jnp.concatenate([jnp.zeros((n, half), F32), sin, jnp.zeros((n, rest), F32)], axis=1)
    sb_h = jnp.concatenate([-sin, jnp.zeros((n, half + rest), F32)], axis=1)
    rep = LANES // HEAD_DIM
    return tuple(jnp.concatenate([t] * rep, axis=1) for t in (cos_h, sa_h, sb_h))


def _qkv_body(x_ref, w_ref, cos_ref, sa_ref, sb_ref, q_ref, k_ref, v_ref, xb_ref, *, nq, nk, tn):
    j = pl.program_id(1)

    @pl.when(j == 0)
    def _():
        xb_ref[...] = x_ref[...].astype(BF16)

    acc = _dot(xb_ref[...], w_ref[...])
    half = ROT_DIM // 2

    def rope(a):
        cos, sa, sb = cos_ref[...], sa_ref[...], sb_ref[...]
        outs = []
        for c in range(tn // LANES):
            blk = a[:, c * LANES:(c + 1) * LANES]
            outs.append(blk * cos + pltpu.roll(blk, half, 1) * sa + pltpu.roll(blk, LANES - half, 1) * sb)
        return jnp.concatenate(outs, axis=1)

    @pl.when(j < nq)
    def _():
        q_ref[...] = (rope(acc) * (HEAD_DIM ** -0.5)).astype(BF16)

    @pl.when((j >= nq) & (j < nq + nk))
    def _():
        k_ref[...] = rope(acc)

    @pl.when(j >= nq + nk)
    def _():
        v_ref[...] = acc


def _qkv_call(x, w_qkv, tables, tm):
    t, d = x.shape
    tn = 2 * LANES
    nq, nk, nv = Q_DIM // tn, KV_DIM // tn, KV_DIM // tn
    tab_blocks = tables[0].shape[0] // tm
    tab_spec = pl.BlockSpec((tm, LANES), lambda i, j: (i % tab_blocks, 0))
    return pl.pallas_call(
        functools.partial(_qkv_body, nq=nq, nk=nk, tn=tn),
        grid=(t // tm, nq + nk + nv),
        in_specs=[
            pl.BlockSpec((tm, d), lambda i, j: (i, 0)),
            pl.BlockSpec((d, tn), lambda i, j: (0, j)),
            tab_spec, tab_spec, tab_spec,
        ],
        out_specs=[
            pl.BlockSpec((tm, tn), lambda i, j: (i, jnp.minimum(j, nq - 1))),
            pl.BlockSpec((tm, tn), lambda i, j: (i, jnp.clip(j - nq, 0, nk - 1))),
            pl.BlockSpec((tm, tn), lambda i, j: (i, jnp.clip(j - nq - nk, 0, nv - 1))),
        ],
        out_shape=[
            jax.ShapeDtypeStruct((t, Q_DIM), BF16),
            jax.ShapeDtypeStruct((t, KV_DIM), F32),
            jax.ShapeDtypeStruct((t, KV_DIM), F32),
        ],
        scratch_shapes=[pltpu.VMEM((tm, d), BF16)],
        compiler_params=_params("parallel", "arbitrary"),
        name="qkv",
    )(x, w_qkv, *tables)


def _head_pair_operands(kv_chunk, odd):
    lane = lax.broadcasted_iota(I32, kv_chunk.shape, 1)
    own = jnp.where((lane >= HEAD_DIM) == odd, kv_chunk, 0.0)
    other = pltpu.roll(own, HEAD_DIM, 1)
    lo, hi = (other, own) if odd else (own, other)
    return lo.astype(BF16), hi.astype(BF16)


def _sink_softmax(s, valid, sink):
    s = jnp.where(valid, s, NEG_BIG)
    m = jnp.maximum(jnp.max(s, axis=-1, keepdims=True), sink)
    p = jnp.exp(s - m)
    den = jnp.sum(p, axis=-1, keepdims=True) + jnp.exp(sink - m)
    return (p / den).astype(BF16)


def _attend(q, kk, vv, valid, sinks_ref, rows):
    chunks = []
    for kh in range(N_KV_HEADS):
        c = (kh * HEAD_DIM) // LANES
        odd = bool((kh * HEAD_DIM) % LANES)
        k_lo, k_hi = _head_pair_operands(kk[:, c * LANES:(c + 1) * LANES], odd)
        v_lo, v_hi = _head_pair_operands(vv[:, c * LANES:(c + 1) * LANES], odd)
        v_both = jnp.concatenate([v_lo, v_hi], axis=0)
        h0 = kh * GROUP
        qc = [q[:, (h0 // 2 + i) * LANES:(h0 // 2 + i + 1) * LANES] for i in range(GROUP // 2)]
        q_st = jnp.concatenate(qc, axis=0)
        ps = []
        for par, k_op in ((0, k_lo), (1, k_hi)):
            s = _dot_nt(q_st, k_op)
            sink = jnp.concatenate(
                [jnp.full((rows, 1), sinks_ref[h0 + 2 * i + par], F32) for i in range(GROUP // 2)], axis=0)
            ps.append(_sink_softmax(s, valid, sink))
        for i in range(GROUP // 2):
            p_both = jnp.concatenate([ps[0][i * rows:(i + 1) * rows], ps[1][i * rows:(i + 1) * rows]], axis=1)
            chunks.append(_dot(p_both, v_both))
    return jnp.concatenate(chunks, axis=1)


def _attn_prompt_body(sinks_ref, q_ref, kp_ref, kc_ref, vp_ref, vc_ref, o_ref):
    n = pl.program_id(1)
    w = WINDOW
    kk = jnp.concatenate([kp_ref[...], kc_ref[...]], axis=0)
    vv = jnp.concatenate([vp_ref[...], vc_ref[...]], axis=0)
    a = jnp.concatenate([lax.broadcasted_iota(I32, (w, 2 * w), 0)] * (GROUP // 2), axis=0)
    c = lax.broadcasted_iota(I32, ((GROUP // 2) * w, 2 * w), 1)
    valid = (c > a) & (c <= a + w) & ((n > 0) | (c >= w))
    o_ref[...] = _attend(q_ref[...], kk, vv, valid, sinks_ref, w).astype(BF16)


def _attn_prompt_call(q, k, v, sinks, batch, seq):
    w = WINDOW
    nb = seq // w
    cur = lambda b, n: (b * nb + n, 0)
    prev = lambda b, n: (b * nb + jnp.maximum(n - 1, 0), 0)
    return pl.pallas_call(
        _attn_prompt_body,
        grid=(batch, nb),
        in_specs=[
            pl.BlockSpec(memory_space=pltpu.SMEM),
            pl.BlockSpec((w, Q_DIM), cur),
            pl.BlockSpec((w, KV_DIM), prev),
            pl.BlockSpec((w, KV_DIM), cur),
            pl.BlockSpec((w, KV_DIM), prev),
            pl.BlockSpec((w, KV_DIM), cur),
        ],
        out_specs=pl.BlockSpec((w, Q_DIM), cur),
        out_shape=jax.ShapeDtypeStruct((batch * seq, Q_DIM), BF16),
        compiler_params=_params("parallel", "parallel"),
        name="attn_prompt",
    )(sinks, q, k, k, v, v)


def _attn_sample_body(sinks_ref, q_ref, k_ref, v_ref, o_ref, *, bt, nkeys):
    kk = k_ref[...].reshape(bt * nkeys, KV_DIM)
    vv = v_ref[...].reshape(bt * nkeys, KV_DIM)
    rows = (GROUP // 2) * bt
    row_b = jnp.concatenate([lax.broadcasted_iota(I32, (bt, bt * nkeys), 0)] * (GROUP // 2), axis=0)
    key_b = jnp.concatenate([jnp.full((rows, nkeys), b, I32) for b in range(bt)], axis=1)
    o_ref[...] = _attend(q_ref[...], kk, vv, row_b == key_b, sinks_ref, bt).astype(BF16)


def _attn_sample_call(q, k_win, v_win, sinks):
    b, nkeys, _ = k_win.shape
    bt = _tile(b, SUBLANES)
    return pl.pallas_call(
        functools.partial(_attn_sample_body, bt=bt, nkeys=nkeys),
        grid=(b // bt,),
        in_specs=[
            pl.BlockSpec(memory_space=pltpu.SMEM),
            pl.BlockSpec((bt, Q_DIM), lambda i: (i, 0)),
            pl.BlockSpec((bt, nkeys, KV_DIM), lambda i: (i, 0, 0)),
            pl.BlockSpec((bt, nkeys, KV_DIM), lambda i: (i, 0, 0)),
        ],
        out_specs=pl.BlockSpec((bt, Q_DIM), lambda i: (i, 0)),
        out_shape=jax.ShapeDtypeStruct((b, Q_DIM), BF16),
        compiler_params=_params("parallel"),
        name="attn_sample",
    )(sinks, q, k_win, v_win)


def _conv_body(*refs, decode, tiles_per_seq, tm, tail):
    if decode:
        x_ref, wb_ref, wc_ref, wh_ref, cw_ref, s0_ref, s1_ref, cb_ref, ut_ref, xb_ref = refs
    else:
        x_ref, wb_ref, wc_ref, wh_ref, cw_ref, cb_ref, ut_ref, xb_ref, carry_ref = refs
    i = pl.program_id(0)
    c = pl.program_id(1)

    @pl.when(c == 0)
    def _():
        xb_ref[...] = x_ref[...].astype(BF16)

    xb = xb_ref[...]
    b_g = _dot(xb, wb_ref[...])
    u = _dot(xb, wc_ref[...]) * _dot(xb, wh_ref[...])
    if decode:
        u_m1, u_m2 = s1_ref[...], s0_ref[...]
    else:
        @pl.when(i % tiles_per_seq == 0)
        def _():
            carry_ref[c] = jnp.zeros(carry_ref.shape[1:], F32)

        prev = carry_ref[c]
        p_m2, p_m1 = prev[SUBLANES - 2:SUBLANES - 1], prev[SUBLANES - 1:SUBLANES]
        r = lax.broadcasted_iota(I32, u.shape, 0)
        u_m1 = jnp.where(r == 0, p_m1, pltpu.roll(u, 1, 0))
        u_m2 = jnp.where(r == 0, p_m2, jnp.where(r == 1, p_m1, pltpu.roll(u, 2, 0)))
        carry_ref[c] = u[tm - SUBLANES:]
    cw = cw_ref[...]
    conv = cw[0:1] * u_m2 + cw[1:2] * u_m1 + cw[2:3] * u
    cb_ref[...] = (b_g * conv).astype(BF16)
    ut_ref[...] = u[tm - tail:]


def _conv_call(x, w_b, w_c, w_h, conv_w, tm, seq_rows, state=None):
    t, d = x.shape
    cdim = w_b.shape[1]
    tc = _tile(cdim, 2 * LANES)
    decode = state is not None
    tail = tm if decode else SUBLANES
    w_spec = pl.BlockSpec((d, tc), lambda i, c: (0, c))
    in_specs = [pl.BlockSpec((tm, d), lambda i, c: (i, 0)), w_spec, w_spec, w_spec,
                pl.BlockSpec((CONV_WIDTH, tc), lambda i, c: (0, c))]
    args = [x, w_b, w_c, w_h, conv_w]
    scratch = [pltpu.VMEM((tm, d), BF16)]
    if decode:
        in_specs += [pl.BlockSpec((tm, tc), lambda i, c: (i, c))] * 2
        args += list(state)
    else:
        scratch.append(pltpu.VMEM((cdim // tc, SUBLANES, tc), F32))
    return pl.pallas_call(
        functools.partial(_conv_body, decode=decode, tiles_per_seq=max(seq_rows // tm, 1), tm=tm, tail=tail),
        grid=(t // tm, cdim // tc),
        in_specs=in_specs,
        out_specs=[pl.BlockSpec((tm, tc), lambda i, c: (i, c)), pl.BlockSpec((tail, tc), lambda i, c: (i, c))],
        out_shape=[jax.ShapeDtypeStruct((t, cdim), BF16), jax.ShapeDtypeStruct((t // tm * tail, cdim), F32)],
        scratch_shapes=scratch,
        compiler_params=_params("arbitrary", "arbitrary"),
        name="conv",
    )(*args)


def _gate_body(x_ref, at_ref, cb_ref, wga_ref, wgc_ref, wa_ref, wco_ref, o_ref, xb_ref):
    @pl.when(pl.program_id(1) == 0)
    def _():
        xb_ref[...] = x_ref[...].astype(BF16)

    xb = xb_ref[...]
    g_a = _dot(xb, wga_ref[...])
    g_c = _dot(xb, wgc_ref[...])
    a = _dot(at_ref[...], wa_ref[...])
    c = _dot(cb_ref[...], wco_ref[...])
    o_ref[...] = (jax.nn.sigmoid(g_a) * a + jax.nn.sigmoid(g_c) * c).astype(BF16)


def _gate_call(x, attn, cb, w_ga, w_gc, w_a, w_co, tm):
    t, d = x.shape
    tn = _tile(d, 4 * LANES)
    row = lambda i, n: (i, 0)
    col = lambda i, n: (0, n)
    return pl.pallas_call(
        _gate_body,
        grid=(t // tm, d // tn),
        in_specs=[
            pl.BlockSpec((tm, d), row),
            pl.BlockSpec((tm, attn.shape[1]), row),
            pl.BlockSpec((tm, cb.shape[1]), row),
            pl.BlockSpec((d, tn), col),
            pl.BlockSpec((d, tn), col),
            pl.BlockSpec((w_a.shape[0], tn), col),
            pl.BlockSpec((w_co.shape[0], tn), col),
        ],
        out_specs=pl.BlockSpec((tm, tn), lambda i, n: (i, n)),
        out_shape=jax.ShapeDtypeStruct((t, d), BF16),
        scratch_shapes=[pltpu.VMEM((tm, d), BF16)],
        compiler_params=_params("parallel", "arbitrary"),
        name="gate",
    )(x, attn, cb, w_ga, w_gc, w_a, w_co)


def _post_body(x_ref, mp_ref, wo_ref, g_ref, b_ref, wr_ref, wsg_ref, wsu_ref, wsd_ref,
               hq_ref, base_ref, lg_ref, *, alpha):
    mixed = _dot(mp_ref[...], wo_ref[...])
    h = _layer_norm(alpha * x_ref[...] + mixed, g_ref[...], b_ref[...])
    hb = h.astype(BF16)
    lg_ref[...] = _dot_nt(wr_ref[...], hb)
    act = jax.nn.silu(_dot(hb, wsg_ref[...])) * _dot(hb, wsu_ref[...])
    base_ref[...] = alpha * h + _dot(act.astype(BF16), wsd_ref[...])
    half = hb.shape[1] // 2
    hq_ref[...] = _pack_pair(hb[:, :half], hb[:, half:])


def _post_call(x, mp, w_o, ln_g, ln_b, w_r_t, w_sg, w_su, w_sd, tm, alpha):
    t, d = x.shape
    n_e, f = w_r_t.shape[0], w_sg.shape[1]
    row = lambda i: (i, 0)
    full = lambda i: (0, 0)
    return pl.pallas_call(
        functools.partial(_post_body, alpha=alpha),
        grid=(t // tm,),
        in_specs=[
            pl.BlockSpec((tm, d), row),
            pl.BlockSpec((tm, d), row),
            pl.BlockSpec((d, d), full),
            pl.BlockSpec((1, d), full),
            pl.BlockSpec((1, d), full),
            pl.BlockSpec((n_e, d), full),
            pl.BlockSpec((d, f), full),
            pl.BlockSpec((d, f), full),
            pl.BlockSpec((f, d), full),
        ],
        out_specs=[
            pl.BlockSpec((tm, d // 2), row),
            pl.BlockSpec((tm, d), row),
            pl.BlockSpec((n_e, tm), lambda i: (0, i)),
        ],
        out_shape=[
            jax.ShapeDtypeStruct((t, d // 2), I32),
            jax.ShapeDtypeStruct((t, d), F32),
            jax.ShapeDtypeStruct((n_e, t), F32),
        ],
        compiler_params=_params("parallel"),
        name="post",
    )(x, mp, w_o, ln_g, ln_b, w_r_t, w_sg, w_su, w_sd)


def _route_body(bias_ref, lg_ref, idx_ref, w_ref):
    per_group = N_EXPERTS // N_GROUPS
    neg_inf = jnp.float32(-jnp.inf)
    scores = [jax.nn.sigmoid(lg_ref[e]) for e in range(N_EXPERTS)]
    choice = [scores[e] + bias_ref[e] for e in range(N_EXPERTS)]

    group_score = []
    for g in range(N_GROUPS):
        vals = choice[g * per_group:(g + 1) * per_group]
        m1 = functools.reduce(jnp.maximum, vals)
        m2 = jnp.full_like(m1, neg_inf)
        found = jnp.zeros(m1.shape, jnp.bool_)
        for v in vals:
            eq = v == m1
            m2 = jnp.maximum(m2, jnp.where(eq & ~found, neg_inf, v))
            found = found | eq
        group_score.append(m1 + m2)

    masked = []
    for g in range(N_GROUPS):
        ahead = jnp.zeros(group_score[g].shape, I32)
        for o in range(N_GROUPS):
            if o == g:
                continue
            beats = group_score[o] > group_score[g]
            if o < g:
                beats = beats | (group_score[o] == group_score[g])
            ahead = ahead + beats.astype(I32)
        keep = ahead < TOPK_GROUPS
        masked += [jnp.where(keep, choice[e], neg_inf) for e in range(g * per_group, (g + 1) * per_group)]

    picked_w = []
    for r in range(TOP_K):
        best = functools.reduce(jnp.maximum, masked)
        sel = jnp.full(best.shape, N_EXPERTS, I32)
        for e in reversed(range(N_EXPERTS)):
            sel = jnp.where(masked[e] == best, e, sel)
        w = jnp.zeros(best.shape, F32)
        for e in range(N_EXPERTS):
            hit = sel == e
            w = jnp.where(hit, scores[e], w)
            masked[e] = jnp.where(hit, neg_inf, masked[e])
        idx_ref[r] = sel
        picked_w.append(w)
    total = functools.reduce(lambda a, b: a + b, picked_w)
    for r in range(TOP_K):
        w_ref[r] = picked_w[r] / total * ROUTED_SCALE


def _route_call(logits_t, bias):
    n_e, t = logits_t.shape
    rows = t // LANES
    lg3 = logits_t.reshape(n_e, rows, LANES)
    idx, w = pl.pallas_call(
        _route_body,
        grid=(rows // SUBLANES,),
        in_specs=[
            pl.BlockSpec(memory_space=pltpu.SMEM),
            pl.BlockSpec((n_e, SUBLANES, LANES), lambda i: (0, i, 0)),
        ],
        out_specs=[
            pl.BlockSpec((TOP_K, SUBLANES, LANES), lambda i: (0, i, 0)),
            pl.BlockSpec((TOP_K, SUBLANES, LANES), lambda i: (0, i, 0)),
        ],
        out_shape=[
            jax.ShapeDtypeStruct((TOP_K, rows, LANES), I32),
            jax.ShapeDtypeStruct((TOP_K, rows, LANES), F32),
        ],
        compiler_params=_params("parallel"),
        name="route",
    )(bias, lg3)
    return idx.reshape(TOP_K, t), w.reshape(TOP_K, t)


def _moe_body(be_ref, tok_ref, tokn_ref, dst_ref, hq_hbm, wg_ref, wu_ref, wd_ref, out_hbm,
              xbuf, ybuf, wgb, wub, wdb, gsem, ssem):
    b = pl.program_id(0)
    nb = pl.num_programs(0)
    slot = b % 2
    rows = EXPERT_BLOCK

    def gather(idx_ref, s):
        for r in range(rows):
            pltpu.make_async_copy(hq_hbm.at[pl.ds(idx_ref[0, 0, r], 1), :],
                                  xbuf.at[s, pl.ds(r, 1), :], gsem.at[s]).start()

    def whole_block_copy(s):
        return pltpu.make_async_copy(hq_hbm.at[pl.ds(0, rows), :], xbuf.at[s], gsem.at[s])

    def scatter_done(s):
        return pltpu.make_async_copy(ybuf.at[s], out_hbm.at[pl.ds(0, rows), :], ssem.at[s])

    @pl.when(b == 0)
    def _():
        gather(tok_ref, 0)

    @pl.when(b + 1 < nb)
    def _():
        gather(tokn_ref, 1 - slot)

    whole_block_copy(slot).wait()

    @pl.when((b == 0) | (be_ref[b] != be_ref[jnp.maximum(b - 1, 0)]))
    def _():
        wgb[...] = wg_ref[...].astype(BF16)
        wub[...] = wu_ref[...].astype(BF16)
        wdb[...] = wd_ref[...].astype(BF16)

    lo, hi = _unpack_pair(xbuf[slot])
    xb = jnp.concatenate([lo, hi], axis=1).astype(BF16)
    act = jax.nn.silu(_dot(xb, wgb[...])) * _dot(xb, wub[...])
    y = _dot(act.astype(BF16), wdb[...]).astype(BF16)
    half = y.shape[1] // 2

    @pl.when(b >= 2)
    def _():
        scatter_done(slot).wait()

    ybuf[slot] = _pack_pair(y[:, :half], y[:, half:])
    for r in range(rows):
        pltpu.make_async_copy(ybuf.at[slot, pl.ds(r, 1), :],
                              out_hbm.at[pl.ds(dst_ref[0, 0, r], 1), :], ssem.at[slot]).start()

    @pl.when(b == nb - 1)
    def _():
        scatter_done(slot).wait()

        @pl.when(nb >= 2)
        def _():
            scatter_done(1 - slot).wait()


def _moe_call(block_e, tok, dst, hq, w_gate, w_up, w_down, out_rows):
    n_blocks = block_e.shape[0]
    n_e, d, f = w_gate.shape
    rows = EXPERT_BLOCK
    smem_blk = lambda off: pl.BlockSpec(
        (1, 1, rows), lambda b, be: (jnp.minimum(b + off, n_blocks - 1), 0, 0), memory_space=pltpu.SMEM)
    return pl.pallas_call(
        _moe_body,
        grid_spec=pltpu.PrefetchScalarGridSpec(
            num_scalar_prefetch=1,
            grid=(n_blocks,),
            in_specs=[
                smem_blk(0), smem_blk(1), smem_blk(0),
                pl.BlockSpec(memory_space=pl.ANY),
                pl.BlockSpec((None, d, f), lambda b, be: (be[b], 0, 0)),
                pl.BlockSpec((None, d, f), lambda b, be: (be[b], 0, 0)),
                pl.BlockSpec((None, f, d), lambda b, be: (be[b], 0, 0)),
            ],
            out_specs=pl.BlockSpec(memory_space=pl.ANY),
            scratch_shapes=[
                pltpu.VMEM((2, rows, d // 2), I32),
                pltpu.VMEM((2, rows, d // 2), I32),
                pltpu.VMEM((d, f), BF16),
                pltpu.VMEM((d, f), BF16),
                pltpu.VMEM((f, d), BF16),
                pltpu.SemaphoreType.DMA((2,)),
                pltpu.SemaphoreType.DMA((2,)),
            ],
        ),
        out_shape=jax.ShapeDtypeStruct((out_rows, d // 2), I32),
        compiler_params=_params("arbitrary"),
        name="moe",
    )(block_e, tok, tok, dst, hq, w_gate, w_up, w_down)


def _dispatch_plan(idx, n_tok):
    m = n_tok * TOP_K
    n_blocks = (m + N_EXPERTS * (EXPERT_BLOCK - 1) + EXPERT_BLOCK - 1) // EXPERT_BLOCK
    p = n_blocks * EXPERT_BLOCK
    spare = p - m
    plane = n_tok + -(-spare // TOP_K)
    plane = -(-plane // SUBLANES) * SUBLANES
    flat_e = idx.T.reshape(m)
    order = jnp.argsort(flat_e).astype(I32)
    sorted_e = flat_e[order]
    counts = jnp.sum((flat_e[:, None] == jnp.arange(N_EXPERTS, dtype=I32)[None, :]).astype(I32), axis=0)
    padded = (counts + EXPERT_BLOCK - 1) // EXPERT_BLOCK * EXPERT_BLOCK
    pad_end = jnp.cumsum(padded)
    pad_start = pad_end - padded
    start = jnp.cumsum(counts) - counts
    dest = pad_start[sorted_e] + jnp.arange(m, dtype=I32) - start[sorted_e]
    assign = jnp.full((p,), -1, I32).at[dest].set(order)
    filler = assign < 0
    spare_rank = jnp.cumsum(filler.astype(I32)) - 1
    t_of = assign // TOP_K
    j_of = assign % TOP_K
    tok = jnp.where(filler, 0, t_of)
    dst = jnp.where(filler, (spare_rank % TOP_K) * plane + n_tok + spare_rank // TOP_K, j_of * plane + t_of)
    block_e = jnp.minimum(
        jnp.searchsorted(pad_end, jnp.arange(n_blocks, dtype=I32) * EXPERT_BLOCK, side='right'),
        N_EXPERTS - 1).astype(I32)
    shape = (n_blocks, 1, EXPERT_BLOCK)
    return block_e, tok.reshape(shape), dst.reshape(shape), plane


def _final_body(base_ref, y8_ref, w_ref, g_ref, b_ref, o_ref):
    w = w_ref[...]
    ffn = None
    for j in range(TOP_K):
        lo, hi = _unpack_pair(y8_ref[j])
        yj = jnp.concatenate([lo, hi], axis=1) * w[:, j:j + 1]
        ffn = yj if ffn is None else ffn + yj
    o_ref[...] = _layer_norm(base_ref[...] + ffn, g_ref[...], b_ref[...])


def _final_call(base, y8, w8, ln_g, ln_b, tm, row_offset):
    t, d = base.shape
    off = row_offset // tm
    return pl.pallas_call(
        _final_body,
        grid=(t // tm,),
        in_specs=[
            pl.BlockSpec((tm, d), lambda i: (i, 0)),
            pl.BlockSpec((TOP_K, tm, d // 2), lambda i: (0, i + off, 0)),
            pl.BlockSpec((tm, TOP_K), lambda i: (i + off, 0)),
            pl.BlockSpec((1, d), lambda i: (0, 0)),
            pl.BlockSpec((1, d), lambda i: (0, 0)),
        ],
        out_specs=pl.BlockSpec((tm, d), lambda i: (i, 0)),
        out_shape=jax.ShapeDtypeStruct((t, d), F32),
        compiler_params=_params("parallel"),
        name="final",
    )(base, y8, w8, ln_g, ln_b)


def _mixer_and_post(x, pos_tables, w, tm, seq_rows, attend, conv_state, alpha):
    q, k, v = _qkv_call(x, w['qkv'], pos_tables, tm)
    attn = attend(q, k, v)
    cb, u_tail = _conv_call(x, w['b'], w['c'], w['h'], w['conv'], tm, seq_rows, conv_state)
    mp = _gate_call(x, attn, cb, w['ga'], w['gc'], w['attn_out'], w['conv_out'], tm)
    hq, base, logits_t = _post_call(x, mp, w['o'], w['ln1_g'], w['ln1_b'], w['router_t'],
                                    w['sh_gate'], w['sh_up'], w['sh_down'], min(tm, 2 * LANES), alpha)
    return k, v, u_tail, hq, base, logits_t


def kernel(x_prompt, x_sample, cache_k, cache_v, state_conv, w_in, attn_sinks, conv_w, w_attn_out, w_conv_out, w_o, ln1_g, ln1_b, w_router, router_bias, w_exp_gate, w_exp_up, w_exp_down, w_sh_gate, w_sh_up, w_sh_down, ln2_g, ln2_b):
    depth, d, _ = w_in.shape
    batch, seq, _ = x_prompt.shape
    dec_batch, dec_seq, _ = x_sample.shape
    win_buf = cache_k.shape[2]
    cdim = conv_w.shape[2]
    assert dec_seq == 1 and win_buf == WINDOW and seq % WINDOW == 0
    alpha = (2 * depth) ** 0.25
    t_p, t_s = batch * seq, dec_batch * dec_seq
    tm_p, tm_s = _tile(seq, 4 * LANES), _tile(t_s, LANES)
    assert t_p % tm_s == 0 and t_s % tm_s == 0
    tab_p = _rope_tables(jnp.arange(seq))
    tab_s = _rope_tables(jnp.full((tm_s,), PAST_LEN, I32))

    yp = x_prompt.reshape(t_p, d)
    ys = x_sample.reshape(t_s, d)
    p_k, p_v, p_c, s_k, s_v, s_c = [], [], [], [], [], []
    for l in range(depth):
        wl = w_in[l].astype(BF16)
        o = 0
        w = {}
        for name, width in (('qkv', Q_DIM + 2 * KV_DIM), ('b', cdim), ('c', cdim), ('h', cdim), ('ga', d), ('gc', d)):
            w[name] = wl[:, o:o + width]
            o += width
        w.update(
            conv=conv_w[l], attn_out=w_attn_out[l].astype(BF16), conv_out=w_conv_out[l].astype(BF16),
            o=w_o[l].astype(BF16), ln1_g=ln1_g[l][None], ln1_b=ln1_b[l][None],
            router_t=w_router[l].T.astype(BF16), sh_gate=w_sh_gate[l].astype(BF16),
            sh_up=w_sh_up[l].astype(BF16), sh_down=w_sh_down[l].astype(BF16))
        sinks = attn_sinks[l]

        k, v, u_tail, hq_p, base_p, lg_p = _mixer_and_post(
            yp, tab_p, w, tm_p, seq,
            lambda q, k, v: _attn_prompt_call(q, k, v, sinks, batch, seq), None, alpha)
        keep = min(WINDOW, seq)
        p_k.append(k.reshape(batch, seq, N_KV_HEADS, HEAD_DIM)[:, seq - keep:])
        p_v.append(v.reshape(batch, seq, N_KV_HEADS, HEAD_DIM)[:, seq - keep:])
        tails = u_tail.reshape(batch, seq // tm_p, SUBLANES, cdim)
        p_c.append(tails[:, -1, SUBLANES - (CONV_WIDTH - 1):])

        new_kv = {}

        def attend_sample(q, k, v, l=l):
            new_kv['k'] = jnp.concatenate([cache_k[l][:, 1:], k.reshape(t_s, 1, N_KV_HEADS, HEAD_DIM)], axis=1)
            new_kv['v'] = jnp.concatenate([cache_v[l][:, 1:], v.reshape(t_s, 1, N_KV_HEADS, HEAD_DIM)], axis=1)
            return _attn_sample_call(q, new_kv['k'].reshape(t_s, win_buf, KV_DIM),
                                     new_kv['v'].reshape(t_s, win_buf, KV_DIM), sinks)

        state = (state_conv[l][:, 0], state_conv[l][:, 1])
        _, _, u_s, hq_s, base_s, lg_s = _mixer_and_post(ys, tab_s, w, tm_s, 1, attend_sample, state, alpha)
        s_k.append(new_kv['k'])
        s_v.append(new_kv['v'])
        s_c.append(jnp.concatenate([state_conv[l][:, 1:], u_s[:, None]], axis=1))

        n_tok = t_p + t_s
        route_tile = SUBLANES * LANES
        t_pad = -(-n_tok // route_tile) * route_tile
        logits_t = jnp.concatenate([lg_p, lg_s, jnp.zeros((N_EXPERTS, t_pad - n_tok), F32)], axis=1)
        idx, wts = _route_call(logits_t, router_bias[l])
        idx, wts = idx[:, :n_tok], wts[:, :n_tok]
        block_e, tok, dst, plane = _dispatch_plan(idx, n_tok)
        hq = jnp.concatenate([hq_p, hq_s], axis=0)
        y8 = _moe_call(block_e, tok, dst, hq, w_exp_gate[l], w_exp_up[l], w_exp_down[l], TOP_K * plane)
        y8 = y8.reshape(TOP_K, plane, d // 2)
        w8 = wts.T
        yp = _final_call(base_p, y8, w8, ln2_g[l][None], ln2_b[l][None], min(tm_p, 2 * LANES), 0)
        ys = _final_call(base_s, y8, w8, ln2_g[l][None], ln2_b[l][None], tm_s, t_p)

    return (yp.reshape(batch, seq, d), ys.reshape(dec_batch, dec_seq, d), jnp.stack(p_k), jnp.stack(p_v),
            jnp.stack(p_c), jnp.stack(s_k), jnp.stack(s_v), jnp.stack(s_c))
```

```python
import functools

import jax
import jax.numpy as jnp
from jax import lax
from jax.experimental import pallas as pl
from jax.experimental.pallas import tpu as pltpu

N_HEADS = 16
N_KV_HEADS = 4
HEAD_DIM = 64
GROUP = N_HEADS // N_KV_HEADS
Q_DIM = N_HEADS * HEAD_DIM
KV_DIM = N_KV_HEADS * HEAD_DIM
ROT_DIM = HEAD_DIM // 4
ROPE_THETA = 500000.0
WINDOW = 128
CONV_WIDTH = 3
PAST_LEN = 16384
N_EXPERTS = 64
N_GROUPS = 8
TOPK_GROUPS = 4
TOP_K = 8
ROUTED_SCALE = 2.5
LN_EPS = 1e-5
MOE_BLOCK_ROWS = 256

LANES = 128
SUBLANES = 8
VMEM_LIMIT_BYTES = 48 * 1024 * 1024
NEG_BIG = -1e30

F32 = jnp.float32
BF16 = jnp.bfloat16
I32 = jnp.int32


def _tile(n, pref):
    t = pref
    while t > 1 and n % t:
        t //= 2
    return t


def _params(*sem):
    return pltpu.CompilerParams(dimension_semantics=sem, vmem_limit_bytes=VMEM_LIMIT_BYTES)


def _dot(a, b):
    return jnp.dot(a, b, preferred_element_type=F32)


def _dot_nt(a, b):
    return lax.dot_general(a, b, (((1,), (1,)), ((), ())), preferred_element_type=F32)


def _layer_norm(x, g, b):
    mu = jnp.mean(x, axis=-1, keepdims=True)
    xc = x - mu
    var = jnp.mean(xc * xc, axis=-1, keepdims=True)
    return xc * lax.rsqrt(var + LN_EPS) * g + b


def _pack_pair(lo, hi):
    lo32 = lax.bitcast_convert_type(lo.astype(F32), I32)
    hi32 = lax.bitcast_convert_type(hi.astype(F32), I32)
    return lax.shift_right_logical(lo32, 16) | (hi32 & -65536)


def _unpack_pair(p):
    lo = lax.bitcast_convert_type(lax.shift_left(p, 16), F32)
    hi = lax.bitcast_convert_type(p & -65536, F32)
    return lo, hi


def _rope_tables(pos):
    half = ROT_DIM // 2
    n = pos.shape[0]
    inv_freq = jnp.power(jnp.float32(ROPE_THETA), -jnp.arange(half, dtype=F32) * (2.0 / ROT_DIM))
    ang = pos.astype(F32)[:, None] * inv_freq[None, :]
    cos, sin = jnp.cos(ang), jnp.sin(ang)
    rest = HEAD_DIM - ROT_DIM
    cos_h = jnp.concatenate([cos, cos, jnp.ones((n, rest), F32)], axis=1)
    sa_h = jnp.concatenate([jnp.zeros((n, half), F32), sin, jnp.zeros((n, rest), F32)], axis=1)
    sb_h = jnp.concatenate([-sin, jnp.zeros((n, half + rest), F32)], axis=1)
    rep = LANES // HEAD_DIM
    return tuple(jnp.concatenate([t] * rep, axis=1) for t in (cos_h, sa_h, sb_h))


def _qkv_body(x_ref, w_ref, cos_ref, sa_ref, sb_ref, q_ref, k_ref, v_ref, xb_ref, *, nq, nk, tn):
    j = pl.program_id(1)

    @pl.when(j == 0)
    def _():
        xb_ref[...] = x_ref[...].astype(BF16)

    acc = _dot(xb_ref[...], w_ref[...])
    half = ROT_DIM // 2

    def rope(a):
        cos, sa, sb = cos_ref[...], sa_ref[...], sb_ref[...]
        outs = []
        for c in range(tn // LANES):
            blk = a[:, c * LANES:(c + 1) * LANES]
            outs.append(blk * cos + pltpu.roll(blk, half, 1) * sa + pltpu.roll(blk, LANES - half, 1) * sb)
        return jnp.concatenate(outs, axis=1)

    @pl.when(j < nq)
    def _():
        q_ref[...] = (rope(acc) * (HEAD_DIM ** -0.5)).astype(BF16)

    @pl.when((j >= nq) & (j < nq + nk))
    def _():
        k_ref[...] = rope(acc)

    @pl.when(j >= nq + nk)
    def _():
        v_ref[...] = acc


def _qkv_call(x, w_qkv, tables, tm):
    t, d = x.shape
    tn = 2 * LANES
    nq, nk, nv = Q_DIM // tn, KV_DIM // tn, KV_DIM // tn
    tab_blocks = tables[0].shape[0] // tm
    tab_spec = pl.BlockSpec((tm, LANES), lambda i, j: (i % tab_blocks, 0))
    return pl.pallas_call(
        functools.partial(_qkv_body, nq=nq, nk=nk, tn=tn),
        grid=(t // tm, nq + nk + nv),
        in_specs=[
            pl.BlockSpec((tm, d), lambda i, j: (i, 0)),
            pl.BlockSpec((d, tn), lambda i, j: (0, j)),
            tab_spec, tab_spec, tab_spec,
        ],
        out_specs=[
            pl.BlockSpec((tm, tn), lambda i, j: (i, jnp.minimum(j, nq - 1))),
            pl.BlockSpec((tm, tn), lambda i, j: (i, jnp.clip(j - nq, 0, nk - 1))),
            pl.BlockSpec((tm, tn), lambda i, j: (i, jnp.clip(j - nq - nk, 0, nv - 1))),
        ],
        out_shape=[
            jax.ShapeDtypeStruct((t, Q_DIM), BF16),
            jax.ShapeDtypeStruct((t, KV_DIM), F32),
            jax.ShapeDtypeStruct((t, KV_DIM), F32),
        ],
        scratch_shapes=[pltpu.VMEM((tm, d), BF16)],
        compiler_params=_params("parallel", "arbitrary"),
        name="qkv",
    )(x, w_qkv, *tables)


def _head_pair_operands(kv_chunk, odd):
    lane = lax.broadcasted_iota(I32, kv_chunk.shape, 1)
    own = jnp.where((lane >= HEAD_DIM) == odd, kv_chunk, 0.0)
    other = pltpu.roll(own, HEAD_DIM, 1)
    lo, hi = (other, own) if odd else (own, other)
    return lo.astype(BF16), hi.astype(BF16)


def _sink_softmax(s, valid, sink):
    s = jnp.where(valid, s, NEG_BIG)
    m = jnp.maximum(jnp.max(s, axis=-1, keepdims=True), sink)
    p = jnp.exp(s - m)
    den = jnp.sum(p, axis=-1, keepdims=True) + jnp.exp(sink - m)
    return (p / den).astype(BF16)


def _attend(q, kk, vv, valid, sinks_ref, rows):
    chunks = []
    for kh in range(N_KV_HEADS):
        c = (kh * HEAD_DIM) // LANES
        odd = bool((kh * HEAD_DIM) % LANES)
        k_lo, k_hi = _head_pair_operands(kk[:, c * LANES:(c + 1) * LANES], odd)
        v_lo, v_hi = _head_pair_operands(vv[:, c * LANES:(c + 1) * LANES], odd)
        v_both = jnp.concatenate([v_lo, v_hi], axis=0)
        h0 = kh * GROUP
        qc = [q[:, (h0 // 2 + i) * LANES:(h0 // 2 + i + 1) * LANES] for i in range(GROUP // 2)]
        q_st = jnp.concatenate(qc, axis=0)
        ps = []
        for par, k_op in ((0, k_lo), (1, k_hi)):
            s = _dot_nt(q_st, k_op)
            sink = jnp.concatenate(
                [jnp.full((rows, 1), sinks_ref[h0 + 2 * i + par], F32) for i in range(GROUP // 2)], axis=0)
            ps.append(_sink_softmax(s, valid, sink))
        for i in range(GROUP // 2):
            p_both = jnp.concatenate([ps[0][i * rows:(i + 1) * rows], ps[1][i * rows:(i + 1) * rows]], axis=1)
            chunks.append(_dot(p_both, v_both))
    return jnp.concatenate(chunks, axis=1)


def _attn_prompt_body(sinks_ref, q_ref, kp_ref, kc_ref, vp_ref, vc_ref, o_ref):
    n = pl.program_id(1)
    w = WINDOW
    kk = jnp.concatenate([kp_ref[...], kc_ref[...]], axis=0)
    vv = jnp.concatenate([vp_ref[...], vc_ref[...]], axis=0)
    a = jnp.concatenate([lax.broadcasted_iota(I32, (w, 2 * w), 0)] * (GROUP // 2), axis=0)
    c = lax.broadcasted_iota(I32, ((GROUP // 2) * w, 2 * w), 1)
    valid = (c > a) & (c <= a + w) & ((n > 0) | (c >= w))
    o_ref[...] = _attend(q_ref[...], kk, vv, valid, sinks_ref, w).astype(BF16)


def _attn_prompt_call(q, k, v, sinks, batch, seq):
    w = WINDOW
    nb = seq // w
    cur = lambda b, n: (b * nb + n, 0)
    prev = lambda b, n: (b * nb + jnp.maximum(n - 1, 0), 0)
    return pl.pallas_call(
        _attn_prompt_body,
        grid=(batch, nb),
        in_specs=[
            pl.BlockSpec(memory_space=pltpu.SMEM),
            pl.BlockSpec((w, Q_DIM), cur),
            pl.BlockSpec((w, KV_DIM), prev),
            pl.BlockSpec((w, KV_DIM), cur),
            pl.BlockSpec((w, KV_DIM), prev),
            pl.BlockSpec((w, KV_DIM), cur),
        ],
        out_specs=pl.BlockSpec((w, Q_DIM), cur),
        out_shape=jax.ShapeDtypeStruct((batch * seq, Q_DIM), BF16),
        compiler_params=_params("parallel", "parallel"),
        name="attn_prompt",
    )(sinks, q, k, k, v, v)


def _attn_sample_body(sinks_ref, q_ref, k_ref, v_ref, o_ref, *, bt, nkeys):
    kk = k_ref[...].reshape(bt * nkeys, KV_DIM)
    vv = v_ref[...].reshape(bt * nkeys, KV_DIM)
    rows = (GROUP // 2) * bt
    row_b = jnp.concatenate([lax.broadcasted_iota(I32, (bt, bt * nkeys), 0)] * (GROUP // 2), axis=0)
    key_b = jnp.concatenate([jnp.full((rows, nkeys), b, I32) for b in range(bt)], axis=1)
    o_ref[...] = _attend(q_ref[...], kk, vv, row_b == key_b, sinks_ref, bt).astype(BF16)


def _attn_sample_call(q, k_win, v_win, sinks):
    b, nkeys, _ = k_win.shape
    bt = _tile(b, SUBLANES)
    return pl.pallas_call(
        functools.partial(_attn_sample_body, bt=bt, nkeys=nkeys),
        grid=(b // bt,),
        in_specs=[
            pl.BlockSpec(memory_space=pltpu.SMEM),
            pl.BlockSpec((bt, Q_DIM), lambda i: (i, 0)),
            pl.BlockSpec((bt, nkeys, KV_DIM), lambda i: (i, 0, 0)),
            pl.BlockSpec((bt, nkeys, KV_DIM), lambda i: (i, 0, 0)),
        ],
        out_specs=pl.BlockSpec((bt, Q_DIM), lambda i: (i, 0)),
        out_shape=jax.ShapeDtypeStruct((b, Q_DIM), BF16),
        compiler_params=_params("parallel"),
        name="attn_sample",
    )(sinks, q, k_win, v_win)


def _conv_body(*refs, decode, tiles_per_seq, tm, tail):
    if decode:
        x_ref, wb_ref, wc_ref, wh_ref, cw_ref, s0_ref, s1_ref, cb_ref, ut_ref, xb_ref = refs
    else:
        x_ref, wb_ref, wc_ref, wh_ref, cw_ref, cb_ref, ut_ref, xb_ref, carry_ref = refs
    i = pl.program_id(0)
    c = pl.program_id(1)

    @pl.when(c == 0)
    def _():
        xb_ref[...] = x_ref[...].astype(BF16)

    xb = xb_ref[...]
    b_g = _dot(xb, wb_ref[...])
    u = _dot(xb, wc_ref[...]) * _dot(xb, wh_ref[...])
    if decode:
        u_m1, u_m2 = s1_ref[...], s0_ref[...]
    else:
        @pl.when(i % tiles_per_seq == 0)
        def _():
            carry_ref[c] = jnp.zeros(carry_ref.shape[1:], F32)

        prev = carry_ref[c]
        p_m2, p_m1 = prev[SUBLANES - 2:SUBLANES - 1], prev[SUBLANES - 1:SUBLANES]
        r = lax.broadcasted_iota(I32, u.shape, 0)
        u_m1 = jnp.where(r == 0, p_m1, pltpu.roll(u, 1, 0))
        u_m2 = jnp.where(r == 0, p_m2, jnp.where(r == 1, p_m1, pltpu.roll(u, 2, 0)))
        carry_ref[c] = u[tm - SUBLANES:]
    cw = cw_ref[...]
    conv = cw[0:1] * u_m2 + cw[1:2] * u_m1 + cw[2:3] * u
    cb_ref[...] = (b_g * conv).astype(BF16)
    ut_ref[...] = u[tm - tail:]


def _conv_call(x, w_b, w_c, w_h, conv_w, tm, seq_rows, state=None):
    t, d = x.shape
    cdim = w_b.shape[1]
    tc = _tile(cdim, 2 * LANES)
    decode = state is not None
    tail = tm if decode else SUBLANES
    w_spec = pl.BlockSpec((d, tc), lambda i, c: (0, c))
    in_specs = [pl.BlockSpec((tm, d), lambda i, c: (i, 0)), w_spec, w_spec, w_spec,
                pl.BlockSpec((CONV_WIDTH, tc), lambda i, c: (0, c))]
    args = [x, w_b, w_c, w_h, conv_w]
    scratch = [pltpu.VMEM((tm, d), BF16)]
    if decode:
        in_specs += [pl.BlockSpec((tm, tc), lambda i, c: (i, c))] * 2
        args += list(state)
    else:
        scratch.append(pltpu.VMEM((cdim // tc, SUBLANES, tc), F32))
    return pl.pallas_call(
        functools.partial(_conv_body, decode=decode, tiles_per_seq=max(seq_rows // tm, 1), tm=tm, tail=tail),
        grid=(t // tm, cdim // tc),
        in_specs=in_specs,
        out_specs=[pl.BlockSpec((tm, tc), lambda i, c: (i, c)), pl.BlockSpec((tail, tc), lambda i, c: (i, c))],
        out_shape=[jax.ShapeDtypeStruct((t, cdim), BF16), jax.ShapeDtypeStruct((t // tm * tail, cdim), F32)],
        scratch_shapes=scratch,
        compiler_params=_params("arbitrary", "arbitrary"),
        name="conv",
    )(*args)


def _gate_body(x_ref, at_ref, cb_ref, wga_ref, wgc_ref, wa_ref, wco_ref, o_ref, xb_ref):
    @pl.when(pl.program_id(1) == 0)
    def _():
        xb_ref[...] = x_ref[...].astype(BF16)

    xb = xb_ref[...]
    g_a = _dot(xb, wga_ref[...])
    g_c = _dot(xb, wgc_ref[...])
    a = _dot(at_ref[...], wa_ref[...])
    c = _dot(cb_ref[...], wco_ref[...])
    o_ref[...] = (jax.nn.sigmoid(g_a) * a + jax.nn.sigmoid(g_c) * c).astype(BF16)


def _gate_call(x, attn, cb, w_ga, w_gc, w_a, w_co, tm):
    t, d = x.shape
    tn = _tile(d, 4 * LANES)
    row = lambda i, n: (i, 0)
    col = lambda i, n: (0, n)
    return pl.pallas_call(
        _gate_body,
        grid=(t // tm, d // tn),
        in_specs=[
            pl.BlockSpec((tm, d), row),
            pl.BlockSpec((tm, attn.shape[1]), row),
            pl.BlockSpec((tm, cb.shape[1]), row),
            pl.BlockSpec((d, tn), col),
            pl.BlockSpec((d, tn), col),
            pl.BlockSpec((w_a.shape[0], tn), col),
            pl.BlockSpec((w_co.shape[0], tn), col),
        ],
        out_specs=pl.BlockSpec((tm, tn), lambda i, n: (i, n)),
        out_shape=jax.ShapeDtypeStruct((t, d), BF16),
        scratch_shapes=[pltpu.VMEM((tm, d), BF16)],
        compiler_params=_params("parallel", "arbitrary"),
        name="gate",
    )(x, attn, cb, w_ga, w_gc, w_a, w_co)


def _post_body(x_ref, mp_ref, wo_ref, g_ref, b_ref, wr_ref, wsg_ref, wsu_ref, wsd_ref,
               hq_ref, base_ref, lg_ref, *, alpha):
    mixed = _dot(mp_ref[...], wo_ref[...])
    h = _layer_norm(alpha * x_ref[...] + mixed, g_ref[...], b_ref[...])
    hb = h.astype(BF16)
    lg_ref[...] = _dot_nt(wr_ref[...], hb)
    act = jax.nn.silu(_dot(hb, wsg_ref[...])) * _dot(hb, wsu_ref[...])
    base_ref[...] = alpha * h + _dot(act.astype(BF16), wsd_ref[...])
    half = hb.shape[1] // 2
    hq_ref[...] = _pack_pair(hb[:, :half], hb[:, half:])


def _post_call(x, mp, w_o, ln_g, ln_b, w_r_t, w_sg, w_su, w_sd, tm, alpha):
    t, d = x.shape
    n_e, f = w_r_t.shape[0], w_sg.shape[1]
    row = lambda i: (i, 0)
    full = lambda i: (0, 0)
    return pl.pallas_call(
        functools.partial(_post_body, alpha=alpha),
        grid=(t // tm,),
        in_specs=[
            pl.BlockSpec((tm, d), row),
            pl.BlockSpec((tm, d), row),
            pl.BlockSpec((d, d), full),
            pl.BlockSpec((1, d), full),
            pl.BlockSpec((1, d), full),
            pl.BlockSpec((n_e, d), full),
            pl.BlockSpec((d, f), full),
            pl.BlockSpec((d, f), full),
            pl.BlockSpec((f, d), full),
        ],
        out_specs=[
            pl.BlockSpec((tm, d // 2), row),
            pl.BlockSpec((tm, d), row),
            pl.BlockSpec((n_e, tm), lambda i: (0, i)),
        ],
        out_shape=[
            jax.ShapeDtypeStruct((t, d // 2), I32),
            jax.ShapeDtypeStruct((t, d), F32),
            jax.ShapeDtypeStruct((n_e, t), F32),
        ],
        compiler_params=_params("parallel"),
        name="post",
    )(x, mp, w_o, ln_g, ln_b, w_r_t, w_sg, w_su, w_sd)


def _route_body(bias_ref, lg_ref, idx_ref, w_ref, cnt_ref, *, n_tok):
    per_group = N_EXPERTS // N_GROUPS
    neg_inf = jnp.float32(-jnp.inf)
    i = pl.program_id(0)
    scores = [jax.nn.sigmoid(lg_ref[e]) for e in range(N_EXPERTS)]
    choice = [scores[e] + bias_ref[e] for e in range(N_EXPERTS)]

    group_score = []
    for g in range(N_GROUPS):
        vals = choice[g * per_group:(g + 1) * per_group]
        m1 = functools.reduce(jnp.maximum, vals)
        m2 = jnp.full_like(m1, neg_inf)
        found = jnp.zeros(m1.shape, jnp.bool_)
        for v in vals:
            eq = v == m1
            m2 = jnp.maximum(m2, jnp.where(eq & ~found, neg_inf, v))
            found = found | eq
        group_score.append(m1 + m2)

    masked = []
    for g in range(N_GROUPS):
        ahead = jnp.zeros(group_score[g].shape, I32)
        for o in range(N_GROUPS):
            if o == g:
                continue
            beats = group_score[o] > group_score[g]
            if o < g:
                beats = beats | (group_score[o] == group_score[g])
            ahead = ahead + beats.astype(I32)
        keep = ahead < TOPK_GROUPS
        masked += [jnp.where(keep, choice[e], neg_inf) for e in range(g * per_group, (g + 1) * per_group)]

    shape = masked[0].shape
    token = (i * SUBLANES + lax.broadcasted_iota(I32, shape, 0)) * LANES + lax.broadcasted_iota(I32, shape, 1)
    real = (token < n_tok).astype(I32)
    chosen = [jnp.zeros(shape, I32) for _ in range(N_EXPERTS)]
    picked_w = []
    for r in range(TOP_K):
        best = functools.reduce(jnp.maximum, masked)
        sel = jnp.full(shape, N_EXPERTS, I32)
        for e in reversed(range(N_EXPERTS)):
            sel = jnp.where(masked[e] == best, e, sel)
        w = jnp.zeros(shape, F32)
        for e in range(N_EXPERTS):
            hit = sel == e
            w = jnp.where(hit, scores[e], w)
            masked[e] = jnp.where(hit, neg_inf, masked[e])
            chosen[e] = jnp.where(hit, real, chosen[e])
        idx_ref[r] = sel
        picked_w.append(w)
    total = functools.reduce(lambda a, b: a + b, picked_w)
    for r in range(TOP_K):
        w_ref[r] = picked_w[r] / total * ROUTED_SCALE

    @pl.when(i == 0)
    def _():
        cnt_ref[...] = jnp.zeros(cnt_ref.shape, I32)

    for e in range(N_EXPERTS):
        cnt_ref[e] += chosen[e]


def _route_call(logits_t, bias, n_tok):
    n_e, t = logits_t.shape
    rows = t // LANES
    lg3 = logits_t.reshape(n_e, rows, LANES)
    pick = pl.BlockSpec((TOP_K, SUBLANES, LANES), lambda i: (0, i, 0))
    idx, w, cnt = pl.pallas_call(
        functools.partial(_route_body, n_tok=n_tok),
        grid=(rows // SUBLANES,),
        in_specs=[
            pl.BlockSpec(memory_space=pltpu.SMEM),
            pl.BlockSpec((n_e, SUBLANES, LANES), lambda i: (0, i, 0)),
        ],
        out_specs=[pick, pick, pl.BlockSpec((n_e, SUBLANES, LANES), lambda i: (0, 0, 0))],
        out_shape=[
            jax.ShapeDtypeStruct((TOP_K, rows, LANES), I32),
            jax.ShapeDtypeStruct((TOP_K, rows, LANES), F32),
            jax.ShapeDtypeStruct((n_e, SUBLANES, LANES), I32),
        ],
        compiler_params=_params("arbitrary"),
        name="route",
    )(bias, lg3)
    return idx.reshape(TOP_K, t), w.reshape(TOP_K, t), jnp.sum(cnt, axis=(1, 2))


def _moe_body(be_ref, tok0_ref, tokn_ref, dstp_ref, hq_hbm, wg_ref, wu_ref, wd_ref, out_hbm,
              xb0, xb1, yb0, yb1, wgb, wub, wdb, gsem, ssem, *, rows):
    b = pl.program_id(0)
    nb = pl.num_programs(0)
    xbufs, ybufs = (xb0, xb1), (yb0, yb1)

    def gather(idx_ref, dst_buf, sem):
        for r in range(rows):
            pltpu.make_async_copy(hq_hbm.at[pl.ds(idx_ref[0, 0, r], 1), :],
                                  dst_buf.at[pl.ds(r, 1), :], sem).start(priority=r % 2)

    def gather_done(s):
        return pltpu.make_async_copy(hq_hbm.at[pl.ds(0, rows), :], xbufs[s], gsem.at[s])

    def scatter_done(s):
        return pltpu.make_async_copy(ybufs[s], out_hbm.at[pl.ds(0, rows), :], ssem.at[s])

    @pl.when(b == 0)
    def _():
        gather(tok0_ref, xb0, gsem.at[0])
        yb1[...] = jnp.zeros(yb1.shape, I32)

    @pl.when((b == 0) | (be_ref[b] != be_ref[jnp.maximum(b - 1, 0)]))
    def _():
        wgb[...] = wg_ref[...].astype(BF16)
        wub[...] = wu_ref[...].astype(BF16)
        wdb[...] = wd_ref[...].astype(BF16)

    def block(s):
        o = 1 - s
        gather_done(s).wait()
        gather(tokn_ref, xbufs[o], gsem.at[o])
        for r in range(rows):
            pltpu.make_async_copy(ybufs[o].at[pl.ds(r, 1), :],
                                  out_hbm.at[pl.ds(dstp_ref[0, 0, r], 1), :], ssem.at[o]).start(priority=r % 2)
        lo, hi = _unpack_pair(xbufs[s][...])
        xb = jnp.concatenate([lo, hi], axis=1).astype(BF16)
        act = jax.nn.silu(_dot(xb, wgb[...])) * _dot(xb, wub[...])
        y = _dot(act.astype(BF16), wdb[...]).astype(BF16)
        half = y.shape[1] // 2

        @pl.when(b >= 1)
        def _():
            scatter_done(s).wait()

        ybufs[s][...] = _pack_pair(y[:, :half], y[:, half:])

    for s in range(2):
        @pl.when(b % 2 == s)
        def _(s=s):
            block(s)

    @pl.when(b == nb - 1)
    def _():
        gather_done(0).wait()
        scatter_done(0).wait()


def _moe_call(block_e, tok0, tokn, dstp, hq, w_gate, w_up, w_down, out_rows, rows):
    n_blocks = block_e.shape[0]
    assert n_blocks % 2 == 0
    n_e, d, f = w_gate.shape
    per_block = pl.BlockSpec((1, 1, rows), lambda b, be: (b, 0, 0), memory_space=pltpu.SMEM)
    return pl.pallas_call(
        functools.partial(_moe_body, rows=rows),
        grid_spec=pltpu.PrefetchScalarGridSpec(
            num_scalar_prefetch=1,
            grid=(n_blocks,),
            in_specs=[
                pl.BlockSpec((1, 1, rows), lambda b, be: (0, 0, 0), memory_space=pltpu.SMEM),
                per_block, per_block,
                pl.BlockSpec(memory_space=pl.ANY),
                pl.BlockSpec((None, d, f), lambda b, be: (be[b], 0, 0)),
                pl.BlockSpec((None, d, f), lambda b, be: (be[b], 0, 0)),
                pl.BlockSpec((None, f, d), lambda b, be: (be[b], 0, 0)),
            ],
            out_specs=pl.BlockSpec(memory_space=pl.ANY),
            scratch_shapes=[pltpu.VMEM((rows, d // 2), I32)] * 4 + [
                pltpu.VMEM((d, f), BF16),
                pltpu.VMEM((d, f), BF16),
                pltpu.VMEM((f, d), BF16),
                pltpu.SemaphoreType.DMA((2,)),
                pltpu.SemaphoreType.DMA((2,)),
            ],
        ),
        out_shape=jax.ShapeDtypeStruct((out_rows, d // 2), I32),
        compiler_params=_params("arbitrary"),
        name="moe",
    )(block_e, tok0, tokn, dstp, hq, w_gate, w_up, w_down)


def _dispatch_plan(idx, counts, n_tok, rows):
    m = n_tok * TOP_K
    n_real = (m + N_EXPERTS * (rows - 1) + rows - 1) // rows
    n_blocks = n_real + 1 + (n_real + 1) % 2
    p = n_blocks * rows
    n_spare = p - m
    plane = -(-(n_tok + -(-n_spare // TOP_K)) // SUBLANES) * SUBLANES
    flat_e = idx.T.reshape(m)
    order = jnp.argsort(flat_e).astype(I32)
    padded = (counts + rows - 1) // rows * rows
    pad_end = jnp.cumsum(padded)
    pad_start = pad_end - padded
    start = jnp.cumsum(counts) - counts
    block_first = jnp.arange(n_blocks, dtype=I32) * rows
    block_e = jnp.minimum(jnp.sum((pad_end[None, :] <= block_first[:, None]).astype(I32), axis=1), N_EXPERTS - 1)
    slot = block_first[:, None] + jnp.arange(rows, dtype=I32)[None, :]
    q = slot - pad_start[block_e][:, None]
    cnt_b = counts[block_e][:, None]
    start_b = start[block_e][:, None]
    real = q < cnt_b
    assign = order[jnp.clip(start_b + q, 0, m - 1)]
    t_of, j_of = assign // TOP_K, assign % TOP_K
    spare_rank = rows + slot - (start_b + jnp.minimum(q, cnt_b))

    def spare_row(rank):
        return (rank % TOP_K) * plane + n_tok + rank // TOP_K

    tok = jnp.where(real, t_of, 0)
    dst = jnp.where(real, j_of * plane + t_of, spare_row(spare_rank))
    first_dst = spare_row(jnp.arange(rows, dtype=I32))[None, :]
    tokn = jnp.concatenate([tok[1:], jnp.zeros((1, rows), I32)], axis=0)
    dstp = jnp.concatenate([first_dst, dst[:-1]], axis=0)
    shape = (n_blocks, 1, rows)
    return block_e.astype(I32), tok[:1].reshape(1, 1, rows), tokn.reshape(shape), dstp.reshape(shape), plane


def _final_body(base_ref, y8_ref, w_ref, g_ref, b_ref, o_ref):
    w = w_ref[...]
    ffn = None
    for j in range(TOP_K):
        lo, hi = _unpack_pair(y8_ref[j])
        yj = jnp.concatenate([lo, hi], axis=1) * w[:, j:j + 1]
        ffn = yj if ffn is None else ffn + yj
    o_ref[...] = _layer_norm(base_ref[...] + ffn, g_ref[...], b_ref[...])


def _final_call(base, y8, w8, ln_g, ln_b, tm, row_offset):
    t, d = base.shape
    off = row_offset // tm
    return pl.pallas_call(
        _final_body,
        grid=(t // tm,),
        in_specs=[
            pl.BlockSpec((tm, d), lambda i: (i, 0)),
            pl.BlockSpec((TOP_K, tm, d // 2), lambda i: (0, i + off, 0)),
            pl.BlockSpec((tm, TOP_K), lambda i: (i + off, 0)),
            pl.BlockSpec((1, d), lambda i: (0, 0)),
            pl.BlockSpec((1, d), lambda i: (0, 0)),
        ],
        out_specs=pl.BlockSpec((tm, d), lambda i: (i, 0)),
        out_shape=jax.ShapeDtypeStruct((t, d), F32),
        compiler_params=_params("parallel"),
        name="final",
    )(base, y8, w8, ln_g, ln_b)


def _mixer_and_post(x, pos_tables, w, tm, seq_rows, attend, conv_state, alpha):
    q, k, v = _qkv_call(x, w['qkv'], pos_tables, tm)
    attn = attend(q, k, v)
    cb, u_tail = _conv_call(x, w['b'], w['c'], w['h'], w['conv'], tm, seq_rows, conv_state)
    mp = _gate_call(x, attn, cb, w['ga'], w['gc'], w['attn_out'], w['conv_out'], tm)
    hq, base, logits_t = _post_call(x, mp, w['o'], w['ln1_g'], w['ln1_b'], w['router_t'],
                                    w['sh_gate'], w['sh_up'], w['sh_down'], min(tm, 2 * LANES), alpha)
    return k, v, u_tail, hq, base, logits_t


def kernel(x_prompt, x_sample, cache_k, cache_v, state_conv, w_in, attn_sinks, conv_w, w_attn_out, w_conv_out, w_o, ln1_g, ln1_b, w_router, router_bias, w_exp_gate, w_exp_up, w_exp_down, w_sh_gate, w_sh_up, w_sh_down, ln2_g, ln2_b):
    depth, d, _ = w_in.shape
    batch, seq, _ = x_prompt.shape
    dec_batch, dec_seq, _ = x_sample.shape
    win_buf = cache_k.shape[2]
    cdim = conv_w.shape[2]
    assert dec_seq == 1 and win_buf == WINDOW and seq % WINDOW == 0
    alpha = (2 * depth) ** 0.25
    t_p, t_s = batch * seq, dec_batch * dec_seq
    tm_p, tm_s = _tile(seq, 4 * LANES), _tile(t_s, LANES)
    assert t_p % tm_s == 0 and t_s % tm_s == 0
    tab_p = _rope_tables(jnp.arange(seq))
    tab_s = _rope_tables(jnp.full((tm_s,), PAST_LEN, I32))

    yp = x_prompt.reshape(t_p, d)
    ys = x_sample.reshape(t_s, d)
    p_k, p_v, p_c, s_k, s_v, s_c = [], [], [], [], [], []
    for l in range(depth):
        wl = w_in[l].astype(BF16)
        o = 0
        w = {}
        for name, width in (('qkv', Q_DIM + 2 * KV_DIM), ('b', cdim), ('c', cdim), ('h', cdim), ('ga', d), ('gc', d)):
            w[name] = wl[:, o:o + width]
            o += width
        w.update(
            conv=conv_w[l], attn_out=w_attn_out[l].astype(BF16), conv_out=w_conv_out[l].astype(BF16),
            o=w_o[l].astype(BF16), ln1_g=ln1_g[l][None], ln1_b=ln1_b[l][None],
            router_t=w_router[l].T.astype(BF16), sh_gate=w_sh_gate[l].astype(BF16),
            sh_up=w_sh_up[l].astype(BF16), sh_down=w_sh_down[l].astype(BF16))
        sinks = attn_sinks[l]

        k, v, u_tail, hq_p, base_p, lg_p = _mixer_and_post(
            yp, tab_p, w, tm_p, seq,
            lambda q, k, v: _attn_prompt_call(q, k, v, sinks, batch, seq), None, alpha)
        keep = min(WINDOW, seq)
        p_k.append(k.reshape(batch, seq, N_KV_HEADS, HEAD_DIM)[:, seq - keep:])
        p_v.append(v.reshape(batch, seq, N_KV_HEADS, HEAD_DIM)[:, seq - keep:])
        tails = u_tail.reshape(batch, seq // tm_p, SUBLANES, cdim)
        p_c.append(tails[:, -1, SUBLANES - (CONV_WIDTH - 1):])

        new_kv = {}

        def attend_sample(q, k, v, l=l):
            new_kv['k'] = jnp.concatenate([cache_k[l][:, 1:], k.reshape(t_s, 1, N_KV_HEADS, HEAD_DIM)], axis=1)
            new_kv['v'] = jnp.concatenate([cache_v[l][:, 1:], v.reshape(t_s, 1, N_KV_HEADS, HEAD_DIM)], axis=1)
            return _attn_sample_call(q, new_kv['k'].reshape(t_s, win_buf, KV_DIM),
                                     new_kv['v'].reshape(t_s, win_buf, KV_DIM), sinks)

        state = (state_conv[l][:, 0], state_conv[l][:, 1])
        _, _, u_s, hq_s, base_s, lg_s = _mixer_and_post(ys, tab_s, w, tm_s, 1, attend_sample, state, alpha)
        s_k.append(new_kv['k'])
        s_v.append(new_kv['v'])
        s_c.append(jnp.concatenate([state_conv[l][:, 1:], u_s[:, None]], axis=1))

        n_tok = t_p + t_s
        route_tile = SUBLANES * LANES
        t_pad = -(-n_tok // route_tile) * route_tile
        logits_t = jnp.concatenate([lg_p, lg_s, jnp.zeros((N_EXPERTS, t_pad - n_tok), F32)], axis=1)
        idx, wts, counts = _route_call(logits_t, router_bias[l], n_tok)
        idx, wts = idx[:, :n_tok], wts[:, :n_tok]
        block_e, tok0, tokn, dstp, plane = _dispatch_plan(idx, counts, n_tok, MOE_BLOCK_ROWS)
        hq = jnp.concatenate([hq_p, hq_s], axis=0)
        y8 = _moe_call(block_e, tok0, tokn, dstp, hq, w_exp_gate[l], w_exp_up[l], w_exp_down[l],
                       TOP_K * plane, MOE_BLOCK_ROWS)
        y8 = y8.reshape(TOP_K, plane, d // 2)
        w8 = wts.T
        yp = _final_call(base_p, y8, w8, ln2_g[l][None], ln2_b[l][None], min(tm_p, 2 * LANES), 0)
        ys = _final_call(base_s, y8, w8, ln2_g[l][None], ln2_b[l][None], tm_s, t_p)

    return (yp.reshape(batch, seq, d), ys.reshape(dec_batch, dec_seq, d), jnp.stack(p_k), jnp.stack(p_v),
            jnp.stack(p_c), jnp.stack(s_k), jnp.stack(s_v), jnp.stack(s_c))
```

```python
import functools

import jax
import jax.numpy as jnp
from jax import lax
from jax.experimental import pallas as pl
from jax.experimental.pallas import tpu as pltpu

N_HEADS = 16
N_KV_HEADS = 4
HEAD_DIM = 64
GROUP = N_HEADS // N_KV_HEADS
Q_DIM = N_HEADS * HEAD_DIM
KV_DIM = N_KV_HEADS * HEAD_DIM
ROT_DIM = HEAD_DIM // 4
ROPE_THETA = 500000.0
WINDOW = 128
CONV_WIDTH = 3
PAST_LEN = 16384
N_EXPERTS = 64
N_GROUPS = 8
TOPK_GROUPS = 4
TOP_K = 8
ROUTED_SCALE = 2.5
LN_EPS = 1e-5
MOE_BLOCK_ROWS = 256

LANES = 128
SUBLANES = 8
VMEM_LIMIT_BYTES = 48 * 1024 * 1024
NEG_BIG = -1e30

F32 = jnp.float32
BF16 = jnp.bfloat16
I32 = jnp.int32


def _tile(n, pref):
    t = pref
    while t > 1 and n % t:
        t //= 2
    return t


def _params(*sem):
    return pltpu.CompilerParams(dimension_semantics=sem, vmem_limit_bytes=VMEM_LIMIT_BYTES)


def _dot(a, b):
    return jnp.dot(a, b, preferred_element_type=F32)


def _dot_nt(a, b):
    return lax.dot_general(a, b, (((1,), (1,)), ((), ())), preferred_element_type=F32)


def _layer_norm(x, g, b):
    mu = jnp.mean(x, axis=-1, keepdims=True)
    xc = x - mu
    var = jnp.mean(xc * xc, axis=-1, keepdims=True)
    return xc * lax.rsqrt(var + LN_EPS) * g + b


def _pack_pair(lo, hi):
    lo32 = lax.bitcast_convert_type(lo.astype(F32), I32)
    hi32 = lax.bitcast_convert_type(hi.astype(F32), I32)
    return lax.shift_right_logical(lo32, 16) | (hi32 & -65536)


def _unpack_pair(p):
    lo = lax.bitcast_convert_type(lax.shift_left(p, 16), F32)
    hi = lax.bitcast_convert_type(p & -65536, F32)
    return lo, hi


def _store_packed_rows(ref, lead, x, n):
    ns = x.shape[1] // (2 * LANES)
    for c in range(ns):
        lo = x[:, 2 * c * LANES:(2 * c + 1) * LANES]
        hi = x[:, (2 * c + 1) * LANES:(2 * c + 2) * LANES]
        ref[lead + (pl.ds(c, n, stride=ns), slice(None))] = _pack_pair(lo, hi)


def _load_packed_rows(ref, lead, n, ns):
    parts = []
    for c in range(ns):
        parts += list(_unpack_pair(ref[lead + (pl.ds(c, n, stride=ns), slice(None))]))
    return jnp.concatenate(parts, axis=1)


def _rope_tables(pos):
    half = ROT_DIM // 2
    n = pos.shape[0]
    inv_freq = jnp.power(jnp.float32(ROPE_THETA), -jnp.arange(half, dtype=F32) * (2.0 / ROT_DIM))
    ang = pos.astype(F32)[:, None] * inv_freq[None, :]
    cos, sin = jnp.cos(ang), jnp.sin(ang)
    rest = HEAD_DIM - ROT_DIM
    cos_h = jnp.concatenate([cos, cos, jnp.ones((n, rest), F32)], axis=1)
    sa_h = jnp.concatenate([jnp.zeros((n, half), F32), sin, jnp.zeros((n, rest), F32)], axis=1)
    sb_h = jnp.concatenate([-sin, jnp.zeros((n, half + rest), F32)], axis=1)
    rep = LANES // HEAD_DIM
    return tuple(jnp.concatenate([t] * rep, axis=1) for t in (cos_h, sa_h, sb_h))


def _qkv_body(x_ref, w_ref, cos_ref, sa_ref, sb_ref, q_ref, k_ref, v_ref, xb_ref, *, nq, nk, tn):
    j = pl.program_id(1)

    @pl.when(j == 0)
    def _():
        xb_ref[...] = x_ref[...].astype(BF16)

    acc = _dot(xb_ref[...], w_ref[...])
    half = ROT_DIM // 2

    def rope(a):
        cos, sa, sb = cos_ref[...], sa_ref[...], sb_ref[...]
        outs = []
        for c in range(tn // LANES):
            blk = a[:, c * LANES:(c + 1) * LANES]
            outs.append(blk * cos + pltpu.roll(blk, half, 1) * sa + pltpu.roll(blk, LANES - half, 1) * sb)
        return jnp.concatenate(outs, axis=1)

    @pl.when(j < nq)
    def _():
        q_ref[...] = (rope(acc) * (HEAD_DIM ** -0.5)).astype(BF16)

    @pl.when((j >= nq) & (j < nq + nk))
    def _():
        k_ref[...] = rope(acc)

    @pl.when(j >= nq + nk)
    def _():
        v_ref[...] = acc


def _qkv_call(x, w_qkv, tables, tm):
    t, d = x.shape
    tn = 2 * LANES
    nq, nk, nv = Q_DIM // tn, KV_DIM // tn, KV_DIM // tn
    tab_blocks = tables[0].shape[0] // tm
    tab_spec = pl.BlockSpec((tm, LANES), lambda i, j: (i % tab_blocks, 0))
    return pl.pallas_call(
        functools.partial(_qkv_body, nq=nq, nk=nk, tn=tn),
        grid=(t // tm, nq + nk + nv),
        in_specs=[
            pl.BlockSpec((tm, d), lambda i, j: (i, 0)),
            pl.BlockSpec((d, tn), lambda i, j: (0, j)),
            tab_spec, tab_spec, tab_spec,
        ],
        out_specs=[
            pl.BlockSpec((tm, tn), lambda i, j: (i, jnp.minimum(j, nq - 1))),
            pl.BlockSpec((tm, tn), lambda i, j: (i, jnp.clip(j - nq, 0, nk - 1))),
            pl.BlockSpec((tm, tn), lambda i, j: (i, jnp.clip(j - nq - nk, 0, nv - 1))),
        ],
        out_shape=[
            jax.ShapeDtypeStruct((t, Q_DIM), BF16),
            jax.ShapeDtypeStruct((t, KV_DIM), F32),
            jax.ShapeDtypeStruct((t, KV_DIM), F32),
        ],
        scratch_shapes=[pltpu.VMEM((tm, d), BF16)],
        compiler_params=_params("parallel", "arbitrary"),
        name="qkv",
    )(x, w_qkv, *tables)


def _head_pair_operands(kv_chunk, odd):
    lane = lax.broadcasted_iota(I32, kv_chunk.shape, 1)
    own = jnp.where((lane >= HEAD_DIM) == odd, kv_chunk, 0.0)
    other = pltpu.roll(own, HEAD_DIM, 1)
    lo, hi = (other, own) if odd else (own, other)
    return lo.astype(BF16), hi.astype(BF16)


def _sink_softmax(s, valid, sink):
    s = jnp.where(valid, s, NEG_BIG)
    m = jnp.maximum(jnp.max(s, axis=-1, keepdims=True), sink)
    p = jnp.exp(s - m)
    den = jnp.sum(p, axis=-1, keepdims=True) + jnp.exp(sink - m)
    return (p / den).astype(BF16)


def _attend(q, kk, vv, valid, sinks_ref, rows):
    chunks = []
    for kh in range(N_KV_HEADS):
        c = (kh * HEAD_DIM) // LANES
        odd = bool((kh * HEAD_DIM) % LANES)
        k_lo, k_hi = _head_pair_operands(kk[:, c * LANES:(c + 1) * LANES], odd)
        v_lo, v_hi = _head_pair_operands(vv[:, c * LANES:(c + 1) * LANES], odd)
        v_both = jnp.concatenate([v_lo, v_hi], axis=0)
        h0 = kh * GROUP
        qc = [q[:, (h0 // 2 + i) * LANES:(h0 // 2 + i + 1) * LANES] for i in range(GROUP // 2)]
        q_st = jnp.concatenate(qc, axis=0)
        ps = []
        for par, k_op in ((0, k_lo), (1, k_hi)):
            s = _dot_nt(q_st, k_op)
            sink = jnp.concatenate(
                [jnp.full((rows, 1), sinks_ref[h0 + 2 * i + par], F32) for i in range(GROUP // 2)], axis=0)
            ps.append(_sink_softmax(s, valid, sink))
        for i in range(GROUP // 2):
            p_both = jnp.concatenate([ps[0][i * rows:(i + 1) * rows], ps[1][i * rows:(i + 1) * rows]], axis=1)
            chunks.append(_dot(p_both, v_both))
    return jnp.concatenate(chunks, axis=1)


def _attn_prompt_body(sinks_ref, q_ref, kp_ref, kc_ref, vp_ref, vc_ref, o_ref):
    n = pl.program_id(1)
    w = WINDOW
    kk = jnp.concatenate([kp_ref[...], kc_ref[...]], axis=0)
    vv = jnp.concatenate([vp_ref[...], vc_ref[...]], axis=0)
    a = jnp.concatenate([lax.broadcasted_iota(I32, (w, 2 * w), 0)] * (GROUP // 2), axis=0)
    c = lax.broadcasted_iota(I32, ((GROUP // 2) * w, 2 * w), 1)
    valid = (c > a) & (c <= a + w) & ((n > 0) | (c >= w))
    o_ref[...] = _attend(q_ref[...], kk, vv, valid, sinks_ref, w).astype(BF16)


def _attn_prompt_call(q, k, v, sinks, batch, seq):
    w = WINDOW
    nb = seq // w
    cur = lambda b, n: (b * nb + n, 0)
    prev = lambda b, n: (b * nb + jnp.maximum(n - 1, 0), 0)
    return pl.pallas_call(
        _attn_prompt_body,
        grid=(batch, nb),
        in_specs=[
            pl.BlockSpec(memory_space=pltpu.SMEM),
            pl.BlockSpec((w, Q_DIM), cur),
            pl.BlockSpec((w, KV_DIM), prev),
            pl.BlockSpec((w, KV_DIM), cur),
            pl.BlockSpec((w, KV_DIM), prev),
            pl.BlockSpec((w, KV_DIM), cur),
        ],
        out_specs=pl.BlockSpec((w, Q_DIM), cur),
        out_shape=jax.ShapeDtypeStruct((batch * seq, Q_DIM), BF16),
        compiler_params=_params("parallel", "parallel"),
        name="attn_prompt",
    )(sinks, q, k, k, v, v)


def _attn_sample_body(sinks_ref, q_ref, k_ref, v_ref, o_ref, *, bt, nkeys):
    kk = k_ref[...].reshape(bt * nkeys, KV_DIM)
    vv = v_ref[...].reshape(bt * nkeys, KV_DIM)
    rows = (GROUP // 2) * bt
    row_b = jnp.concatenate([lax.broadcasted_iota(I32, (bt, bt * nkeys), 0)] * (GROUP // 2), axis=0)
    key_b = jnp.concatenate([jnp.full((rows, nkeys), b, I32) for b in range(bt)], axis=1)
    o_ref[...] = _attend(q_ref[...], kk, vv, row_b == key_b, sinks_ref, bt).astype(BF16)


def _attn_sample_call(q, k_win, v_win, sinks):
    b, nkeys, _ = k_win.shape
    bt = _tile(b, SUBLANES)
    return pl.pallas_call(
        functools.partial(_attn_sample_body, bt=bt, nkeys=nkeys),
        grid=(b // bt,),
        in_specs=[
            pl.BlockSpec(memory_space=pltpu.SMEM),
            pl.BlockSpec((bt, Q_DIM), lambda i: (i, 0)),
            pl.BlockSpec((bt, nkeys, KV_DIM), lambda i: (i, 0, 0)),
            pl.BlockSpec((bt, nkeys, KV_DIM), lambda i: (i, 0, 0)),
        ],
        out_specs=pl.BlockSpec((bt, Q_DIM), lambda i: (i, 0)),
        out_shape=jax.ShapeDtypeStruct((b, Q_DIM), BF16),
        compiler_params=_params("parallel"),
        name="attn_sample",
    )(sinks, q, k_win, v_win)


def _conv_body(*refs, decode, tiles_per_seq, tm, tail):
    if decode:
        x_ref, wb_ref, wc_ref, wh_ref, cw_ref, s0_ref, s1_ref, cb_ref, ut_ref, xb_ref = refs
    else:
        x_ref, wb_ref, wc_ref, wh_ref, cw_ref, cb_ref, ut_ref, xb_ref, carry_ref = refs
    i = pl.program_id(0)
    c = pl.program_id(1)

    @pl.when(c == 0)
    def _():
        xb_ref[...] = x_ref[...].astype(BF16)

    xb = xb_ref[...]
    b_g = _dot(xb, wb_ref[...])
    u = _dot(xb, wc_ref[...]) * _dot(xb, wh_ref[...])
    if decode:
        u_m1, u_m2 = s1_ref[...], s0_ref[...]
    else:
        @pl.when(i % tiles_per_seq == 0)
        def _():
            carry_ref[c] = jnp.zeros(carry_ref.shape[1:], F32)

        prev = carry_ref[c]
        p_m2, p_m1 = prev[SUBLANES - 2:SUBLANES - 1], prev[SUBLANES - 1:SUBLANES]
        r = lax.broadcasted_iota(I32, u.shape, 0)
        u_m1 = jnp.where(r == 0, p_m1, pltpu.roll(u, 1, 0))
        u_m2 = jnp.where(r == 0, p_m2, jnp.where(r == 1, p_m1, pltpu.roll(u, 2, 0)))
        carry_ref[c] = u[tm - SUBLANES:]
    cw = cw_ref[...]
    conv = cw[0:1] * u_m2 + cw[1:2] * u_m1 + cw[2:3] * u
    cb_ref[...] = (b_g * conv).astype(BF16)
    ut_ref[...] = u[tm - tail:]


def _conv_call(x, w_b, w_c, w_h, conv_w, tm, seq_rows, state=None):
    t, d = x.shape
    cdim = w_b.shape[1]
    tc = _tile(cdim, 2 * LANES)
    decode = state is not None
    tail = tm if decode else SUBLANES
    w_spec = pl.BlockSpec((d, tc), lambda i, c: (0, c))
    in_specs = [pl.BlockSpec((tm, d), lambda i, c: (i, 0)), w_spec, w_spec, w_spec,
                pl.BlockSpec((CONV_WIDTH, tc), lambda i, c: (0, c))]
    args = [x, w_b, w_c, w_h, conv_w]
    scratch = [pltpu.VMEM((tm, d), BF16)]
    if decode:
        in_specs += [pl.BlockSpec((tm, tc), lambda i, c: (i, c))] * 2
        args += list(state)
    else:
        scratch.append(pltpu.VMEM((cdim // tc, SUBLANES, tc), F32))
    return pl.pallas_call(
        functools.partial(_conv_body, decode=decode, tiles_per_seq=max(seq_rows // tm, 1), tm=tm, tail=tail),
        grid=(t // tm, cdim // tc),
        in_specs=in_specs,
        out_specs=[pl.BlockSpec((tm, tc), lambda i, c: (i, c)), pl.BlockSpec((tail, tc), lambda i, c: (i, c))],
        out_shape=[jax.ShapeDtypeStruct((t, cdim), BF16), jax.ShapeDtypeStruct((t // tm * tail, cdim), F32)],
        scratch_shapes=scratch,
        compiler_params=_params("arbitrary", "arbitrary"),
        name="conv",
    )(*args)


def _gate_body(x_ref, at_ref, cb_ref, wga_ref, wgc_ref, wa_ref, wco_ref, o_ref, xb_ref):
    @pl.when(pl.program_id(1) == 0)
    def _():
        xb_ref[...] = x_ref[...].astype(BF16)

    xb = xb_ref[...]
    g_a = _dot(xb, wga_ref[...])
    g_c = _dot(xb, wgc_ref[...])
    a = _dot(at_ref[...], wa_ref[...])
    c = _dot(cb_ref[...], wco_ref[...])
    o_ref[...] = (jax.nn.sigmoid(g_a) * a + jax.nn.sigmoid(g_c) * c).astype(BF16)


def _gate_call(x, attn, cb, w_ga, w_gc, w_a, w_co, tm):
    t, d = x.shape
    tn = _tile(d, 4 * LANES)
    row = lambda i, n: (i, 0)
    col = lambda i, n: (0, n)
    return pl.pallas_call(
        _gate_body,
        grid=(t // tm, d // tn),
        in_specs=[
            pl.BlockSpec((tm, d), row),
            pl.BlockSpec((tm, attn.shape[1]), row),
            pl.BlockSpec((tm, cb.shape[1]), row),
            pl.BlockSpec((d, tn), col),
            pl.BlockSpec((d, tn), col),
            pl.BlockSpec((w_a.shape[0], tn), col),
            pl.BlockSpec((w_co.shape[0], tn), col),
        ],
        out_specs=pl.BlockSpec((tm, tn), lambda i, n: (i, n)),
        out_shape=jax.ShapeDtypeStruct((t, d), BF16),
        scratch_shapes=[pltpu.VMEM((tm, d), BF16)],
        compiler_params=_params("parallel", "arbitrary"),
        name="gate",
    )(x, attn, cb, w_ga, w_gc, w_a, w_co)


def _post_body(x_ref, mp_ref, wo_ref, g_ref, b_ref, wr_ref, wsg_ref, wsu_ref, wsd_ref,
               hq_ref, base_ref, lg_ref, *, alpha):
    mixed = _dot(mp_ref[...], wo_ref[...])
    h = _layer_norm(alpha * x_ref[...] + mixed, g_ref[...], b_ref[...])
    hb = h.astype(BF16)
    lg_ref[...] = _dot_nt(wr_ref[...], hb)
    act = jax.nn.silu(_dot(hb, wsg_ref[...])) * _dot(hb, wsu_ref[...])
    base_ref[...] = alpha * h + _dot(act.astype(BF16), wsd_ref[...])
    _store_packed_rows(hq_ref, (), hb, hb.shape[0])


def _post_call(x, mp, w_o, ln_g, ln_b, w_r_t, w_sg, w_su, w_sd, tm, alpha):
    t, d = x.shape
    n_e, f = w_r_t.shape[0], w_sg.shape[1]
    ns = d // (2 * LANES)
    row = lambda i: (i, 0)
    full = lambda i: (0, 0)
    return pl.pallas_call(
        functools.partial(_post_body, alpha=alpha),
        grid=(t // tm,),
        in_specs=[
            pl.BlockSpec((tm, d), row),
            pl.BlockSpec((tm, d), row),
            pl.BlockSpec((d, d), full),
            pl.BlockSpec((1, d), full),
            pl.BlockSpec((1, d), full),
            pl.BlockSpec((n_e, d), full),
            pl.BlockSpec((d, f), full),
            pl.BlockSpec((d, f), full),
            pl.BlockSpec((f, d), full),
        ],
        out_specs=[
            pl.BlockSpec((tm * ns, LANES), row),
            pl.BlockSpec((tm, d), row),
            pl.BlockSpec((n_e, tm), lambda i: (0, i)),
        ],
        out_shape=[
            jax.ShapeDtypeStruct((t * ns, LANES), I32),
            jax.ShapeDtypeStruct((t, d), F32),
            jax.ShapeDtypeStruct((n_e, t), F32),
        ],
        compiler_params=_params("parallel"),
        name="post",
    )(x, mp, w_o, ln_g, ln_b, w_r_t, w_sg, w_su, w_sd)


def _route_body(bias_ref, lg_ref, idx_ref, w_ref, cnt_ref, *, n_tok):
    per_group = N_EXPERTS // N_GROUPS
    neg_inf = jnp.float32(-jnp.inf)
    i = pl.program_id(0)
    scores = [jax.nn.sigmoid(lg_ref[e]) for e in range(N_EXPERTS)]
    choice = [scores[e] + bias_ref[e] for e in range(N_EXPERTS)]

    group_score = []
    for g in range(N_GROUPS):
        vals = choice[g * per_group:(g + 1) * per_group]
        m1 = functools.reduce(jnp.maximum, vals)
        m2 = jnp.full_like(m1, neg_inf)
        found = jnp.zeros(m1.shape, jnp.bool_)
        for v in vals:
            eq = v == m1
            m2 = jnp.maximum(m2, jnp.where(eq & ~found, neg_inf, v))
            found = found | eq
        group_score.append(m1 + m2)

    masked = []
    for g in range(N_GROUPS):
        ahead = jnp.zeros(group_score[g].shape, I32)
        for o in range(N_GROUPS):
            if o == g:
                continue
            beats = group_score[o] > group_score[g]
            if o < g:
                beats = beats | (group_score[o] == group_score[g])
            ahead = ahead + beats.astype(I32)
        keep = ahead < TOPK_GROUPS
        masked += [jnp.where(keep, choice[e], neg_inf) for e in range(g * per_group, (g + 1) * per_group)]

    shape = masked[0].shape
    token = (i * SUBLANES + lax.broadcasted_iota(I32, shape, 0)) * LANES + lax.broadcasted_iota(I32, shape, 1)
    real = (token < n_tok).astype(I32)
    chosen = [jnp.zeros(shape, I32) for _ in range(N_EXPERTS)]
    picked_w = []
    for r in range(TOP_K):
        best = functools.reduce(jnp.maximum, masked)
        sel = jnp.full(shape, N_EXPERTS, I32)
        for e in reversed(range(N_EXPERTS)):
            sel = jnp.where(masked[e] == best, e, sel)
        w = jnp.zeros(shape, F32)
        for e in range(N_EXPERTS):
            hit = sel == e
            w = jnp.where(hit, scores[e], w)
            masked[e] = jnp.where(hit, neg_inf, masked[e])
            chosen[e] = jnp.where(hit, real, chosen[e])
        idx_ref[r] = sel
        picked_w.append(w)
    total = functools.reduce(lambda a, b: a + b, picked_w)
    for r in range(TOP_K):
        w_ref[r] = picked_w[r] / total * ROUTED_SCALE

    @pl.when(i == 0)
    def _():
        cnt_ref[...] = jnp.zeros(cnt_ref.shape, I32)

    for e in range(N_EXPERTS):
        cnt_ref[e] += chosen[e]


def _route_call(logits_t, bias, n_tok):
    n_e, t = logits_t.shape
    rows = t // LANES
    lg3 = logits_t.reshape(n_e, rows, LANES)
    pick = pl.BlockSpec((TOP_K, SUBLANES, LANES), lambda i: (0, i, 0))
    idx, w, cnt = pl.pallas_call(
        functools.partial(_route_body, n_tok=n_tok),
        grid=(rows // SUBLANES,),
        in_specs=[
            pl.BlockSpec(memory_space=pltpu.SMEM),
            pl.BlockSpec((n_e, SUBLANES, LANES), lambda i: (0, i, 0)),
        ],
        out_specs=[pick, pick, pl.BlockSpec((n_e, SUBLANES, LANES), lambda i: (0, 0, 0))],
        out_shape=[
            jax.ShapeDtypeStruct((TOP_K, rows, LANES), I32),
            jax.ShapeDtypeStruct((TOP_K, rows, LANES), F32),
            jax.ShapeDtypeStruct((n_e, SUBLANES, LANES), I32),
        ],
        compiler_params=_params("arbitrary"),
        name="route",
    )(bias, lg3)
    return idx.reshape(TOP_K, t), w.reshape(TOP_K, t), jnp.sum(cnt, axis=(1, 2))


def _moe_body(be_ref, tok0_ref, tokn_ref, dstp_ref, hq_hbm, wg_ref, wu_ref, wd_ref, out_hbm,
              xb0, xb1, yb0, yb1, wgb, wub, wdb, gsem, ssem, *, rows):
    b = pl.program_id(0)
    nb = pl.num_programs(0)
    xbufs, ybufs = (xb0, xb1), (yb0, yb1)
    ns = xb0.shape[0] // rows

    def token(ref, i):
        return ref.at[pl.ds(pl.multiple_of(i * ns, ns), ns), :]

    def gather(idx_ref, dst_buf, sem):
        for r in range(rows):
            pltpu.make_async_copy(token(hq_hbm, idx_ref[0, 0, r]), token(dst_buf, r), sem).start(priority=r % 2)

    def gather_done(s):
        return pltpu.make_async_copy(hq_hbm.at[pl.ds(0, rows * ns), :], xbufs[s], gsem.at[s])

    def scatter_done(s):
        return pltpu.make_async_copy(ybufs[s], out_hbm.at[pl.ds(0, rows * ns), :], ssem.at[s])

    @pl.when(b == 0)
    def _():
        gather(tok0_ref, xb0, gsem.at[0])
        yb1[...] = jnp.zeros(yb1.shape, I32)

    @pl.when((b == 0) | (be_ref[b] != be_ref[jnp.maximum(b - 1, 0)]))
    def _():
        wgb[...] = wg_ref[...].astype(BF16)
        wub[...] = wu_ref[...].astype(BF16)
        wdb[...] = wd_ref[...].astype(BF16)

    def block(s):
        o = 1 - s
        gather_done(s).wait()
        gather(tokn_ref, xbufs[o], gsem.at[o])
        for r in range(rows):
            pltpu.make_async_copy(token(ybufs[o], r), token(out_hbm, dstp_ref[0, 0, r]),
                                  ssem.at[o]).start(priority=r % 2)
        xb = _load_packed_rows(xbufs[s], (), rows, ns).astype(BF16)
        act = jax.nn.silu(_dot(xb, wgb[...])) * _dot(xb, wub[...])
        y = _dot(act.astype(BF16), wdb[...]).astype(BF16)

        @pl.when(b >= 1)
        def _():
            scatter_done(s).wait()

        _store_packed_rows(ybufs[s], (), y, rows)

    for s in range(2):
        @pl.when(b % 2 == s)
        def _(s=s):
            block(s)

    @pl.when(b == nb - 1)
    def _():
        gather_done(0).wait()
        scatter_done(0).wait()


def _moe_call(block_e, tok0, tokn, dstp, hq, w_gate, w_up, w_down, out_rows, rows):
    n_blocks = block_e.shape[0]
    assert n_blocks % 2 == 0
    n_e, d, f = w_gate.shape
    ns = d // (2 * LANES)
    per_block = pl.BlockSpec((1, 1, rows), lambda b, be: (b, 0, 0), memory_space=pltpu.SMEM)
    return pl.pallas_call(
        functools.partial(_moe_body, rows=rows),
        grid_spec=pltpu.PrefetchScalarGridSpec(
            num_scalar_prefetch=1,
            grid=(n_blocks,),
            in_specs=[
                pl.BlockSpec((1, 1, rows), lambda b, be: (0, 0, 0), memory_space=pltpu.SMEM),
                per_block, per_block,
                pl.BlockSpec(memory_space=pl.ANY),
                pl.BlockSpec((None, d, f), lambda b, be: (be[b], 0, 0)),
                pl.BlockSpec((None, d, f), lambda b, be: (be[b], 0, 0)),
                pl.BlockSpec((None, f, d), lambda b, be: (be[b], 0, 0)),
            ],
            out_specs=pl.BlockSpec(memory_space=pl.ANY),
            scratch_shapes=[pltpu.VMEM((rows * ns, LANES), I32)] * 4 + [
                pltpu.VMEM((d, f), BF16),
                pltpu.VMEM((d, f), BF16),
                pltpu.VMEM((f, d), BF16),
                pltpu.SemaphoreType.DMA((2,)),
                pltpu.SemaphoreType.DMA((2,)),
            ],
        ),
        out_shape=jax.ShapeDtypeStruct((out_rows * ns, LANES), I32),
        compiler_params=_params("arbitrary"),
        name="moe",
    )(block_e, tok0, tokn, dstp, hq, w_gate, w_up, w_down)


def _dispatch_plan(idx, counts, n_tok, rows):
    m = n_tok * TOP_K
    n_real = (m + N_EXPERTS * (rows - 1) + rows - 1) // rows
    n_blocks = n_real + 1 + (n_real + 1) % 2
    p = n_blocks * rows
    n_spare = p - m
    plane = -(-(n_tok + -(-n_spare // TOP_K)) // SUBLANES) * SUBLANES
    flat_e = idx.T.reshape(m)
    order = jnp.argsort(flat_e).astype(I32)
    padded = (counts + rows - 1) // rows * rows
    pad_end = jnp.cumsum(padded)
    pad_start = pad_end - padded
    start = jnp.cumsum(counts) - counts
    block_first = jnp.arange(n_blocks, dtype=I32) * rows
    block_e = jnp.minimum(jnp.sum((pad_end[None, :] <= block_first[:, None]).astype(I32), axis=1), N_EXPERTS - 1)
    slot = block_first[:, None] + jnp.arange(rows, dtype=I32)[None, :]
    q = slot - pad_start[block_e][:, None]
    cnt_b = counts[block_e][:, None]
    start_b = start[block_e][:, None]
    real = q < cnt_b
    assign = order[jnp.clip(start_b + q, 0, m - 1)]
    t_of, j_of = assign // TOP_K, assign % TOP_K
    spare_rank = rows + slot - (start_b + jnp.minimum(q, cnt_b))

    def spare_row(rank):
        return (rank % TOP_K) * plane + n_tok + rank // TOP_K

    tok = jnp.where(real, t_of, 0)
    dst = jnp.where(real, j_of * plane + t_of, spare_row(spare_rank))
    first_dst = spare_row(jnp.arange(rows, dtype=I32))[None, :]
    tokn = jnp.concatenate([tok[1:], jnp.zeros((1, rows), I32)], axis=0)
    dstp = jnp.concatenate([first_dst, dst[:-1]], axis=0)
    shape = (n_blocks, 1, rows)
    return block_e.astype(I32), tok[:1].reshape(1, 1, rows), tokn.reshape(shape), dstp.reshape(shape), plane


def _final_body(base_ref, y8_ref, w_ref, g_ref, b_ref, o_ref):
    w = w_ref[...]
    ffn = None
    tm = w.shape[0]
    ns = y8_ref.shape[1] // tm
    for j in range(TOP_K):
        yj = _load_packed_rows(y8_ref, (j,), tm, ns) * w[:, j:j + 1]
        ffn = yj if ffn is None else ffn + yj
    o_ref[...] = _layer_norm(base_ref[...] + ffn, g_ref[...], b_ref[...])


def _final_call(base, y8, w8, ln_g, ln_b, tm, row_offset):
    t, d = base.shape
    off = row_offset // tm
    ns = d // (2 * LANES)
    return pl.pallas_call(
        _final_body,
        grid=(t // tm,),
        in_specs=[
            pl.BlockSpec((tm, d), lambda i: (i, 0)),
            pl.BlockSpec((TOP_K, tm * ns, LANES), lambda i: (0, i + off, 0)),
            pl.BlockSpec((tm, TOP_K), lambda i: (i + off, 0)),
            pl.BlockSpec((1, d), lambda i: (0, 0)),
            pl.BlockSpec((1, d), lambda i: (0, 0)),
        ],
        out_specs=pl.BlockSpec((tm, d), lambda i: (i, 0)),
        out_shape=jax.ShapeDtypeStruct((t, d), F32),
        compiler_params=_params("parallel"),
        name="final",
    )(base, y8, w8, ln_g, ln_b)


def _mixer_and_post(x, pos_tables, w, tm, seq_rows, attend, conv_state, alpha):
    q, k, v = _qkv_call(x, w['qkv'], pos_tables, tm)
    attn = attend(q, k, v)
    cb, u_tail = _conv_call(x, w['b'], w['c'], w['h'], w['conv'], tm, seq_rows, conv_state)
    mp = _gate_call(x, attn, cb, w['ga'], w['gc'], w['attn_out'], w['conv_out'], tm)
    hq, base, logits_t = _post_call(x, mp, w['o'], w['ln1_g'], w['ln1_b'], w['router_t'],
                                    w['sh_gate'], w['sh_up'], w['sh_down'], min(tm, 2 * LANES), alpha)
    return k, v, u_tail, hq, base, logits_t


def kernel(x_prompt, x_sample, cache_k, cache_v, state_conv, w_in, attn_sinks, conv_w, w_attn_out, w_conv_out, w_o, ln1_g, ln1_b, w_router, router_bias, w_exp_gate, w_exp_up, w_exp_down, w_sh_gate, w_sh_up, w_sh_down, ln2_g, ln2_b):
    depth, d, _ = w_in.shape
    batch, seq, _ = x_prompt.shape
    dec_batch, dec_seq, _ = x_sample.shape
    win_buf = cache_k.shape[2]
    cdim = conv_w.shape[2]
    assert dec_seq == 1 and win_buf == WINDOW and seq % WINDOW == 0
    alpha = (2 * depth) ** 0.25
    t_p, t_s = batch * seq, dec_batch * dec_seq
    tm_p, tm_s = _tile(seq, 4 * LANES), _tile(t_s, LANES)
    assert t_p % tm_s == 0 and t_s % tm_s == 0
    tab_p = _rope_tables(jnp.arange(seq))
    tab_s = _rope_tables(jnp.full((tm_s,), PAST_LEN, I32))

    yp = x_prompt.reshape(t_p, d)
    ys = x_sample.reshape(t_s, d)
    p_k, p_v, p_c, s_k, s_v, s_c = [], [], [], [], [], []
    for l in range(depth):
        wl = w_in[l].astype(BF16)
        o = 0
        w = {}
        for name, width in (('qkv', Q_DIM + 2 * KV_DIM), ('b', cdim), ('c', cdim), ('h', cdim), ('ga', d), ('gc', d)):
            w[name] = wl[:, o:o + width]
            o += width
        w.update(
            conv=conv_w[l], attn_out=w_attn_out[l].astype(BF16), conv_out=w_conv_out[l].astype(BF16),
            o=w_o[l].astype(BF16), ln1_g=ln1_g[l][None], ln1_b=ln1_b[l][None],
            router_t=w_router[l].T.astype(BF16), sh_gate=w_sh_gate[l].astype(BF16),
            sh_up=w_sh_up[l].astype(BF16), sh_down=w_sh_down[l].astype(BF16))
        sinks = attn_sinks[l]

        k, v, u_tail, hq_p, base_p, lg_p = _mixer_and_post(
            yp, tab_p, w, tm_p, seq,
            lambda q, k, v: _attn_prompt_call(q, k, v, sinks, batch, seq), None, alpha)
        keep = min(WINDOW, seq)
        p_k.append(k.reshape(batch, seq, N_KV_HEADS, HEAD_DIM)[:, seq - keep:])
        p_v.append(v.reshape(batch, seq, N_KV_HEADS, HEAD_DIM)[:, seq - keep:])
        tails = u_tail.reshape(batch, seq // tm_p, SUBLANES, cdim)
        p_c.append(tails[:, -1, SUBLANES - (CONV_WIDTH - 1):])

        new_kv = {}

        def attend_sample(q, k, v, l=l):
            new_kv['k'] = jnp.concatenate([cache_k[l][:, 1:], k.reshape(t_s, 1, N_KV_HEADS, HEAD_DIM)], axis=1)
            new_kv['v'] = jnp.concatenate([cache_v[l][:, 1:], v.reshape(t_s, 1, N_KV_HEADS, HEAD_DIM)], axis=1)
            return _attn_sample_call(q, new_kv['k'].reshape(t_s, win_buf, KV_DIM),
                                     new_kv['v'].reshape(t_s, win_buf, KV_DIM), sinks)

        state = (state_conv[l][:, 0], state_conv[l][:, 1])
        _, _, u_s, hq_s, base_s, lg_s = _mixer_and_post(ys, tab_s, w, tm_s, 1, attend_sample, state, alpha)
        s_k.append(new_kv['k'])
        s_v.append(new_kv['v'])
        s_c.append(jnp.concatenate([state_conv[l][:, 1:], u_s[:, None]], axis=1))

        n_tok = t_p + t_s
        route_tile = SUBLANES * LANES
        t_pad = -(-n_tok // route_tile) * route_tile
        logits_t = jnp.concatenate([lg_p, lg_s, jnp.zeros((N_EXPERTS, t_pad - n_tok), F32)], axis=1)
        idx, wts, counts = _route_call(logits_t, router_bias[l], n_tok)
        idx, wts = idx[:, :n_tok], wts[:, :n_tok]
        block_e, tok0, tokn, dstp, plane = _dispatch_plan(idx, counts, n_tok, MOE_BLOCK_ROWS)
        hq = jnp.concatenate([hq_p, hq_s], axis=0)
        y8 = _moe_call(block_e, tok0, tokn, dstp, hq, w_exp_gate[l], w_exp_up[l], w_exp_down[l],
                       TOP_K * plane, MOE_BLOCK_ROWS)
        y8 = y8.reshape(TOP_K, plane * (d // (2 * LANES)), LANES)
        w8 = wts.T
        yp = _final_call(base_p, y8, w8, ln2_g[l][None], ln2_b[l][None], min(tm_p, 2 * LANES), 0)
        ys = _final_call(base_s, y8, w8, ln2_g[l][None], ln2_b[l][None], tm_s, t_p)

    return (yp.reshape(batch, seq, d), ys.reshape(dec_batch, dec_seq, d), jnp.stack(p_k), jnp.stack(p_v),
            jnp.stack(p_c), jnp.stack(s_k), jnp.stack(s_v), jnp.stack(s_c))
```

```python
import functools

import jax
import jax.numpy as jnp
from jax import lax
from jax.experimental import pallas as pl
from jax.experimental.pallas import tpu as pltpu

N_HEADS = 16
N_KV_HEADS = 4
HEAD_DIM = 64
GROUP = N_HEADS // N_KV_HEADS
Q_DIM = N_HEADS * HEAD_DIM
KV_DIM = N_KV_HEADS * HEAD_DIM
ROT_DIM = HEAD_DIM // 4
ROPE_THETA = 500000.0
WINDOW = 128
CONV_WIDTH = 3
PAST_LEN = 16384
N_EXPERTS = 64
N_GROUPS = 8
TOPK_GROUPS = 4
TOP_K = 8
ROUTED_SCALE = 2.5
LN_EPS = 1e-5
MOE_BLOCK_ROWS = 256

LANES = 128
SUBLANES = 8
VMEM_LIMIT_BYTES = 48 * 1024 * 1024
NEG_BIG = -1e30

F32 = jnp.float32
BF16 = jnp.bfloat16
I32 = jnp.int32


def _tile(n, pref):
    t = pref
    while t > 1 and n % t:
        t //= 2
    return t


def _params(*sem):
    return pltpu.CompilerParams(dimension_semantics=sem, vmem_limit_bytes=VMEM_LIMIT_BYTES)


def _dot(a, b):
    return jnp.dot(a, b, preferred_element_type=F32)


def _dot_nt(a, b):
    return lax.dot_general(a, b, (((1,), (1,)), ((), ())), preferred_element_type=F32)


def _layer_norm(x, g, b):
    mu = jnp.mean(x, axis=-1, keepdims=True)
    xc = x - mu
    var = jnp.mean(xc * xc, axis=-1, keepdims=True)
    return xc * lax.rsqrt(var + LN_EPS) * g + b


def _pack_pair(lo, hi):
    lo32 = lax.bitcast_convert_type(lo.astype(F32), I32)
    hi32 = lax.bitcast_convert_type(hi.astype(F32), I32)
    return lax.shift_right_logical(lo32, 16) | (hi32 & -65536)


def _unpack_pair(p):
    lo = lax.bitcast_convert_type(lax.shift_left(p, 16), F32)
    hi = lax.bitcast_convert_type(p & -65536, F32)
    return lo, hi


def _store_packed_rows(ref, lead, x, n):
    ns = x.shape[1] // (2 * LANES)
    for c in range(ns):
        lo = x[:, 2 * c * LANES:(2 * c + 1) * LANES]
        hi = x[:, (2 * c + 1) * LANES:(2 * c + 2) * LANES]
        ref[lead + (pl.ds(c, n, stride=ns), slice(None))] = _pack_pair(lo, hi)


def _load_packed_rows(ref, lead, n, ns):
    parts = []
    for c in range(ns):
        parts += list(_unpack_pair(ref[lead + (pl.ds(c, n, stride=ns), slice(None))]))
    return jnp.concatenate(parts, axis=1)


def _rope_tables(pos):
    half = ROT_DIM // 2
    n = pos.shape[0]
    inv_freq = jnp.power(jnp.float32(ROPE_THETA), -jnp.arange(half, dtype=F32) * (2.0 / ROT_DIM))
    ang = pos.astype(F32)[:, None] * inv_freq[None, :]
    cos, sin = jnp.cos(ang), jnp.sin(ang)
    rest = HEAD_DIM - ROT_DIM
    cos_h = jnp.concatenate([cos, cos, jnp.ones((n, rest), F32)], axis=1)
    sa_h = jnp.concatenate([jnp.zeros((n, half), F32), sin, jnp.zeros((n, rest), F32)], axis=1)
    sb_h = jnp.concatenate([-sin, jnp.zeros((n, half + rest), F32)], axis=1)
    rep = LANES // HEAD_DIM
    return tuple(jnp.concatenate([t] * rep, axis=1) for t in (cos_h, sa_h, sb_h))


def _qkv_body(x_ref, w_ref, cos_ref, sa_ref, sb_ref, q_ref, k_ref, v_ref, xb_ref, *, nq):
    j = pl.program_id(1)

    @pl.when(j == 0)
    def _():
        xb_ref[...] = x_ref[...].astype(BF16)

    acc = _dot(xb_ref[...], w_ref[...])
    half = ROT_DIM // 2

    def rope(a):
        cos, sa, sb = cos_ref[...], sa_ref[...], sb_ref[...]
        outs = []
        for c in range(a.shape[1] // LANES):
            blk = a[:, c * LANES:(c + 1) * LANES]
            outs.append(blk * cos + pltpu.roll(blk, half, 1) * sa + pltpu.roll(blk, LANES - half, 1) * sb)
        return jnp.concatenate(outs, axis=1)

    @pl.when(j < nq)
    def _():
        q_ref[...] = (rope(acc) * (HEAD_DIM ** -0.5)).astype(BF16)

    @pl.when(j == nq)
    def _():
        k_ref[...] = rope(acc[:, :KV_DIM])
        v_ref[...] = acc[:, KV_DIM:]


def _qkv_call(x, w_qkv, tables, tm):
    t, d = x.shape
    tn = 2 * KV_DIM
    assert Q_DIM % tn == 0
    nq = Q_DIM // tn
    tab_blocks = tables[0].shape[0] // tm
    tab_spec = pl.BlockSpec((tm, LANES), lambda i, j: (i % tab_blocks, 0))
    return pl.pallas_call(
        functools.partial(_qkv_body, nq=nq),
        grid=(t // tm, nq + 1),
        in_specs=[
            pl.BlockSpec((tm, d), lambda i, j: (i, 0)),
            pl.BlockSpec((d, tn), lambda i, j: (0, j)),
            tab_spec, tab_spec, tab_spec,
        ],
        out_specs=[
            pl.BlockSpec((tm, tn), lambda i, j: (i, jnp.minimum(j, nq - 1))),
            pl.BlockSpec((tm, KV_DIM), lambda i, j: (i, 0)),
            pl.BlockSpec((tm, KV_DIM), lambda i, j: (i, 0)),
        ],
        out_shape=[
            jax.ShapeDtypeStruct((t, Q_DIM), BF16),
            jax.ShapeDtypeStruct((t, KV_DIM), F32),
            jax.ShapeDtypeStruct((t, KV_DIM), F32),
        ],
        scratch_shapes=[pltpu.VMEM((tm, d), BF16)],
        compiler_params=_params("parallel", "arbitrary"),
        name="qkv",
    )(x, w_qkv, *tables)


def _head_pair_operands(kv_chunk, odd):
    lane = lax.broadcasted_iota(I32, kv_chunk.shape, 1)
    own = jnp.where((lane >= HEAD_DIM) == odd, kv_chunk, 0.0)
    other = pltpu.roll(own, HEAD_DIM, 1)
    lo, hi = (other, own) if odd else (own, other)
    return lo.astype(BF16), hi.astype(BF16)


def _attend(q, kk, vv, valid, sinks_ref):
    heads = []
    for kh in range(N_KV_HEADS):
        c = (kh * HEAD_DIM) // LANES
        odd = bool((kh * HEAD_DIM) % LANES)
        k_ops = _head_pair_operands(kk[:, c * LANES:(c + 1) * LANES], odd)
        v_ops = _head_pair_operands(vv[:, c * LANES:(c + 1) * LANES], odd)
        for g in range(GROUP):
            h = kh * GROUP + g
            heads.append((h // 2, h, k_ops[h % 2], v_ops[h % 2]))
    scores = [jnp.where(valid, _dot_nt(q[:, ch * LANES:(ch + 1) * LANES], k_op), NEG_BIG)
              for ch, _, k_op, _ in heads]
    maxes = [jnp.maximum(jnp.max(s, axis=-1, keepdims=True), sinks_ref[h]) for s, (_, h, _, _) in zip(scores, heads)]
    probs = [jnp.exp(s - m) for s, m in zip(scores, maxes)]
    dens = [jnp.sum(p, axis=-1, keepdims=True) + jnp.exp(sinks_ref[h] - m)
            for p, m, (_, h, _, _) in zip(probs, maxes, heads)]
    outs = [_dot((p / den).astype(BF16), v_op) for p, den, (_, _, _, v_op) in zip(probs, dens, heads)]
    return jnp.concatenate([outs[2 * j] + outs[2 * j + 1] for j in range(N_HEADS // 2)], axis=1)


def _attn_prompt_body(sinks_ref, q_ref, kp_ref, kc_ref, vp_ref, vc_ref, o_ref):
    n = pl.program_id(1)
    w = WINDOW
    kk = jnp.concatenate([kp_ref[...], kc_ref[...]], axis=0)
    vv = jnp.concatenate([vp_ref[...], vc_ref[...]], axis=0)
    a = lax.broadcasted_iota(I32, (w, 2 * w), 0)
    c = lax.broadcasted_iota(I32, (w, 2 * w), 1)
    valid = (c > a) & (c <= a + w) & ((n > 0) | (c >= w))
    o_ref[...] = _attend(q_ref[...], kk, vv, valid, sinks_ref).astype(BF16)


def _attn_prompt_call(q, k, v, sinks, batch, seq):
    w = WINDOW
    nb = seq // w
    cur = lambda b, n: (b * nb + n, 0)
    prev = lambda b, n: (b * nb + jnp.maximum(n - 1, 0), 0)
    return pl.pallas_call(
        _attn_prompt_body,
        grid=(batch, nb),
        in_specs=[
            pl.BlockSpec(memory_space=pltpu.SMEM),
            pl.BlockSpec((w, Q_DIM), cur),
            pl.BlockSpec((w, KV_DIM), prev),
            pl.BlockSpec((w, KV_DIM), cur),
            pl.BlockSpec((w, KV_DIM), prev),
            pl.BlockSpec((w, KV_DIM), cur),
        ],
        out_specs=pl.BlockSpec((w, Q_DIM), cur),
        out_shape=jax.ShapeDtypeStruct((batch * seq, Q_DIM), BF16),
        compiler_params=_params("parallel", "parallel"),
        name="attn_prompt",
    )(sinks, q, k, k, v, v)


def _attn_sample_body(sinks_ref, q_ref, k_ref, v_ref, o_ref, *, bt, nkeys):
    kk = k_ref[...].reshape(bt * nkeys, KV_DIM)
    vv = v_ref[...].reshape(bt * nkeys, KV_DIM)
    row_b = lax.broadcasted_iota(I32, (bt, bt * nkeys), 0)
    key_b = jnp.concatenate([jnp.full((bt, nkeys), b, I32) for b in range(bt)], axis=1)
    o_ref[...] = _attend(q_ref[...], kk, vv, row_b == key_b, sinks_ref).astype(BF16)


def _attn_sample_call(q, k_win, v_win, sinks):
    b, nkeys, _ = k_win.shape
    bt = _tile(b, SUBLANES)
    return pl.pallas_call(
        functools.partial(_attn_sample_body, bt=bt, nkeys=nkeys),
        grid=(b // bt,),
        in_specs=[
            pl.BlockSpec(memory_space=pltpu.SMEM),
            pl.BlockSpec((bt, Q_DIM), lambda i: (i, 0)),
            pl.BlockSpec((bt, nkeys, KV_DIM), lambda i: (i, 0, 0)),
            pl.BlockSpec((bt, nkeys, KV_DIM), lambda i: (i, 0, 0)),
        ],
        out_specs=pl.BlockSpec((bt, Q_DIM), lambda i: (i, 0)),
        out_shape=jax.ShapeDtypeStruct((b, Q_DIM), BF16),
        compiler_params=_params("parallel"),
        name="attn_sample",
    )(sinks, q, k_win, v_win)


def _conv_body(*refs, decode, tiles_per_seq, tm, tail):
    if decode:
        x_ref, wb_ref, wc_ref, wh_ref, cw_ref, s0_ref, s1_ref, cb_ref, ut_ref, xb_ref = refs
    else:
        x_ref, wb_ref, wc_ref, wh_ref, cw_ref, cb_ref, ut_ref, xb_ref, carry_ref = refs
    i = pl.program_id(0)
    c = pl.program_id(1)

    @pl.when(c == 0)
    def _():
        xb_ref[...] = x_ref[...].astype(BF16)

    xb = xb_ref[...]
    b_g = _dot(xb, wb_ref[...])
    u = _dot(xb, wc_ref[...]) * _dot(xb, wh_ref[...])
    if decode:
        u_m1, u_m2 = s1_ref[...], s0_ref[...]
    else:
        @pl.when(i % tiles_per_seq == 0)
        def _():
            carry_ref[c] = jnp.zeros(carry_ref.shape[1:], F32)

        prev = carry_ref[c]
        p_m2, p_m1 = prev[SUBLANES - 2:SUBLANES - 1], prev[SUBLANES - 1:SUBLANES]
        r = lax.broadcasted_iota(I32, u.shape, 0)
        u_m1 = jnp.where(r == 0, p_m1, pltpu.roll(u, 1, 0))
        u_m2 = jnp.where(r == 0, p_m2, jnp.where(r == 1, p_m1, pltpu.roll(u, 2, 0)))
        carry_ref[c] = u[tm - SUBLANES:]
    cw = cw_ref[...]
    conv = cw[0:1] * u_m2 + cw[1:2] * u_m1 + cw[2:3] * u
    cb_ref[...] = (b_g * conv).astype(BF16)
    ut_ref[...] = u[tm - tail:]


def _conv_call(x, w_b, w_c, w_h, conv_w, tm, seq_rows, state=None):
    t, d = x.shape
    cdim = w_b.shape[1]
    tc = _tile(cdim, 2 * LANES)
    decode = state is not None
    tail = tm if decode else SUBLANES
    w_spec = pl.BlockSpec((d, tc), lambda i, c: (0, c))
    in_specs = [pl.BlockSpec((tm, d), lambda i, c: (i, 0)), w_spec, w_spec, w_spec,
                pl.BlockSpec((CONV_WIDTH, tc), lambda i, c: (0, c))]
    args = [x, w_b, w_c, w_h, conv_w]
    scratch = [pltpu.VMEM((tm, d), BF16)]
    if decode:
        in_specs += [pl.BlockSpec((tm, tc), lambda i, c: (i, c))] * 2
        args += list(state)
    else:
        scratch.append(pltpu.VMEM((cdim // tc, SUBLANES, tc), F32))
    return pl.pallas_call(
        functools.partial(_conv_body, decode=decode, tiles_per_seq=max(seq_rows // tm, 1), tm=tm, tail=tail),
        grid=(t // tm, cdim // tc),
        in_specs=in_specs,
        out_specs=[pl.BlockSpec((tm, tc), lambda i, c: (i, c)), pl.BlockSpec((tail, tc), lambda i, c: (i, c))],
        out_shape=[jax.ShapeDtypeStruct((t, cdim), BF16), jax.ShapeDtypeStruct((t // tm * tail, cdim), F32)],
        scratch_shapes=scratch,
        compiler_params=_params("arbitrary", "arbitrary"),
        name="conv",
    )(*args)


def _gate_body(x_ref, at_ref, cb_ref, wga_ref, wgc_ref, wa_ref, wco_ref, o_ref, xb_ref):
    @pl.when(pl.program_id(1) == 0)
    def _():
        xb_ref[...] = x_ref[...].astype(BF16)

    xb = xb_ref[...]
    g_a = _dot(xb, wga_ref[...])
    g_c = _dot(xb, wgc_ref[...])
    a = _dot(at_ref[...], wa_ref[...])
    c = _dot(cb_ref[...], wco_ref[...])
    o_ref[...] = (jax.nn.sigmoid(g_a) * a + jax.nn.sigmoid(g_c) * c).astype(BF16)


def _gate_call(x, attn, cb, w_ga, w_gc, w_a, w_co, tm):
    t, d = x.shape
    tn = _tile(d, 4 * LANES)
    row = lambda i, n: (i, 0)
    col = lambda i, n: (0, n)
    return pl.pallas_call(
        _gate_body,
        grid=(t // tm, d // tn),
        in_specs=[
            pl.BlockSpec((tm, d), row),
            pl.BlockSpec((tm, attn.shape[1]), row),
            pl.BlockSpec((tm, cb.shape[1]), row),
            pl.BlockSpec((d, tn), col),
            pl.BlockSpec((d, tn), col),
            pl.BlockSpec((w_a.shape[0], tn), col),
            pl.BlockSpec((w_co.shape[0], tn), col),
        ],
        out_specs=pl.BlockSpec((tm, tn), lambda i, n: (i, n)),
        out_shape=jax.ShapeDtypeStruct((t, d), BF16),
        scratch_shapes=[pltpu.VMEM((tm, d), BF16)],
        compiler_params=_params("parallel", "arbitrary"),
        name="gate",
    )(x, attn, cb, w_ga, w_gc, w_a, w_co)


def _post_body(x_ref, mp_ref, wo_ref, g_ref, b_ref, wr_ref, wsg_ref, wsu_ref, wsd_ref, *rest, alpha):
    hq_ref, base_ref, lg_ref = rest[-3:]
    mixed = _dot(mp_ref[...], wo_ref[...])
    h = _layer_norm(alpha * x_ref[...] + mixed, g_ref[...], b_ref[...])
    hb = h.astype(BF16)
    lg_ref[...] = _dot_nt(wr_ref[...], hb)
    act = jax.nn.silu(_dot(hb, wsg_ref[...])) * _dot(hb, wsu_ref[...])
    base_ref[...] = alpha * h + _dot(act.astype(BF16), wsd_ref[...])
    _store_packed_rows(hq_ref, (), hb, hb.shape[0])


def _post_call(x, mp, w_o, ln_g, ln_b, w_r_t, w_sg, w_su, w_sd, tm, alpha, hq_tokens, hq_buf, hq_offset):
    t, d = x.shape
    n_e, f = w_r_t.shape[0], w_sg.shape[1]
    ns = d // (2 * LANES)
    off = hq_offset // tm
    row = lambda i: (i, 0)
    full = lambda i: (0, 0)
    in_specs = [
        pl.BlockSpec((tm, d), row),
        pl.BlockSpec((tm, d), row),
        pl.BlockSpec((d, d), full),
        pl.BlockSpec((1, d), full),
        pl.BlockSpec((1, d), full),
        pl.BlockSpec((n_e, d), full),
        pl.BlockSpec((d, f), full),
        pl.BlockSpec((d, f), full),
        pl.BlockSpec((f, d), full),
    ]
    args = [x, mp, w_o, ln_g, ln_b, w_r_t, w_sg, w_su, w_sd]
    aliases = {}
    if hq_buf is not None:
        aliases = {len(args): 0}
        in_specs.append(pl.BlockSpec(memory_space=pl.ANY))
        args.append(hq_buf)
    return pl.pallas_call(
        functools.partial(_post_body, alpha=alpha),
        grid=(t // tm,),
        in_specs=in_specs,
        out_specs=[
            pl.BlockSpec((tm * ns, LANES), lambda i: (i + off, 0)),
            pl.BlockSpec((tm, d), row),
            pl.BlockSpec((n_e, tm), lambda i: (0, i)),
        ],
        out_shape=[
            jax.ShapeDtypeStruct((hq_tokens * ns, LANES), I32),
            jax.ShapeDtypeStruct((t, d), F32),
            jax.ShapeDtypeStruct((n_e, t), F32),
        ],
        input_output_aliases=aliases,
        compiler_params=_params("parallel"),
        name="post",
    )(*args)


def _route_body(bias_ref, lg_ref, idx_ref, w_ref, cnt_ref, *, n_tok):
    per_group = N_EXPERTS // N_GROUPS
    neg_inf = jnp.float32(-jnp.inf)
    i = pl.program_id(0)
    scores = [jax.nn.sigmoid(lg_ref[e]) for e in range(N_EXPERTS)]
    choice = [scores[e] + bias_ref[e] for e in range(N_EXPERTS)]

    group_score = []
    for g in range(N_GROUPS):
        vals = choice[g * per_group:(g + 1) * per_group]
        m1 = functools.reduce(jnp.maximum, vals)
        m2 = jnp.full_like(m1, neg_inf)
        found = jnp.zeros(m1.shape, jnp.bool_)
        for v in vals:
            eq = v == m1
            m2 = jnp.maximum(m2, jnp.where(eq & ~found, neg_inf, v))
            found = found | eq
        group_score.append(m1 + m2)

    masked = []
    for g in range(N_GROUPS):
        ahead = jnp.zeros(group_score[g].shape, I32)
        for o in range(N_GROUPS):
            if o == g:
                continue
            beats = group_score[o] > group_score[g]
            if o < g:
                beats = beats | (group_score[o] == group_score[g])
            ahead = ahead + beats.astype(I32)
        keep = ahead < TOPK_GROUPS
        masked += [jnp.where(keep, choice[e], neg_inf) for e in range(g * per_group, (g + 1) * per_group)]

    shape = masked[0].shape
    token = (i * SUBLANES + lax.broadcasted_iota(I32, shape, 0)) * LANES + lax.broadcasted_iota(I32, shape, 1)
    real = (token < n_tok).astype(I32)
    chosen = [jnp.zeros(shape, I32) for _ in range(N_EXPERTS)]
    picked_w = []
    for r in range(TOP_K):
        best = functools.reduce(jnp.maximum, masked)
        sel = jnp.full(shape, N_EXPERTS, I32)
        for e in reversed(range(N_EXPERTS)):
            sel = jnp.where(masked[e] == best, e, sel)
        w = jnp.zeros(shape, F32)
        for e in range(N_EXPERTS):
            hit = sel == e
            w = jnp.where(hit, scores[e], w)
            masked[e] = jnp.where(hit, neg_inf, masked[e])
            chosen[e] = jnp.where(hit, real, chosen[e])
        idx_ref[r] = sel
        picked_w.append(w)
    total = functools.reduce(lambda a, b: a + b, picked_w)
    for r in range(TOP_K):
        w_ref[r] = picked_w[r] / total * ROUTED_SCALE

    @pl.when(i == 0)
    def _():
        cnt_ref[...] = jnp.zeros(cnt_ref.shape, I32)

    for e in range(N_EXPERTS):
        cnt_ref[e] += chosen[e]


def _route_call(logits_t, bias, n_tok):
    n_e, t = logits_t.shape
    rows = t // LANES
    lg3 = logits_t.reshape(n_e, rows, LANES)
    pick = pl.BlockSpec((TOP_K, SUBLANES, LANES), lambda i: (0, i, 0))
    idx, w, cnt = pl.pallas_call(
        functools.partial(_route_body, n_tok=n_tok),
        grid=(rows // SUBLANES,),
        in_specs=[
            pl.BlockSpec(memory_space=pltpu.SMEM),
            pl.BlockSpec((n_e, SUBLANES, LANES), lambda i: (0, i, 0)),
        ],
        out_specs=[pick, pick, pl.BlockSpec((n_e, SUBLANES, LANES), lambda i: (0, 0, 0))],
        out_shape=[
            jax.ShapeDtypeStruct((TOP_K, rows, LANES), I32),
            jax.ShapeDtypeStruct((TOP_K, rows, LANES), F32),
            jax.ShapeDtypeStruct((n_e, SUBLANES, LANES), I32),
        ],
        compiler_params=_params("arbitrary"),
        name="route",
    )(bias, lg3)
    return idx.reshape(TOP_K, t), w.reshape(TOP_K, t), jnp.sum(cnt, axis=(1, 2))


def _moe_body(be_ref, tok0_ref, tokn_ref, dstp_ref, hq_hbm, wg_ref, wu_ref, wd_ref, out_hbm,
              xb0, xb1, yb0, yb1, wgb, wub, wdb, gsem, ssem, *, rows):
    b = pl.program_id(0)
    nb = pl.num_programs(0)
    xbufs, ybufs = (xb0, xb1), (yb0, yb1)
    ns = xb0.shape[0] // rows

    def token(ref, i):
        return ref.at[pl.ds(pl.multiple_of(i * ns, ns), ns), :]

    def gather(idx_ref, dst_buf, sem):
        for r in range(rows):
            pltpu.make_async_copy(token(hq_hbm, idx_ref[0, 0, r]), token(dst_buf, r), sem).start(priority=r % 2)

    def gather_done(s):
        return pltpu.make_async_copy(hq_hbm.at[pl.ds(0, rows * ns), :], xbufs[s], gsem.at[s])

    def scatter_done(s):
        return pltpu.make_async_copy(ybufs[s], out_hbm.at[pl.ds(0, rows * ns), :], ssem.at[s])

    @pl.when(b == 0)
    def _():
        gather(tok0_ref, xb0, gsem.at[0])
        yb1[...] = jnp.zeros(yb1.shape, I32)

    @pl.when((b == 0) | (be_ref[b] != be_ref[jnp.maximum(b - 1, 0)]))
    def _():
        wgb[...] = wg_ref[...].astype(BF16)
        wub[...] = wu_ref[...].astype(BF16)
        wdb[...] = wd_ref[...].astype(BF16)

    def block(s):
        o = 1 - s
        gather_done(s).wait()
        gather(tokn_ref, xbufs[o], gsem.at[o])
        for r in range(rows):
            pltpu.make_async_copy(token(ybufs[o], r), token(out_hbm, dstp_ref[0, 0, r]),
                                  ssem.at[o]).start(priority=r % 2)
        xb = _load_packed_rows(xbufs[s], (), rows, ns).astype(BF16)
        act = jax.nn.silu(_dot(xb, wgb[...])) * _dot(xb, wub[...])
        y = _dot(act.astype(BF16), wdb[...]).astype(BF16)

        @pl.when(b >= 1)
        def _():
            scatter_done(s).wait()

        _store_packed_rows(ybufs[s], (), y, rows)

    for s in range(2):
        @pl.when(b % 2 == s)
        def _(s=s):
            block(s)

    @pl.when(b == nb - 1)
    def _():
        gather_done(0).wait()
        scatter_done(0).wait()


def _moe_call(block_e, tok0, tokn, dstp, hq, w_gate, w_up, w_down, out_rows, rows):
    n_blocks = block_e.shape[0]
    assert n_blocks % 2 == 0
    n_e, d, f = w_gate.shape
    ns = d // (2 * LANES)
    per_block = pl.BlockSpec((1, 1, rows), lambda b, be: (b, 0, 0), memory_space=pltpu.SMEM)
    return pl.pallas_call(
        functools.partial(_moe_body, rows=rows),
        grid_spec=pltpu.PrefetchScalarGridSpec(
            num_scalar_prefetch=1,
            grid=(n_blocks,),
            in_specs=[
                pl.BlockSpec((1, 1, rows), lambda b, be: (0, 0, 0), memory_space=pltpu.SMEM),
                per_block, per_block,
                pl.BlockSpec(memory_space=pl.ANY),
                pl.BlockSpec((None, d, f), lambda b, be: (be[b], 0, 0)),
                pl.BlockSpec((None, d, f), lambda b, be: (be[b], 0, 0)),
                pl.BlockSpec((None, f, d), lambda b, be: (be[b], 0, 0)),
            ],
            out_specs=pl.BlockSpec(memory_space=pl.ANY),
            scratch_shapes=[pltpu.VMEM((rows * ns, LANES), I32)] * 4 + [
                pltpu.VMEM((d, f), BF16),
                pltpu.VMEM((d, f), BF16),
                pltpu.VMEM((f, d), BF16),
                pltpu.SemaphoreType.DMA((2,)),
                pltpu.SemaphoreType.DMA((2,)),
            ],
        ),
        out_shape=jax.ShapeDtypeStruct((out_rows * ns, LANES), I32),
        compiler_params=_params("arbitrary"),
        name="moe",
    )(block_e, tok0, tokn, dstp, hq, w_gate, w_up, w_down)


def _dispatch_plan(idx, counts, n_tok, rows):
    m = n_tok * TOP_K
    n_real = (m + N_EXPERTS * (rows - 1) + rows - 1) // rows
    n_blocks = n_real + 1 + (n_real + 1) % 2
    p = n_blocks * rows
    n_spare = p - m
    plane = -(-(n_tok + -(-n_spare // TOP_K)) // SUBLANES) * SUBLANES
    flat_e = idx.T.reshape(m)
    order = jnp.argsort(flat_e).astype(I32)
    padded = (counts + rows - 1) // rows * rows
    pad_end = jnp.cumsum(padded)
    pad_start = pad_end - padded
    start = jnp.cumsum(counts) - counts
    block_first = jnp.arange(n_blocks, dtype=I32) * rows
    block_e = jnp.minimum(jnp.sum((pad_end[None, :] <= block_first[:, None]).astype(I32), axis=1), N_EXPERTS - 1)
    slot = block_first[:, None] + jnp.arange(rows, dtype=I32)[None, :]
    q = slot - pad_start[block_e][:, None]
    cnt_b = counts[block_e][:, None]
    start_b = start[block_e][:, None]
    real = q < cnt_b
    assign = order[jnp.clip(start_b + q, 0, m - 1)]
    t_of, j_of = assign // TOP_K, assign % TOP_K
    spare_rank = rows + slot - (start_b + jnp.minimum(q, cnt_b))

    def spare_row(rank):
        return (rank % TOP_K) * plane + n_tok + rank // TOP_K

    tok = jnp.where(real, t_of, 0)
    dst = jnp.where(real, j_of * plane + t_of, spare_row(spare_rank))
    first_dst = spare_row(jnp.arange(rows, dtype=I32))[None, :]
    tokn = jnp.concatenate([tok[1:], jnp.zeros((1, rows), I32)], axis=0)
    dstp = jnp.concatenate([first_dst, dst[:-1]], axis=0)
    shape = (n_blocks, 1, rows)
    return block_e.astype(I32), tok[:1].reshape(1, 1, rows), tokn.reshape(shape), dstp.reshape(shape), plane


def _final_body(base_ref, y8_ref, w_ref, g_ref, b_ref, o_ref):
    w = w_ref[...]
    ffn = None
    tm = w.shape[0]
    ns = y8_ref.shape[1] // tm
    for j in range(TOP_K):
        yj = _load_packed_rows(y8_ref, (j,), tm, ns) * w[:, j:j + 1]
        ffn = yj if ffn is None else ffn + yj
    o_ref[...] = _layer_norm(base_ref[...] + ffn, g_ref[...], b_ref[...])


def _final_call(base, y8, w8, ln_g, ln_b, tm, row_offset):
    t, d = base.shape
    off = row_offset // tm
    ns = d // (2 * LANES)
    return pl.pallas_call(
        _final_body,
        grid=(t // tm,),
        in_specs=[
            pl.BlockSpec((tm, d), lambda i: (i, 0)),
            pl.BlockSpec((TOP_K, tm * ns, LANES), lambda i: (0, i + off, 0)),
            pl.BlockSpec((tm, TOP_K), lambda i: (i + off, 0)),
            pl.BlockSpec((1, d), lambda i: (0, 0)),
            pl.BlockSpec((1, d), lambda i: (0, 0)),
        ],
        out_specs=pl.BlockSpec((tm, d), lambda i: (i, 0)),
        out_shape=jax.ShapeDtypeStruct((t, d), F32),
        compiler_params=_params("parallel"),
        name="final",
    )(base, y8, w8, ln_g, ln_b)


def _mixer_and_post(x, pos_tables, w, tm, seq_rows, attend, conv_state, alpha, hq_tokens, hq_buf, hq_offset):
    tm_qkv = 2 * tm if pos_tables[0].shape[0] % (2 * tm) == 0 else tm
    q, k, v = _qkv_call(x, w['qkv'], pos_tables, tm_qkv)
    attn = attend(q, k, v)
    cb, u_tail = _conv_call(x, w['b'], w['c'], w['h'], w['conv'], tm, seq_rows, conv_state)
    mp = _gate_call(x, attn, cb, w['ga'], w['gc'], w['attn_out'], w['conv_out'], tm)
    hq, base, logits_t = _post_call(x, mp, w['o'], w['ln1_g'], w['ln1_b'], w['router_t'],
                                    w['sh_gate'], w['sh_up'], w['sh_down'], min(tm, 2 * LANES), alpha,
                                    hq_tokens, hq_buf, hq_offset)
    return k, v, u_tail, hq, base, logits_t


def kernel(x_prompt, x_sample, cache_k, cache_v, state_conv, w_in, attn_sinks, conv_w, w_attn_out, w_conv_out, w_o, ln1_g, ln1_b, w_router, router_bias, w_exp_gate, w_exp_up, w_exp_down, w_sh_gate, w_sh_up, w_sh_down, ln2_g, ln2_b):
    depth, d, _ = w_in.shape
    batch, seq, _ = x_prompt.shape
    dec_batch, dec_seq, _ = x_sample.shape
    win_buf = cache_k.shape[2]
    cdim = conv_w.shape[2]
    assert dec_seq == 1 and win_buf == WINDOW and seq % WINDOW == 0
    alpha = (2 * depth) ** 0.25
    t_p, t_s = batch * seq, dec_batch * dec_seq
    tm_p, tm_s = _tile(seq, 4 * LANES), _tile(t_s, LANES)
    assert t_p % tm_s == 0 and t_s % tm_s == 0
    tab_p = _rope_tables(jnp.arange(seq))
    tab_s = _rope_tables(jnp.full((tm_s,), PAST_LEN, I32))

    yp = x_prompt.reshape(t_p, d)
    ys = x_sample.reshape(t_s, d)
    p_k, p_v, p_c, s_k, s_v, s_c = [], [], [], [], [], []
    for l in range(depth):
        wl = w_in[l].astype(BF16)
        o = 0
        w = {}
        for name, width in (('qkv', Q_DIM + 2 * KV_DIM), ('b', cdim), ('c', cdim), ('h', cdim), ('ga', d), ('gc', d)):
            w[name] = wl[:, o:o + width]
            o += width
        w.update(
            conv=conv_w[l], attn_out=w_attn_out[l].astype(BF16), conv_out=w_conv_out[l].astype(BF16),
            o=w_o[l].astype(BF16), ln1_g=ln1_g[l][None], ln1_b=ln1_b[l][None],
            router_t=w_router[l].T.astype(BF16), sh_gate=w_sh_gate[l].astype(BF16),
            sh_up=w_sh_up[l].astype(BF16), sh_down=w_sh_down[l].astype(BF16))
        sinks = attn_sinks[l]

        n_tok = t_p + t_s
        k, v, u_tail, hq_p, base_p, lg_p = _mixer_and_post(
            yp, tab_p, w, tm_p, seq,
            lambda q, k, v: _attn_prompt_call(q, k, v, sinks, batch, seq), None, alpha, n_tok, None, 0)
        keep = min(WINDOW, seq)
        p_k.append(k.reshape(batch, seq, N_KV_HEADS, HEAD_DIM)[:, seq - keep:])
        p_v.append(v.reshape(batch, seq, N_KV_HEADS, HEAD_DIM)[:, seq - keep:])
        tails = u_tail.reshape(batch, seq // tm_p, SUBLANES, cdim)
        p_c.append(tails[:, -1, SUBLANES - (CONV_WIDTH - 1):])

        new_kv = {}

        def attend_sample(q, k, v, l=l):
            new_kv['k'] = jnp.concatenate([cache_k[l][:, 1:], k.reshape(t_s, 1, N_KV_HEADS, HEAD_DIM)], axis=1)
            new_kv['v'] = jnp.concatenate([cache_v[l][:, 1:], v.reshape(t_s, 1, N_KV_HEADS, HEAD_DIM)], axis=1)
            return _attn_sample_call(q, new_kv['k'].reshape(t_s, win_buf, KV_DIM),
                                     new_kv['v'].reshape(t_s, win_buf, KV_DIM), sinks)

        state = (state_conv[l][:, 0], state_conv[l][:, 1])
        _, _, u_s, hq, base_s, lg_s = _mixer_and_post(ys, tab_s, w, tm_s, 1, attend_sample, state, alpha,
                                                      n_tok, hq_p, t_p)
        s_k.append(new_kv['k'])
        s_v.append(new_kv['v'])
        s_c.append(jnp.concatenate([state_conv[l][:, 1:], u_s[:, None]], axis=1))

        route_tile = SUBLANES * LANES
        t_pad = -(-n_tok // route_tile) * route_tile
        logits_t = jnp.concatenate([lg_p, lg_s, jnp.zeros((N_EXPERTS, t_pad - n_tok), F32)], axis=1)
        idx, wts, counts = _route_call(logits_t, router_bias[l], n_tok)
        idx, wts = idx[:, :n_tok], wts[:, :n_tok]
        block_e, tok0, tokn, dstp, plane = _dispatch_plan(idx, counts, n_tok, MOE_BLOCK_ROWS)
        y8 = _moe_call(block_e, tok0, tokn, dstp, hq, w_exp_gate[l], w_exp_up[l], w_exp_down[l],
                       TOP_K * plane, MOE_BLOCK_ROWS)
        y8 = y8.reshape(TOP_K, plane * (d // (2 * LANES)), LANES)
        w8 = wts.T
        yp = _final_call(base_p, y8, w8, ln2_g[l][None], ln2_b[l][None], min(tm_p, 2 * LANES), 0)
        ys = _final_call(base_s, y8, w8, ln2_g[l][None], ln2_b[l][None], tm_s, t_p)

    return (yp.reshape(batch, seq, d), ys.reshape(dec_batch, dec_seq, d), jnp.stack(p_k), jnp.stack(p_v),
            jnp.stack(p_c), jnp.stack(s_k), jnp.stack(s_v), jnp.stack(s_c))
```

```python
import functools

import jax
import jax.numpy as jnp
from jax import lax
from jax.experimental import pallas as pl
from jax.experimental.pallas import tpu as pltpu

N_HEADS = 16
N_KV_HEADS = 4
HEAD_DIM = 64
GROUP = N_HEADS // N_KV_HEADS
Q_DIM = N_HEADS * HEAD_DIM
KV_DIM = N_KV_HEADS * HEAD_DIM
ROT_DIM = HEAD_DIM // 4
ROPE_THETA = 500000.0
WINDOW = 128
CONV_WIDTH = 3
PAST_LEN = 16384
N_EXPERTS = 64
N_GROUPS = 8
TOPK_GROUPS = 4
TOP_K = 8
ROUTED_SCALE = 2.5
LN_EPS = 1e-5
MOE_BLOCK_ROWS = 256
MOE_F_CHUNK = 256
MOE_D_CHUNK = 512

LANES = 128
SUBLANES = 8
VMEM_LIMIT_BYTES = 48 * 1024 * 1024
NEG_BIG = -1e30

F32 = jnp.float32
BF16 = jnp.bfloat16
I32 = jnp.int32


def _tile(n, pref):
    t = pref
    while t > 1 and n % t:
        t //= 2
    return t


def _params(*sem):
    return pltpu.CompilerParams(dimension_semantics=sem, vmem_limit_bytes=VMEM_LIMIT_BYTES)


def _dot(a, b):
    return jnp.dot(a, b, preferred_element_type=F32)


def _dot_nt(a, b):
    return lax.dot_general(a, b, (((1,), (1,)), ((), ())), preferred_element_type=F32)


def _layer_norm(x, g, b):
    mu = jnp.mean(x, axis=-1, keepdims=True)
    xc = x - mu
    var = jnp.mean(xc * xc, axis=-1, keepdims=True)
    return xc * lax.rsqrt(var + LN_EPS) * g + b


def _pack_pair(lo, hi):
    lo32 = lax.bitcast_convert_type(lo.astype(F32), I32)
    hi32 = lax.bitcast_convert_type(hi.astype(F32), I32)
    return lax.shift_right_logical(lo32, 16) | (hi32 & -65536)


def _unpack_pair(p):
    lo = lax.bitcast_convert_type(lax.shift_left(p, 16), F32)
    hi = lax.bitcast_convert_type(p & -65536, F32)
    return lo, hi


def _store_packed_rows(ref, lead, x, n):
    ns = x.shape[1] // (2 * LANES)
    for c in range(ns):
        lo = x[:, 2 * c * LANES:(2 * c + 1) * LANES]
        hi = x[:, (2 * c + 1) * LANES:(2 * c + 2) * LANES]
        ref[lead + (pl.ds(c, n, stride=ns), slice(None))] = _pack_pair(lo, hi)


def _load_packed_rows(ref, lead, n, ns):
    parts = []
    for c in range(ns):
        parts += list(_unpack_pair(ref[lead + (pl.ds(c, n, stride=ns), slice(None))]))
    return jnp.concatenate(parts, axis=1)


def _rope_tables(pos):
    half = ROT_DIM // 2
    n = pos.shape[0]
    inv_freq = jnp.power(jnp.float32(ROPE_THETA), -jnp.arange(half, dtype=F32) * (2.0 / ROT_DIM))
    ang = pos.astype(F32)[:, None] * inv_freq[None, :]
    cos, sin = jnp.cos(ang), jnp.sin(ang)
    rest = HEAD_DIM - ROT_DIM
    cos_h = jnp.concatenate([cos, cos, jnp.ones((n, rest), F32)], axis=1)
    sa_h = jnp.concatenate([jnp.zeros((n, half), F32), sin, jnp.zeros((n, rest), F32)], axis=1)
    sb_h = jnp.concatenate([-sin, jnp.zeros((n, half + rest), F32)], axis=1)
    rep = LANES // HEAD_DIM
    return tuple(jnp.concatenate([t] * rep, axis=1) for t in (cos_h, sa_h, sb_h))


def _qkv_body(x_ref, w_ref, cos_ref, sa_ref, sb_ref, q_ref, k_ref, v_ref, xb_ref, *, nq):
    j = pl.program_id(1)

    @pl.when(j == 0)
    def _():
        xb_ref[...] = x_ref[...].astype(BF16)

    acc = _dot(xb_ref[...], w_ref[...])
    half = ROT_DIM // 2

    def rope(a):
        cos, sa, sb = cos_ref[...], sa_ref[...], sb_ref[...]
        outs = []
        for c in range(a.shape[1] // LANES):
            blk = a[:, c * LANES:(c + 1) * LANES]
            outs.append(blk * cos + pltpu.roll(blk, half, 1) * sa + pltpu.roll(blk, LANES - half, 1) * sb)
        return jnp.concatenate(outs, axis=1)

    @pl.when(j < nq)
    def _():
        q_ref[...] = (rope(acc) * (HEAD_DIM ** -0.5)).astype(BF16)

    @pl.when(j == nq)
    def _():
        k_ref[...] = rope(acc[:, :KV_DIM])
        v_ref[...] = acc[:, KV_DIM:]


def _qkv_call(x, w_qkv, tables, tm):
    t, d = x.shape
    tn = 2 * KV_DIM
    assert Q_DIM % tn == 0
    nq = Q_DIM // tn
    tab_blocks = tables[0].shape[0] // tm
    tab_spec = pl.BlockSpec((tm, LANES), lambda i, j: (i % tab_blocks, 0))
    return pl.pallas_call(
        functools.partial(_qkv_body, nq=nq),
        grid=(t // tm, nq + 1),
        in_specs=[
            pl.BlockSpec((tm, d), lambda i, j: (i, 0)),
            pl.BlockSpec((d, tn), lambda i, j: (0, j)),
            tab_spec, tab_spec, tab_spec,
        ],
        out_specs=[
            pl.BlockSpec((tm, tn), lambda i, j: (i, jnp.minimum(j, nq - 1))),
            pl.BlockSpec((tm, KV_DIM), lambda i, j: (i, 0)),
            pl.BlockSpec((tm, KV_DIM), lambda i, j: (i, 0)),
        ],
        out_shape=[
            jax.ShapeDtypeStruct((t, Q_DIM), BF16),
            jax.ShapeDtypeStruct((t, KV_DIM), F32),
            jax.ShapeDtypeStruct((t, KV_DIM), F32),
        ],
        scratch_shapes=[pltpu.VMEM((tm, d), BF16)],
        compiler_params=_params("parallel", "arbitrary"),
        name="qkv",
    )(x, w_qkv, *tables)


def _head_pair_operands(kv_chunk, odd):
    lane = lax.broadcasted_iota(I32, kv_chunk.shape, 1)
    own = jnp.where((lane >= HEAD_DIM) == odd, kv_chunk, 0.0)
    other = pltpu.roll(own, HEAD_DIM, 1)
    lo, hi = (other, own) if odd else (own, other)
    return lo.astype(BF16), hi.astype(BF16)


def _attend(q, kk, vv, valid, sinks_ref):
    heads = []
    for kh in range(N_KV_HEADS):
        c = (kh * HEAD_DIM) // LANES
        odd = bool((kh * HEAD_DIM) % LANES)
        k_ops = _head_pair_operands(kk[:, c * LANES:(c + 1) * LANES], odd)
        v_ops = _head_pair_operands(vv[:, c * LANES:(c + 1) * LANES], odd)
        for g in range(GROUP):
            h = kh * GROUP + g
            heads.append((h // 2, h, k_ops[h % 2], v_ops[h % 2]))
    scores = [jnp.where(valid, _dot_nt(q[:, ch * LANES:(ch + 1) * LANES], k_op), NEG_BIG)
              for ch, _, k_op, _ in heads]
    maxes = [jnp.maximum(jnp.max(s, axis=-1, keepdims=True), sinks_ref[h]) for s, (_, h, _, _) in zip(scores, heads)]
    probs = [jnp.exp(s - m) for s, m in zip(scores, maxes)]
    dens = [jnp.sum(p, axis=-1, keepdims=True) + jnp.exp(sinks_ref[h] - m)
            for p, m, (_, h, _, _) in zip(probs, maxes, heads)]
    outs = [_dot((p / den).astype(BF16), v_op) for p, den, (_, _, _, v_op) in zip(probs, dens, heads)]
    return jnp.concatenate([outs[2 * j] + outs[2 * j + 1] for j in range(N_HEADS // 2)], axis=1)


def _attn_prompt_body(sinks_ref, q_ref, kp_ref, kc_ref, vp_ref, vc_ref, o_ref):
    n = pl.program_id(1)
    w = WINDOW
    kk = jnp.concatenate([kp_ref[...], kc_ref[...]], axis=0)
    vv = jnp.concatenate([vp_ref[...], vc_ref[...]], axis=0)
    a = lax.broadcasted_iota(I32, (w, 2 * w), 0)
    c = lax.broadcasted_iota(I32, (w, 2 * w), 1)
    valid = (c > a) & (c <= a + w) & ((n > 0) | (c >= w))
    o_ref[...] = _attend(q_ref[...], kk, vv, valid, sinks_ref).astype(BF16)


def _attn_prompt_call(q, k, v, sinks, batch, seq):
    w = WINDOW
    nb = seq // w
    cur = lambda b, n: (b * nb + n, 0)
    prev = lambda b, n: (b * nb + jnp.maximum(n - 1, 0), 0)
    return pl.pallas_call(
        _attn_prompt_body,
        grid=(batch, nb),
        in_specs=[
            pl.BlockSpec(memory_space=pltpu.SMEM),
            pl.BlockSpec((w, Q_DIM), cur),
            pl.BlockSpec((w, KV_DIM), prev),
            pl.BlockSpec((w, KV_DIM), cur),
            pl.BlockSpec((w, KV_DIM), prev),
            pl.BlockSpec((w, KV_DIM), cur),
        ],
        out_specs=pl.BlockSpec((w, Q_DIM), cur),
        out_shape=jax.ShapeDtypeStruct((batch * seq, Q_DIM), BF16),
        compiler_params=_params("parallel", "parallel"),
        name="attn_prompt",
    )(sinks, q, k, k, v, v)


def _attn_sample_body(sinks_ref, q_ref, k_ref, v_ref, o_ref, *, bt, nkeys):
    kk = k_ref[...].reshape(bt * nkeys, KV_DIM)
    vv = v_ref[...].reshape(bt * nkeys, KV_DIM)
    row_b = lax.broadcasted_iota(I32, (bt, bt * nkeys), 0)
    key_b = jnp.concatenate([jnp.full((bt, nkeys), b, I32) for b in range(bt)], axis=1)
    o_ref[...] = _attend(q_ref[...], kk, vv, row_b == key_b, sinks_ref).astype(BF16)


def _attn_sample_call(q, k_win, v_win, sinks):
    b, nkeys, _ = k_win.shape
    bt = _tile(b, SUBLANES)
    return pl.pallas_call(
        functools.partial(_attn_sample_body, bt=bt, nkeys=nkeys),
        grid=(b // bt,),
        in_specs=[
            pl.BlockSpec(memory_space=pltpu.SMEM),
            pl.BlockSpec((bt, Q_DIM), lambda i: (i, 0)),
            pl.BlockSpec((bt, nkeys, KV_DIM), lambda i: (i, 0, 0)),
            pl.BlockSpec((bt, nkeys, KV_DIM), lambda i: (i, 0, 0)),
        ],
        out_specs=pl.BlockSpec((bt, Q_DIM), lambda i: (i, 0)),
        out_shape=jax.ShapeDtypeStruct((b, Q_DIM), BF16),
        compiler_params=_params("parallel"),
        name="attn_sample",
    )(sinks, q, k_win, v_win)


def _conv_body(*refs, decode, tiles_per_seq, tm, tail):
    if decode:
        x_ref, wb_ref, wc_ref, wh_ref, cw_ref, s0_ref, s1_ref, cb_ref, ut_ref, xb_ref = refs
    else:
        x_ref, wb_ref, wc_ref, wh_ref, cw_ref, cb_ref, ut_ref, xb_ref, carry_ref = refs
    i = pl.program_id(0)
    c = pl.program_id(1)

    @pl.when(c == 0)
    def _():
        xb_ref[...] = x_ref[...].astype(BF16)

    xb = xb_ref[...]
    b_g = _dot(xb, wb_ref[...])
    u = _dot(xb, wc_ref[...]) * _dot(xb, wh_ref[...])
    if decode:
        u_m1, u_m2 = s1_ref[...], s0_ref[...]
    else:
        @pl.when(i % tiles_per_seq == 0)
        def _():
            carry_ref[c] = jnp.zeros(carry_ref.shape[1:], F32)

        prev = carry_ref[c]
        p_m2, p_m1 = prev[SUBLANES - 2:SUBLANES - 1], prev[SUBLANES - 1:SUBLANES]
        r = lax.broadcasted_iota(I32, u.shape, 0)
        u_m1 = jnp.where(r == 0, p_m1, pltpu.roll(u, 1, 0))
        u_m2 = jnp.where(r == 0, p_m2, jnp.where(r == 1, p_m1, pltpu.roll(u, 2, 0)))
        carry_ref[c] = u[tm - SUBLANES:]
    cw = cw_ref[...]
    conv = cw[0:1] * u_m2 + cw[1:2] * u_m1 + cw[2:3] * u
    cb_ref[...] = (b_g * conv).astype(BF16)
    ut_ref[...] = u[tm - tail:]


def _conv_call(x, w_b, w_c, w_h, conv_w, tm, seq_rows, state=None):
    t, d = x.shape
    cdim = w_b.shape[1]
    tc = _tile(cdim, 2 * LANES)
    decode = state is not None
    tail = tm if decode else SUBLANES
    w_spec = pl.BlockSpec((d, tc), lambda i, c: (0, c))
    in_specs = [pl.BlockSpec((tm, d), lambda i, c: (i, 0)), w_spec, w_spec, w_spec,
                pl.BlockSpec((CONV_WIDTH, tc), lambda i, c: (0, c))]
    args = [x, w_b, w_c, w_h, conv_w]
    scratch = [pltpu.VMEM((tm, d), BF16)]
    if decode:
        in_specs += [pl.BlockSpec((tm, tc), lambda i, c: (i, c))] * 2
        args += list(state)
    else:
        scratch.append(pltpu.VMEM((cdim // tc, SUBLANES, tc), F32))
    return pl.pallas_call(
        functools.partial(_conv_body, decode=decode, tiles_per_seq=max(seq_rows // tm, 1), tm=tm, tail=tail),
        grid=(t // tm, cdim // tc),
        in_specs=in_specs,
        out_specs=[pl.BlockSpec((tm, tc), lambda i, c: (i, c)), pl.BlockSpec((tail, tc), lambda i, c: (i, c))],
        out_shape=[jax.ShapeDtypeStruct((t, cdim), BF16), jax.ShapeDtypeStruct((t // tm * tail, cdim), F32)],
        scratch_shapes=scratch,
        compiler_params=_params("arbitrary", "arbitrary"),
        name="conv",
    )(*args)


def _gate_body(x_ref, at_ref, cb_ref, wga_ref, wgc_ref, wa_ref, wco_ref, o_ref, xb_ref):
    @pl.when(pl.program_id(1) == 0)
    def _():
        xb_ref[...] = x_ref[...].astype(BF16)

    xb = xb_ref[...]
    g_a = _dot(xb, wga_ref[...])
    g_c = _dot(xb, wgc_ref[...])
    a = _dot(at_ref[...], wa_ref[...])
    c = _dot(cb_ref[...], wco_ref[...])
    o_ref[...] = (jax.nn.sigmoid(g_a) * a + jax.nn.sigmoid(g_c) * c).astype(BF16)


def _gate_call(x, attn, cb, w_ga, w_gc, w_a, w_co, tm):
    t, d = x.shape
    tn = _tile(d, 4 * LANES)
    row = lambda i, n: (i, 0)
    col = lambda i, n: (0, n)
    return pl.pallas_call(
        _gate_body,
        grid=(t // tm, d // tn),
        in_specs=[
            pl.BlockSpec((tm, d), row),
            pl.BlockSpec((tm, attn.shape[1]), row),
            pl.BlockSpec((tm, cb.shape[1]), row),
            pl.BlockSpec((d, tn), col),
            pl.BlockSpec((d, tn), col),
            pl.BlockSpec((w_a.shape[0], tn), col),
            pl.BlockSpec((w_co.shape[0], tn), col),
        ],
        out_specs=pl.BlockSpec((tm, tn), lambda i, n: (i, n)),
        out_shape=jax.ShapeDtypeStruct((t, d), BF16),
        scratch_shapes=[pltpu.VMEM((tm, d), BF16)],
        compiler_params=_params("parallel", "arbitrary"),
        name="gate",
    )(x, attn, cb, w_ga, w_gc, w_a, w_co)


def _post_body(x_ref, mp_ref, wo_ref, g_ref, b_ref, wr_ref, wsg_ref, wsu_ref, wsd_ref, *rest, alpha):
    hq_ref, base_ref, lg_ref = rest[-3:]
    mixed = _dot(mp_ref[...], wo_ref[...])
    h = _layer_norm(alpha * x_ref[...] + mixed, g_ref[...], b_ref[...])
    hb = h.astype(BF16)
    lg_ref[...] = _dot_nt(wr_ref[...], hb)
    act = jax.nn.silu(_dot(hb, wsg_ref[...])) * _dot(hb, wsu_ref[...])
    base_ref[...] = alpha * h + _dot(act.astype(BF16), wsd_ref[...])
    _store_packed_rows(hq_ref, (), hb, hb.shape[0])


def _post_call(x, mp, w_o, ln_g, ln_b, w_r_t, w_sg, w_su, w_sd, tm, alpha, hq_tokens, hq_buf, hq_offset):
    t, d = x.shape
    n_e, f = w_r_t.shape[0], w_sg.shape[1]
    ns = d // (2 * LANES)
    off = hq_offset // tm
    row = lambda i: (i, 0)
    full = lambda i: (0, 0)
    in_specs = [
        pl.BlockSpec((tm, d), row),
        pl.BlockSpec((tm, d), row),
        pl.BlockSpec((d, d), full),
        pl.BlockSpec((1, d), full),
        pl.BlockSpec((1, d), full),
        pl.BlockSpec((n_e, d), full),
        pl.BlockSpec((d, f), full),
        pl.BlockSpec((d, f), full),
        pl.BlockSpec((f, d), full),
    ]
    args = [x, mp, w_o, ln_g, ln_b, w_r_t, w_sg, w_su, w_sd]
    aliases = {}
    if hq_buf is not None:
        aliases = {len(args): 0}
        in_specs.append(pl.BlockSpec(memory_space=pl.ANY))
        args.append(hq_buf)
    return pl.pallas_call(
        functools.partial(_post_body, alpha=alpha),
        grid=(t // tm,),
        in_specs=in_specs,
        out_specs=[
            pl.BlockSpec((tm * ns, LANES), lambda i: (i + off, 0)),
            pl.BlockSpec((tm, d), row),
            pl.BlockSpec((n_e, tm), lambda i: (0, i)),
        ],
        out_shape=[
            jax.ShapeDtypeStruct((hq_tokens * ns, LANES), I32),
            jax.ShapeDtypeStruct((t, d), F32),
            jax.ShapeDtypeStruct((n_e, t), F32),
        ],
        input_output_aliases=aliases,
        compiler_params=_params("parallel"),
        name="post",
    )(*args)


def _route_body(bias_ref, lg_ref, idx_ref, w_ref, cnt_ref, *, n_tok):
    per_group = N_EXPERTS // N_GROUPS
    neg_inf = jnp.float32(-jnp.inf)
    i = pl.program_id(0)
    scores = [jax.nn.sigmoid(lg_ref[e]) for e in range(N_EXPERTS)]
    choice = [scores[e] + bias_ref[e] for e in range(N_EXPERTS)]

    group_score = []
    for g in range(N_GROUPS):
        vals = choice[g * per_group:(g + 1) * per_group]
        m1 = functools.reduce(jnp.maximum, vals)
        m2 = jnp.full_like(m1, neg_inf)
        found = jnp.zeros(m1.shape, jnp.bool_)
        for v in vals:
            eq = v == m1
            m2 = jnp.maximum(m2, jnp.where(eq & ~found, neg_inf, v))
            found = found | eq
        group_score.append(m1 + m2)

    masked = []
    for g in range(N_GROUPS):
        ahead = jnp.zeros(group_score[g].shape, I32)
        for o in range(N_GROUPS):
            if o == g:
                continue
            beats = group_score[o] > group_score[g]
            if o < g:
                beats = beats | (group_score[o] == group_score[g])
            ahead = ahead + beats.astype(I32)
        keep = ahead < TOPK_GROUPS
        masked += [jnp.where(keep, choice[e], neg_inf) for e in range(g * per_group, (g + 1) * per_group)]

    shape = masked[0].shape
    token = (i * SUBLANES + lax.broadcasted_iota(I32, shape, 0)) * LANES + lax.broadcasted_iota(I32, shape, 1)
    real = (token < n_tok).astype(I32)
    chosen = [jnp.zeros(shape, I32) for _ in range(N_EXPERTS)]
    picked_w = []
    for r in range(TOP_K):
        best = functools.reduce(jnp.maximum, masked)
        sel = jnp.full(shape, N_EXPERTS, I32)
        for e in reversed(range(N_EXPERTS)):
            sel = jnp.where(masked[e] == best, e, sel)
        w = jnp.zeros(shape, F32)
        for e in range(N_EXPERTS):
            hit = sel == e
            w = jnp.where(hit, scores[e], w)
            masked[e] = jnp.where(hit, neg_inf, masked[e])
            chosen[e] = jnp.where(hit, real, chosen[e])
        idx_ref[r] = sel
        picked_w.append(w)
    total = functools.reduce(lambda a, b: a + b, picked_w)
    for r in range(TOP_K):
        w_ref[r] = picked_w[r] / total * ROUTED_SCALE

    @pl.when(i == 0)
    def _():
        cnt_ref[...] = jnp.zeros(cnt_ref.shape, I32)

    for e in range(N_EXPERTS):
        cnt_ref[e] += chosen[e]


def _route_call(logits_t, bias, n_tok):
    n_e, t = logits_t.shape
    rows = t // LANES
    lg3 = logits_t.reshape(n_e, rows, LANES)
    pick = pl.BlockSpec((TOP_K, SUBLANES, LANES), lambda i: (0, i, 0))
    idx, w, cnt = pl.pallas_call(
        functools.partial(_route_body, n_tok=n_tok),
        grid=(rows // SUBLANES,),
        in_specs=[
            pl.BlockSpec(memory_space=pltpu.SMEM),
            pl.BlockSpec((n_e, SUBLANES, LANES), lambda i: (0, i, 0)),
        ],
        out_specs=[pick, pick, pl.BlockSpec((n_e, SUBLANES, LANES), lambda i: (0, 0, 0))],
        out_shape=[
            jax.ShapeDtypeStruct((TOP_K, rows, LANES), I32),
            jax.ShapeDtypeStruct((TOP_K, rows, LANES), F32),
            jax.ShapeDtypeStruct((n_e, SUBLANES, LANES), I32),
        ],
        compiler_params=_params("arbitrary"),
        name="route",
    )(bias, lg3)
    return idx.reshape(TOP_K, t), w.reshape(TOP_K, t), jnp.sum(cnt, axis=(1, 2))


def _moe_body(be_ref, tok0_ref, tokn_ref, dstp_ref, hq_hbm, wg_ref, wu_ref, wd_ref, out_hbm,
              xb0, xb1, yb0, yb1, wgb, wub, wdb, act_ref, gsem, ssem, *, rows):
    b = pl.program_id(0)
    nb = pl.num_programs(0)
    xbufs, ybufs = (xb0, xb1), (yb0, yb1)
    ns = xb0.shape[0] // rows

    def token(ref, i):
        return ref.at[pl.ds(pl.multiple_of(i * ns, ns), ns), :]

    def gather(idx_ref, dst_buf, sem, r0=0, r1=rows, z=0):
        for r in range(r0, r1):
            pltpu.make_async_copy(token(hq_hbm, idx_ref[0, 0, r] + z), token(dst_buf, r), sem).start(priority=r % 2)

    def scatter(src_buf, sem, r0, r1, z):
        for r in range(r0, r1):
            pltpu.make_async_copy(token(src_buf, r), token(out_hbm, dstp_ref[0, 0, r] + z), sem).start(priority=r % 2)

    def after(v):
        m = jnp.max(v[:SUBLANES, :LANES].astype(F32))
        return jnp.minimum((m != m).astype(I32), 0)

    def gather_done(s):
        return pltpu.make_async_copy(hq_hbm.at[pl.ds(0, rows * ns), :], xbufs[s], gsem.at[s])

    def scatter_done(s):
        return pltpu.make_async_copy(ybufs[s], out_hbm.at[pl.ds(0, rows * ns), :], ssem.at[s])

    @pl.when(b == 0)
    def _():
        gather(tok0_ref, xb0, gsem.at[0])
        yb1[...] = jnp.zeros(yb1.shape, I32)

    @pl.when((b == 0) | (be_ref[b] != be_ref[jnp.maximum(b - 1, 0)]))
    def _():
        wgb[...] = wg_ref[...].astype(BF16)
        wub[...] = wu_ref[...].astype(BF16)
        wdb[...] = wd_ref[...].astype(BF16)

    def block(s):
        o = 1 - s
        d, f = wgb.shape
        f_chunk, d_chunk = min(MOE_F_CHUNK, f), min(MOE_D_CHUNK, d)
        f_chunks = f // f_chunk
        d_chunks = d // d_chunk
        stages = f_chunks + d_chunks
        bounds = [rows * g // stages for g in range(stages + 1)]
        stage = 0
        z = 0

        def copies():
            gather(tokn_ref, xbufs[o], gsem.at[o], bounds[stage], bounds[stage + 1], z)
            scatter(ybufs[o], ssem.at[o], bounds[stage], bounds[stage + 1], z)

        gather_done(s).wait()
        xb = _load_packed_rows(xbufs[s], (), rows, ns).astype(BF16)
        for c in range(f_chunks):
            copies()
            cols = slice(c * f_chunk, (c + 1) * f_chunk)
            act = (jax.nn.silu(_dot(xb, wgb[:, cols])) * _dot(xb, wub[:, cols])).astype(BF16)
            act_ref[:, cols] = act
            z = after(act)
            stage += 1

        @pl.when(b >= 1)
        def _():
            scatter_done(s).wait()

        per_chunk = d_chunk // (2 * LANES)
        for c in range(d_chunks):
            copies()
            y = _dot(act_ref[...], wdb[:, c * d_chunk:(c + 1) * d_chunk]).astype(BF16)
            for k in range(per_chunk):
                lo = y[:, 2 * k * LANES:(2 * k + 1) * LANES]
                hi = y[:, (2 * k + 1) * LANES:(2 * k + 2) * LANES]
                ybufs[s][pl.ds(c * per_chunk + k, rows, stride=ns), :] = _pack_pair(lo, hi)
            z = after(y)
            stage += 1

    for s in range(2):
        @pl.when(b % 2 == s)
        def _(s=s):
            block(s)

    @pl.when(b == nb - 1)
    def _():
        gather_done(0).wait()
        scatter_done(0).wait()


def _moe_call(block_e, tok0, tokn, dstp, hq, w_gate, w_up, w_down, out_rows, rows):
    n_blocks = block_e.shape[0]
    assert n_blocks % 2 == 0
    n_e, d, f = w_gate.shape
    ns = d // (2 * LANES)
    per_block = pl.BlockSpec((1, 1, rows), lambda b, be: (b, 0, 0), memory_space=pltpu.SMEM)
    return pl.pallas_call(
        functools.partial(_moe_body, rows=rows),
        grid_spec=pltpu.PrefetchScalarGridSpec(
            num_scalar_prefetch=1,
            grid=(n_blocks,),
            in_specs=[
                pl.BlockSpec((1, 1, rows), lambda b, be: (0, 0, 0), memory_space=pltpu.SMEM),
                per_block, per_block,
                pl.BlockSpec(memory_space=pl.ANY),
                pl.BlockSpec((None, d, f), lambda b, be: (be[b], 0, 0)),
                pl.BlockSpec((None, d, f), lambda b, be: (be[b], 0, 0)),
                pl.BlockSpec((None, f, d), lambda b, be: (be[b], 0, 0)),
            ],
            out_specs=pl.BlockSpec(memory_space=pl.ANY),
            scratch_shapes=[pltpu.VMEM((rows * ns, LANES), I32)] * 4 + [
                pltpu.VMEM((d, f), BF16),
                pltpu.VMEM((d, f), BF16),
                pltpu.VMEM((f, d), BF16),
                pltpu.VMEM((rows, f), BF16),
                pltpu.SemaphoreType.DMA((2,)),
                pltpu.SemaphoreType.DMA((2,)),
            ],
        ),
        out_shape=jax.ShapeDtypeStruct((out_rows * ns, LANES), I32),
        compiler_params=_params("arbitrary"),
        name="moe",
    )(block_e, tok0, tokn, dstp, hq, w_gate, w_up, w_down)


def _dispatch_plan(idx, counts, n_tok, rows):
    m = n_tok * TOP_K
    n_real = (m + N_EXPERTS * (rows - 1) + rows - 1) // rows
    n_blocks = n_real + 1 + (n_real + 1) % 2
    p = n_blocks * rows
    n_spare = p - m
    plane = -(-(n_tok + -(-n_spare // TOP_K)) // SUBLANES) * SUBLANES
    flat_e = idx.T.reshape(m)
    order = jnp.argsort(flat_e).astype(I32)
    padded = (counts + rows - 1) // rows * rows
    pad_end = jnp.cumsum(padded)
    pad_start = pad_end - padded
    start = jnp.cumsum(counts) - counts
    block_first = jnp.arange(n_blocks, dtype=I32) * rows
    block_e = jnp.minimum(jnp.sum((pad_end[None, :] <= block_first[:, None]).astype(I32), axis=1), N_EXPERTS - 1)
    slot = block_first[:, None] + jnp.arange(rows, dtype=I32)[None, :]
    q = slot - pad_start[block_e][:, None]
    cnt_b = counts[block_e][:, None]
    start_b = start[block_e][:, None]
    real = q < cnt_b
    assign = order[jnp.clip(start_b + q, 0, m - 1)]
    t_of, j_of = assign // TOP_K, assign % TOP_K
    spare_rank = rows + slot - (start_b + jnp.minimum(q, cnt_b))

    def spare_row(rank):
        return (rank % TOP_K) * plane + n_tok + rank // TOP_K

    tok = jnp.where(real, t_of, 0)
    dst = jnp.where(real, j_of * plane + t_of, spare_row(spare_rank))
    first_dst = spare_row(jnp.arange(rows, dtype=I32))[None, :]
    tokn = jnp.concatenate([tok[1:], jnp.zeros((1, rows), I32)], axis=0)
    dstp = jnp.concatenate([first_dst, dst[:-1]], axis=0)
    shape = (n_blocks, 1, rows)
    return block_e.astype(I32), tok[:1].reshape(1, 1, rows), tokn.reshape(shape), dstp.reshape(shape), plane


def _final_body(base_ref, y8_ref, w_ref, g_ref, b_ref, o_ref):
    w = w_ref[...]
    ffn = None
    tm = w.shape[0]
    ns = y8_ref.shape[1] // tm
    for j in range(TOP_K):
        yj = _load_packed_rows(y8_ref, (j,), tm, ns) * w[:, j:j + 1]
        ffn = yj if ffn is None else ffn + yj
    o_ref[...] = _layer_norm(base_ref[...] + ffn, g_ref[...], b_ref[...])


def _final_call(base, y8, w8, ln_g, ln_b, tm, row_offset):
    t, d = base.shape
    off = row_offset // tm
    ns = d // (2 * LANES)
    return pl.pallas_call(
        _final_body,
        grid=(t // tm,),
        in_specs=[
            pl.BlockSpec((tm, d), lambda i: (i, 0)),
            pl.BlockSpec((TOP_K, tm * ns, LANES), lambda i: (0, i + off, 0)),
            pl.BlockSpec((tm, TOP_K), lambda i: (i + off, 0)),
            pl.BlockSpec((1, d), lambda i: (0, 0)),
            pl.BlockSpec((1, d), lambda i: (0, 0)),
        ],
        out_specs=pl.BlockSpec((tm, d), lambda i: (i, 0)),
        out_shape=jax.ShapeDtypeStruct((t, d), F32),
        compiler_params=_params("parallel"),
        name="final",
    )(base, y8, w8, ln_g, ln_b)


def _mixer_and_post(x, pos_tables, w, tm, seq_rows, attend, conv_state, alpha, hq_tokens, hq_buf, hq_offset):
    tm_qkv = 2 * tm if pos_tables[0].shape[0] % (2 * tm) == 0 else tm
    q, k, v = _qkv_call(x, w['qkv'], pos_tables, tm_qkv)
    attn = attend(q, k, v)
    cb, u_tail = _conv_call(x, w['b'], w['c'], w['h'], w['conv'], tm, seq_rows, conv_state)
    mp = _gate_call(x, attn, cb, w['ga'], w['gc'], w['attn_out'], w['conv_out'], tm)
    hq, base, logits_t = _post_call(x, mp, w['o'], w['ln1_g'], w['ln1_b'], w['router_t'],
                                    w['sh_gate'], w['sh_up'], w['sh_down'], min(tm, 2 * LANES), alpha,
                                    hq_tokens, hq_buf, hq_offset)
    return k, v, u_tail, hq, base, logits_t


def kernel(x_prompt, x_sample, cache_k, cache_v, state_conv, w_in, attn_sinks, conv_w, w_attn_out, w_conv_out, w_o, ln1_g, ln1_b, w_router, router_bias, w_exp_gate, w_exp_up, w_exp_down, w_sh_gate, w_sh_up, w_sh_down, ln2_g, ln2_b):
    depth, d, _ = w_in.shape
    batch, seq, _ = x_prompt.shape
    dec_batch, dec_seq, _ = x_sample.shape
    win_buf = cache_k.shape[2]
    cdim = conv_w.shape[2]
    assert dec_seq == 1 and win_buf == WINDOW and seq % WINDOW == 0
    alpha = (2 * depth) ** 0.25
    t_p, t_s = batch * seq, dec_batch * dec_seq
    tm_p, tm_s = _tile(seq, 4 * LANES), _tile(t_s, LANES)
    assert t_p % tm_s == 0 and t_s % tm_s == 0
    tab_p = _rope_tables(jnp.arange(seq))
    tab_s = _rope_tables(jnp.full((tm_s,), PAST_LEN, I32))

    yp = x_prompt.reshape(t_p, d)
    ys = x_sample.reshape(t_s, d)
    p_k, p_v, p_c, s_k, s_v, s_c = [], [], [], [], [], []
    for l in range(depth):
        wl = w_in[l].astype(BF16)
        o = 0
        w = {}
        for name, width in (('qkv', Q_DIM + 2 * KV_DIM), ('b', cdim), ('c', cdim), ('h', cdim), ('ga', d), ('gc', d)):
            w[name] = wl[:, o:o + width]
            o += width
        w.update(
            conv=conv_w[l], attn_out=w_attn_out[l].astype(BF16), conv_out=w_conv_out[l].astype(BF16),
            o=w_o[l].astype(BF16), ln1_g=ln1_g[l][None], ln1_b=ln1_b[l][None],
            router_t=w_router[l].T.astype(BF16), sh_gate=w_sh_gate[l].astype(BF16),
            sh_up=w_sh_up[l].astype(BF16), sh_down=w_sh_down[l].astype(BF16))
        sinks = attn_sinks[l]

        n_tok = t_p + t_s
        k, v, u_tail, hq_p, base_p, lg_p = _mixer_and_post(
            yp, tab_p, w, tm_p, seq,
            lambda q, k, v: _attn_prompt_call(q, k, v, sinks, batch, seq), None, alpha, n_tok, None, 0)
        keep = min(WINDOW, seq)
        p_k.append(k.reshape(batch, seq, N_KV_HEADS, HEAD_DIM)[:, seq - keep:])
        p_v.append(v.reshape(batch, seq, N_KV_HEADS, HEAD_DIM)[:, seq - keep:])
        tails = u_tail.reshape(batch, seq // tm_p, SUBLANES, cdim)
        p_c.append(tails[:, -1, SUBLANES - (CONV_WIDTH - 1):])

        new_kv = {}

        def attend_sample(q, k, v, l=l):
            new_kv['k'] = jnp.concatenate([cache_k[l][:, 1:], k.reshape(t_s, 1, N_KV_HEADS, HEAD_DIM)], axis=1)
            new_kv['v'] = jnp.concatenate([cache_v[l][:, 1:], v.reshape(t_s, 1, N_KV_HEADS, HEAD_DIM)], axis=1)
            return _attn_sample_call(q, new_kv['k'].reshape(t_s, win_buf, KV_DIM),
                                     new_kv['v'].reshape(t_s, win_buf, KV_DIM), sinks)

        state = (state_conv[l][:, 0], state_conv[l][:, 1])
        _, _, u_s, hq, base_s, lg_s = _mixer_and_post(ys, tab_s, w, tm_s, 1, attend_sample, state, alpha,
                                                      n_tok, hq_p, t_p)
        s_k.append(new_kv['k'])
        s_v.append(new_kv['v'])
        s_c.append(jnp.concatenate([state_conv[l][:, 1:], u_s[:, None]], axis=1))

        route_tile = SUBLANES * LANES
        t_pad = -(-n_tok // route_tile) * route_tile
        logits_t = jnp.concatenate([lg_p, lg_s, jnp.zeros((N_EXPERTS, t_pad - n_tok), F32)], axis=1)
        idx, wts, counts = _route_call(logits_t, router_bias[l], n_tok)
        idx, wts = idx[:, :n_tok], wts[:, :n_tok]
        block_e, tok0, tokn, dstp, plane = _dispatch_plan(idx, counts, n_tok, MOE_BLOCK_ROWS)
        y8 = _moe_call(block_e, tok0, tokn, dstp, hq, w_exp_gate[l], w_exp_up[l], w_exp_down[l],
                       TOP_K * plane, MOE_BLOCK_ROWS)
        y8 = y8.reshape(TOP_K, plane * (d // (2 * LANES)), LANES)
        w8 = wts.T
        yp = _final_call(base_p, y8, w8, ln2_g[l][None], ln2_b[l][None], min(tm_p, 2 * LANES), 0)
        ys = _final_call(base_s, y8, w8, ln2_g[l][None], ln2_b[l][None], tm_s, t_p)

    return (yp.reshape(batch, seq, d), ys.reshape(dec_batch, dec_seq, d), jnp.stack(p_k), jnp.stack(p_v),
            jnp.stack(p_c), jnp.stack(s_k), jnp.stack(s_v), jnp.stack(s_c))
```

```python
import functools

import jax
import jax.numpy as jnp
from jax import lax
from jax.experimental import pallas as pl
from jax.experimental.pallas import tpu as pltpu

N_HEADS = 16
N_KV_HEADS = 4
HEAD_DIM = 64
GROUP = N_HEADS // N_KV_HEADS
Q_DIM = N_HEADS * HEAD_DIM
KV_DIM = N_KV_HEADS * HEAD_DIM
ROT_DIM = HEAD_DIM // 4
ROPE_THETA = 500000.0
WINDOW = 128
CONV_WIDTH = 3
PAST_LEN = 16384
N_EXPERTS = 64
N_GROUPS = 8
TOPK_GROUPS = 4
TOP_K = 8
ROUTED_SCALE = 2.5
LN_EPS = 1e-5
MOE_BLOCK_ROWS = 256
MOE_COPY_GROUP = 32

LANES = 128
SUBLANES = 8
VMEM_LIMIT_BYTES = 48 * 1024 * 1024
NEG_BIG = -1e30

F32 = jnp.float32
BF16 = jnp.bfloat16
I32 = jnp.int32


def _tile(n, pref):
    t = pref
    while t > 1 and n % t:
        t //= 2
    return t


def _params(*sem):
    return pltpu.CompilerParams(dimension_semantics=sem, vmem_limit_bytes=VMEM_LIMIT_BYTES)


def _dot(a, b):
    return jnp.dot(a, b, preferred_element_type=F32)


def _dot_nt(a, b):
    return lax.dot_general(a, b, (((1,), (1,)), ((), ())), preferred_element_type=F32)


def _layer_norm(x, g, b):
    mu = jnp.mean(x, axis=-1, keepdims=True)
    xc = x - mu
    var = jnp.mean(xc * xc, axis=-1, keepdims=True)
    return xc * lax.rsqrt(var + LN_EPS) * g + b


def _pack_pair(lo, hi):
    lo32 = lax.bitcast_convert_type(lo.astype(F32), I32)
    hi32 = lax.bitcast_convert_type(hi.astype(F32), I32)
    return lax.shift_right_logical(lo32, 16) | (hi32 & -65536)


def _unpack_pair(p):
    lo = lax.bitcast_convert_type(lax.shift_left(p, 16), F32)
    hi = lax.bitcast_convert_type(p & -65536, F32)
    return lo, hi


def _store_packed_rows(ref, lead, x, n):
    ns = x.shape[1] // (2 * LANES)
    for c in range(ns):
        lo = x[:, 2 * c * LANES:(2 * c + 1) * LANES]
        hi = x[:, (2 * c + 1) * LANES:(2 * c + 2) * LANES]
        ref[lead + (pl.ds(c, n, stride=ns), slice(None))] = _pack_pair(lo, hi)


def _load_packed_rows(ref, lead, n, ns):
    parts = []
    for c in range(ns):
        parts += list(_unpack_pair(ref[lead + (pl.ds(c, n, stride=ns), slice(None))]))
    return jnp.concatenate(parts, axis=1)


def _rope_tables(pos):
    half = ROT_DIM // 2
    n = pos.shape[0]
    inv_freq = jnp.power(jnp.float32(ROPE_THETA), -jnp.arange(half, dtype=F32) * (2.0 / ROT_DIM))
    ang = pos.astype(F32)[:, None] * inv_freq[None, :]
    cos, sin = jnp.cos(ang), jnp.sin(ang)
    rest = HEAD_DIM - ROT_DIM
    cos_h = jnp.concatenate([cos, cos, jnp.ones((n, rest), F32)], axis=1)
    sa_h = jnp.concatenate([jnp.zeros((n, half), F32), sin, jnp.zeros((n, rest), F32)], axis=1)
    sb_h = jnp.concatenate([-sin, jnp.zeros((n, half + rest), F32)], axis=1)
    rep = LANES // HEAD_DIM
    return tuple(jnp.concatenate([t] * rep, axis=1) for t in (cos_h, sa_h, sb_h))


def _qkv_body(x_ref, w_ref, cos_ref, sa_ref, sb_ref, q_ref, k_ref, v_ref, xb_ref, *, nq):
    j = pl.program_id(1)

    @pl.when(j == 0)
    def _():
        xb_ref[...] = x_ref[...].astype(BF16)

    acc = _dot(xb_ref[...], w_ref[...])
    half = ROT_DIM // 2

    def rope(a):
        cos, sa, sb = cos_ref[...], sa_ref[...], sb_ref[...]
        outs = []
        for c in range(a.shape[1] // LANES):
            blk = a[:, c * LANES:(c + 1) * LANES]
            outs.append(blk * cos + pltpu.roll(blk, half, 1) * sa + pltpu.roll(blk, LANES - half, 1) * sb)
        return jnp.concatenate(outs, axis=1)

    @pl.when(j < nq)
    def _():
        q_ref[...] = (rope(acc) * (HEAD_DIM ** -0.5)).astype(BF16)

    @pl.when(j == nq)
    def _():
        k_ref[...] = rope(acc[:, :KV_DIM])
        v_ref[...] = acc[:, KV_DIM:]


def _qkv_call(x, w_qkv, tables, tm):
    t, d = x.shape
    tn = 2 * KV_DIM
    assert Q_DIM % tn == 0
    nq = Q_DIM // tn
    tab_blocks = tables[0].shape[0] // tm
    tab_spec = pl.BlockSpec((tm, LANES), lambda i, j: (i % tab_blocks, 0))
    return pl.pallas_call(
        functools.partial(_qkv_body, nq=nq),
        grid=(t // tm, nq + 1),
        in_specs=[
            pl.BlockSpec((tm, d), lambda i, j: (i, 0)),
            pl.BlockSpec((d, tn), lambda i, j: (0, j)),
            tab_spec, tab_spec, tab_spec,
        ],
        out_specs=[
            pl.BlockSpec((tm, tn), lambda i, j: (i, jnp.minimum(j, nq - 1))),
            pl.BlockSpec((tm, KV_DIM), lambda i, j: (i, 0)),
            pl.BlockSpec((tm, KV_DIM), lambda i, j: (i, 0)),
        ],
        out_shape=[
            jax.ShapeDtypeStruct((t, Q_DIM), BF16),
            jax.ShapeDtypeStruct((t, KV_DIM), F32),
            jax.ShapeDtypeStruct((t, KV_DIM), F32),
        ],
        scratch_shapes=[pltpu.VMEM((tm, d), BF16)],
        compiler_params=_params("parallel", "arbitrary"),
        name="qkv",
    )(x, w_qkv, *tables)


def _head_pair_operands(kv_chunk, odd):
    lane = lax.broadcasted_iota(I32, kv_chunk.shape, 1)
    own = jnp.where((lane >= HEAD_DIM) == odd, kv_chunk, 0.0)
    other = pltpu.roll(own, HEAD_DIM, 1)
    lo, hi = (other, own) if odd else (own, other)
    return lo.astype(BF16), hi.astype(BF16)


def _attend(q, kk, vv, valid, sinks_ref):
    heads = []
    for kh in range(N_KV_HEADS):
        c = (kh * HEAD_DIM) // LANES
        odd = bool((kh * HEAD_DIM) % LANES)
        k_ops = _head_pair_operands(kk[:, c * LANES:(c + 1) * LANES], odd)
        v_ops = _head_pair_operands(vv[:, c * LANES:(c + 1) * LANES], odd)
        for g in range(GROUP):
            h = kh * GROUP + g
            heads.append((h // 2, h, k_ops[h % 2], v_ops[h % 2]))
    scores = [jnp.where(valid, _dot_nt(q[:, ch * LANES:(ch + 1) * LANES], k_op), NEG_BIG)
              for ch, _, k_op, _ in heads]
    maxes = [jnp.maximum(jnp.max(s, axis=-1, keepdims=True), sinks_ref[h]) for s, (_, h, _, _) in zip(scores, heads)]
    probs = [jnp.exp(s - m) for s, m in zip(scores, maxes)]
    dens = [jnp.sum(p, axis=-1, keepdims=True) + jnp.exp(sinks_ref[h] - m)
            for p, m, (_, h, _, _) in zip(probs, maxes, heads)]
    outs = [_dot((p / den).astype(BF16), v_op) for p, den, (_, _, _, v_op) in zip(probs, dens, heads)]
    return jnp.concatenate([outs[2 * j] + outs[2 * j + 1] for j in range(N_HEADS // 2)], axis=1)


def _attn_prompt_body(sinks_ref, q_ref, kp_ref, kc_ref, vp_ref, vc_ref, o_ref):
    n = pl.program_id(1)
    w = WINDOW
    kk = jnp.concatenate([kp_ref[...], kc_ref[...]], axis=0)
    vv = jnp.concatenate([vp_ref[...], vc_ref[...]], axis=0)
    a = lax.broadcasted_iota(I32, (w, 2 * w), 0)
    c = lax.broadcasted_iota(I32, (w, 2 * w), 1)
    valid = (c > a) & (c <= a + w) & ((n > 0) | (c >= w))
    o_ref[...] = _attend(q_ref[...], kk, vv, valid, sinks_ref).astype(BF16)


def _attn_prompt_call(q, k, v, sinks, batch, seq):
    w = WINDOW
    nb = seq // w
    cur = lambda b, n: (b * nb + n, 0)
    prev = lambda b, n: (b * nb + jnp.maximum(n - 1, 0), 0)
    return pl.pallas_call(
        _attn_prompt_body,
        grid=(batch, nb),
        in_specs=[
            pl.BlockSpec(memory_space=pltpu.SMEM),
            pl.BlockSpec((w, Q_DIM), cur),
            pl.BlockSpec((w, KV_DIM), prev),
            pl.BlockSpec((w, KV_DIM), cur),
            pl.BlockSpec((w, KV_DIM), prev),
            pl.BlockSpec((w, KV_DIM), cur),
        ],
        out_specs=pl.BlockSpec((w, Q_DIM), cur),
        out_shape=jax.ShapeDtypeStruct((batch * seq, Q_DIM), BF16),
        compiler_params=_params("parallel", "parallel"),
        name="attn_prompt",
    )(sinks, q, k, k, v, v)


def _attn_sample_body(sinks_ref, q_ref, k_ref, v_ref, o_ref, *, bt, nkeys):
    kk = k_ref[...].reshape(bt * nkeys, KV_DIM)
    vv = v_ref[...].reshape(bt * nkeys, KV_DIM)
    row_b = lax.broadcasted_iota(I32, (bt, bt * nkeys), 0)
    key_b = jnp.concatenate([jnp.full((bt, nkeys), b, I32) for b in range(bt)], axis=1)
    o_ref[...] = _attend(q_ref[...], kk, vv, row_b == key_b, sinks_ref).astype(BF16)


def _attn_sample_call(q, k_win, v_win, sinks):
    b, nkeys, _ = k_win.shape
    bt = _tile(b, SUBLANES)
    return pl.pallas_call(
        functools.partial(_attn_sample_body, bt=bt, nkeys=nkeys),
        grid=(b // bt,),
        in_specs=[
            pl.BlockSpec(memory_space=pltpu.SMEM),
            pl.BlockSpec((bt, Q_DIM), lambda i: (i, 0)),
            pl.BlockSpec((bt, nkeys, KV_DIM), lambda i: (i, 0, 0)),
            pl.BlockSpec((bt, nkeys, KV_DIM), lambda i: (i, 0, 0)),
        ],
        out_specs=pl.BlockSpec((bt, Q_DIM), lambda i: (i, 0)),
        out_shape=jax.ShapeDtypeStruct((b, Q_DIM), BF16),
        compiler_params=_params("parallel"),
        name="attn_sample",
    )(sinks, q, k_win, v_win)


def _conv_body(*refs, decode, tiles_per_seq, tm, tail):
    if decode:
        x_ref, wb_ref, wc_ref, wh_ref, cw_ref, s0_ref, s1_ref, cb_ref, ut_ref, xb_ref = refs
    else:
        x_ref, wb_ref, wc_ref, wh_ref, cw_ref, cb_ref, ut_ref, xb_ref, carry_ref = refs
    i = pl.program_id(0)
    c = pl.program_id(1)

    @pl.when(c == 0)
    def _():
        xb_ref[...] = x_ref[...].astype(BF16)

    xb = xb_ref[...]
    b_g = _dot(xb, wb_ref[...])
    u = _dot(xb, wc_ref[...]) * _dot(xb, wh_ref[...])
    if decode:
        u_m1, u_m2 = s1_ref[...], s0_ref[...]
    else:
        @pl.when(i % tiles_per_seq == 0)
        def _():
            carry_ref[c] = jnp.zeros(carry_ref.shape[1:], F32)

        prev = carry_ref[c]
        p_m2, p_m1 = prev[SUBLANES - 2:SUBLANES - 1], prev[SUBLANES - 1:SUBLANES]
        r = lax.broadcasted_iota(I32, u.shape, 0)
        u_m1 = jnp.where(r == 0, p_m1, pltpu.roll(u, 1, 0))
        u_m2 = jnp.where(r == 0, p_m2, jnp.where(r == 1, p_m1, pltpu.roll(u, 2, 0)))
        carry_ref[c] = u[tm - SUBLANES:]
    cw = cw_ref[...]
    conv = cw[0:1] * u_m2 + cw[1:2] * u_m1 + cw[2:3] * u
    cb_ref[...] = (b_g * conv).astype(BF16)
    ut_ref[...] = u[tm - tail:]


def _conv_call(x, w_b, w_c, w_h, conv_w, tm, seq_rows, state=None):
    t, d = x.shape
    cdim = w_b.shape[1]
    tc = _tile(cdim, 2 * LANES)
    decode = state is not None
    tail = tm if decode else SUBLANES
    w_spec = pl.BlockSpec((d, tc), lambda i, c: (0, c))
    in_specs = [pl.BlockSpec((tm, d), lambda i, c: (i, 0)), w_spec, w_spec, w_spec,
                pl.BlockSpec((CONV_WIDTH, tc), lambda i, c: (0, c))]
    args = [x, w_b, w_c, w_h, conv_w]
    scratch = [pltpu.VMEM((tm, d), BF16)]
    if decode:
        in_specs += [pl.BlockSpec((tm, tc), lambda i, c: (i, c))] * 2
        args += list(state)
    else:
        scratch.append(pltpu.VMEM((cdim // tc, SUBLANES, tc), F32))
    return pl.pallas_call(
        functools.partial(_conv_body, decode=decode, tiles_per_seq=max(seq_rows // tm, 1), tm=tm, tail=tail),
        grid=(t // tm, cdim // tc),
        in_specs=in_specs,
        out_specs=[pl.BlockSpec((tm, tc), lambda i, c: (i, c)), pl.BlockSpec((tail, tc), lambda i, c: (i, c))],
        out_shape=[jax.ShapeDtypeStruct((t, cdim), BF16), jax.ShapeDtypeStruct((t // tm * tail, cdim), F32)],
        scratch_shapes=scratch,
        compiler_params=_params("arbitrary", "arbitrary"),
        name="conv",
    )(*args)


def _gate_body(x_ref, at_ref, cb_ref, wga_ref, wgc_ref, wa_ref, wco_ref, o_ref, xb_ref):
    @pl.when(pl.program_id(1) == 0)
    def _():
        xb_ref[...] = x_ref[...].astype(BF16)

    xb = xb_ref[...]
    g_a = _dot(xb, wga_ref[...])
    g_c = _dot(xb, wgc_ref[...])
    a = _dot(at_ref[...], wa_ref[...])
    c = _dot(cb_ref[...], wco_ref[...])
    o_ref[...] = (jax.nn.sigmoid(g_a) * a + jax.nn.sigmoid(g_c) * c).astype(BF16)


def _gate_call(x, attn, cb, w_ga, w_gc, w_a, w_co, tm):
    t, d = x.shape
    tn = _tile(d, 4 * LANES)
    row = lambda i, n: (i, 0)
    col = lambda i, n: (0, n)
    return pl.pallas_call(
        _gate_body,
        grid=(t // tm, d // tn),
        in_specs=[
            pl.BlockSpec((tm, d), row),
            pl.BlockSpec((tm, attn.shape[1]), row),
            pl.BlockSpec((tm, cb.shape[1]), row),
            pl.BlockSpec((d, tn), col),
            pl.BlockSpec((d, tn), col),
            pl.BlockSpec((w_a.shape[0], tn), col),
            pl.BlockSpec((w_co.shape[0], tn), col),
        ],
        out_specs=pl.BlockSpec((tm, tn), lambda i, n: (i, n)),
        out_shape=jax.ShapeDtypeStruct((t, d), BF16),
        scratch_shapes=[pltpu.VMEM((tm, d), BF16)],
        compiler_params=_params("parallel", "arbitrary"),
        name="gate",
    )(x, attn, cb, w_ga, w_gc, w_a, w_co)


def _post_body(x_ref, mp_ref, wo_ref, g_ref, b_ref, wr_ref, wsg_ref, wsu_ref, wsd_ref, *rest, alpha):
    hq_ref, base_ref, lg_ref = rest[-3:]
    mixed = _dot(mp_ref[...], wo_ref[...])
    h = _layer_norm(alpha * x_ref[...] + mixed, g_ref[...], b_ref[...])
    hb = h.astype(BF16)
    lg_ref[...] = _dot_nt(wr_ref[...], hb)
    act = jax.nn.silu(_dot(hb, wsg_ref[...])) * _dot(hb, wsu_ref[...])
    base_ref[...] = alpha * h + _dot(act.astype(BF16), wsd_ref[...])
    _store_packed_rows(hq_ref, (), hb, hb.shape[0])


def _post_call(x, mp, w_o, ln_g, ln_b, w_r_t, w_sg, w_su, w_sd, tm, alpha, hq_tokens, hq_buf, hq_offset):
    t, d = x.shape
    n_e, f = w_r_t.shape[0], w_sg.shape[1]
    ns = d // (2 * LANES)
    off = hq_offset // tm
    row = lambda i: (i, 0)
    full = lambda i: (0, 0)
    in_specs = [
        pl.BlockSpec((tm, d), row),
        pl.BlockSpec((tm, d), row),
        pl.BlockSpec((d, d), full),
        pl.BlockSpec((1, d), full),
        pl.BlockSpec((1, d), full),
        pl.BlockSpec((n_e, d), full),
        pl.BlockSpec((d, f), full),
        pl.BlockSpec((d, f), full),
        pl.BlockSpec((f, d), full),
    ]
    args = [x, mp, w_o, ln_g, ln_b, w_r_t, w_sg, w_su, w_sd]
    aliases = {}
    if hq_buf is not None:
        aliases = {len(args): 0}
        in_specs.append(pl.BlockSpec(memory_space=pl.ANY))
        args.append(hq_buf)
    return pl.pallas_call(
        functools.partial(_post_body, alpha=alpha),
        grid=(t // tm,),
        in_specs=in_specs,
        out_specs=[
            pl.BlockSpec((tm * ns, LANES), lambda i: (i + off, 0)),
            pl.BlockSpec((tm, d), row),
            pl.BlockSpec((n_e, tm), lambda i: (0, i)),
        ],
        out_shape=[
            jax.ShapeDtypeStruct((hq_tokens * ns, LANES), I32),
            jax.ShapeDtypeStruct((t, d), F32),
            jax.ShapeDtypeStruct((n_e, t), F32),
        ],
        input_output_aliases=aliases,
        compiler_params=_params("parallel"),
        name="post",
    )(*args)


def _route_body(bias_ref, lg_ref, idx_ref, w_ref, cnt_ref, *, n_tok):
    per_group = N_EXPERTS // N_GROUPS
    neg_inf = jnp.float32(-jnp.inf)
    i = pl.program_id(0)
    scores = [jax.nn.sigmoid(lg_ref[e]) for e in range(N_EXPERTS)]
    choice = [scores[e] + bias_ref[e] for e in range(N_EXPERTS)]

    group_score = []
    for g in range(N_GROUPS):
        vals = choice[g * per_group:(g + 1) * per_group]
        m1 = functools.reduce(jnp.maximum, vals)
        m2 = jnp.full_like(m1, neg_inf)
        found = jnp.zeros(m1.shape, jnp.bool_)
        for v in vals:
            eq = v == m1
            m2 = jnp.maximum(m2, jnp.where(eq & ~found, neg_inf, v))
            found = found | eq
        group_score.append(m1 + m2)

    masked = []
    for g in range(N_GROUPS):
        ahead = jnp.zeros(group_score[g].shape, I32)
        for o in range(N_GROUPS):
            if o == g:
                continue
            beats = group_score[o] > group_score[g]
            if o < g:
                beats = beats | (group_score[o] == group_score[g])
            ahead = ahead + beats.astype(I32)
        keep = ahead < TOPK_GROUPS
        masked += [jnp.where(keep, choice[e], neg_inf) for e in range(g * per_group, (g + 1) * per_group)]

    shape = masked[0].shape
    token = (i * SUBLANES + lax.broadcasted_iota(I32, shape, 0)) * LANES + lax.broadcasted_iota(I32, shape, 1)
    real = (token < n_tok).astype(I32)
    chosen = [jnp.zeros(shape, I32) for _ in range(N_EXPERTS)]
    picked_w = []
    for r in range(TOP_K):
        best = functools.reduce(jnp.maximum, masked)
        sel = jnp.full(shape, N_EXPERTS, I32)
        for e in reversed(range(N_EXPERTS)):
            sel = jnp.where(masked[e] == best, e, sel)
        w = jnp.zeros(shape, F32)
        for e in range(N_EXPERTS):
            hit = sel == e
            w = jnp.where(hit, scores[e], w)
            masked[e] = jnp.where(hit, neg_inf, masked[e])
            chosen[e] = jnp.where(hit, real, chosen[e])
        idx_ref[r] = sel
        picked_w.append(w)
    total = functools.reduce(lambda a, b: a + b, picked_w)
    for r in range(TOP_K):
        w_ref[r] = picked_w[r] / total * ROUTED_SCALE

    @pl.when(i == 0)
    def _():
        cnt_ref[...] = jnp.zeros(cnt_ref.shape, I32)

    for e in range(N_EXPERTS):
        cnt_ref[e] += chosen[e]


def _route_call(logits_t, bias, n_tok):
    n_e, t = logits_t.shape
    rows = t // LANES
    lg3 = logits_t.reshape(n_e, rows, LANES)
    pick = pl.BlockSpec((TOP_K, SUBLANES, LANES), lambda i: (0, i, 0))
    idx, w, cnt = pl.pallas_call(
        functools.partial(_route_body, n_tok=n_tok),
        grid=(rows // SUBLANES,),
        in_specs=[
            pl.BlockSpec(memory_space=pltpu.SMEM),
            pl.BlockSpec((n_e, SUBLANES, LANES), lambda i: (0, i, 0)),
        ],
        out_specs=[pick, pick, pl.BlockSpec((n_e, SUBLANES, LANES), lambda i: (0, 0, 0))],
        out_shape=[
            jax.ShapeDtypeStruct((TOP_K, rows, LANES), I32),
            jax.ShapeDtypeStruct((TOP_K, rows, LANES), F32),
            jax.ShapeDtypeStruct((n_e, SUBLANES, LANES), I32),
        ],
        compiler_params=_params("arbitrary"),
        name="route",
    )(bias, lg3)
    return idx.reshape(TOP_K, t), w.reshape(TOP_K, t), jnp.sum(cnt, axis=(1, 2))


def _moe_body(be_ref, nreal_ref, tok0_ref, tokn_ref, dstp_ref, hq_hbm, wg_ref, wu_ref, wd_ref, out_hbm,
              xb0, xb1, yb0, yb1, wgb, wub, wdb, gsem, ssem, *, rows):
    b = pl.program_id(0)
    nb = pl.num_programs(0)
    xbufs, ybufs = (xb0, xb1), (yb0, yb1)
    ns = xb0.shape[0] // rows
    grp = min(MOE_COPY_GROUP, rows)

    def n_real(k):
        return jnp.where((k >= 0) & (k < nb), nreal_ref[jnp.clip(k, 0, nb - 1)], 0)

    def token(ref, i):
        return ref.at[pl.ds(pl.multiple_of(i * ns, ns), ns), :]

    def start_groups(make_copy, n):
        for r in range(rows):
            @pl.when(n > (r // grp) * grp)
            def _(r=r):
                make_copy(r).start(priority=r % 2)

    def wait_groups(src, dst, sem, n):
        for g in range(rows // grp):
            @pl.when(n > g * grp)
            def _(g=g):
                part = pl.ds(g * grp * ns, grp * ns)
                pltpu.make_async_copy(src.at[part, :], dst.at[part, :], sem).wait()

    def gather(idx_ref, dst_buf, sem, n):
        start_groups(lambda r: pltpu.make_async_copy(token(hq_hbm, idx_ref[0, 0, r]), token(dst_buf, r), sem), n)

    @pl.when(b == 0)
    def _():
        xb0[...] = jnp.zeros(xb0.shape, I32)
        xb1[...] = jnp.zeros(xb1.shape, I32)
        gather(tok0_ref, xb0, gsem.at[0], n_real(0))

    @pl.when((b == 0) | (be_ref[b] != be_ref[jnp.maximum(b - 1, 0)]))
    def _():
        wgb[...] = wg_ref[...].astype(BF16)
        wub[...] = wu_ref[...].astype(BF16)
        wdb[...] = wd_ref[...].astype(BF16)

    def block(s):
        o = 1 - s
        wait_groups(hq_hbm, xbufs[s], gsem.at[s], n_real(b))
        gather(tokn_ref, xbufs[o], gsem.at[o], n_real(b + 1))
        start_groups(lambda r: pltpu.make_async_copy(token(ybufs[o], r), token(out_hbm, dstp_ref[0, 0, r]),
                                                     ssem.at[o]), n_real(b - 1))
        xb = _load_packed_rows(xbufs[s], (), rows, ns).astype(BF16)
        act = jax.nn.silu(_dot(xb, wgb[...])) * _dot(xb, wub[...])
        y = _dot(act.astype(BF16), wdb[...]).astype(BF16)
        wait_groups(ybufs[s], out_hbm, ssem.at[s], n_real(b - 2))
        _store_packed_rows(ybufs[s], (), y, rows)

    for s in range(2):
        @pl.when(b % 2 == s)
        def _(s=s):
            block(s)

    @pl.when(b == nb - 1)
    def _():
        wait_groups(yb0, out_hbm, ssem.at[0], n_real(b - 1))


def _moe_call(block_e, n_real, tok0, tokn, dstp, hq, w_gate, w_up, w_down, out_rows, rows):
    n_blocks = block_e.shape[0]
    assert n_blocks % 2 == 0 and rows % min(MOE_COPY_GROUP, rows) == 0
    n_e, d, f = w_gate.shape
    ns = d // (2 * LANES)
    per_block = pl.BlockSpec((1, 1, rows), lambda b, be, nr: (b, 0, 0), memory_space=pltpu.SMEM)
    expert = lambda b, be, nr: (be[b], 0, 0)
    return pl.pallas_call(
        functools.partial(_moe_body, rows=rows),
        grid_spec=pltpu.PrefetchScalarGridSpec(
            num_scalar_prefetch=2,
            grid=(n_blocks,),
            in_specs=[
                pl.BlockSpec((1, 1, rows), lambda b, be, nr: (0, 0, 0), memory_space=pltpu.SMEM),
                per_block, per_block,
                pl.BlockSpec(memory_space=pl.ANY),
                pl.BlockSpec((None, d, f), expert),
                pl.BlockSpec((None, d, f), expert),
                pl.BlockSpec((None, f, d), expert),
            ],
            out_specs=pl.BlockSpec(memory_space=pl.ANY),
            scratch_shapes=[pltpu.VMEM((rows * ns, LANES), I32)] * 4 + [
                pltpu.VMEM((d, f), BF16),
                pltpu.VMEM((d, f), BF16),
                pltpu.VMEM((f, d), BF16),
                pltpu.SemaphoreType.DMA((2,)),
                pltpu.SemaphoreType.DMA((2,)),
            ],
        ),
        out_shape=jax.ShapeDtypeStruct((out_rows * ns, LANES), I32),
        compiler_params=_params("arbitrary"),
        name="moe",
    )(block_e, n_real, tok0, tokn, dstp, hq, w_gate, w_up, w_down)


def _dispatch_plan(idx, counts, n_tok, rows):
    m = n_tok * TOP_K
    n_real = (m + N_EXPERTS * (rows - 1) + rows - 1) // rows
    n_blocks = n_real + 1 + (n_real + 1) % 2
    p = n_blocks * rows
    n_spare = p - m
    plane = -(-(n_tok + -(-n_spare // TOP_K)) // SUBLANES) * SUBLANES
    flat_e = idx.T.reshape(m)
    order = jnp.argsort(flat_e).astype(I32)
    padded = (counts + rows - 1) // rows * rows
    pad_end = jnp.cumsum(padded)
    pad_start = pad_end - padded
    start = jnp.cumsum(counts) - counts
    block_first = jnp.arange(n_blocks, dtype=I32) * rows
    block_e = jnp.minimum(jnp.sum((pad_end[None, :] <= block_first[:, None]).astype(I32), axis=1), N_EXPERTS - 1)
    slot = block_first[:, None] + jnp.arange(rows, dtype=I32)[None, :]
    q = slot - pad_start[block_e][:, None]
    cnt_b = counts[block_e][:, None]
    start_b = start[block_e][:, None]
    real = q < cnt_b
    assign = order[jnp.clip(start_b + q, 0, m - 1)]
    t_of, j_of = assign // TOP_K, assign % TOP_K
    spare_rank = slot - (start_b + jnp.minimum(q, cnt_b))
    tok = jnp.where(real, t_of, 0)
    dst = jnp.where(real, j_of * plane + t_of, (spare_rank % TOP_K) * plane + n_tok + spare_rank // TOP_K)
    n_real_rows = jnp.clip(cnt_b[:, 0] - q[:, 0], 0, rows).astype(I32)
    tokn = jnp.concatenate([tok[1:], jnp.zeros((1, rows), I32)], axis=0)
    dstp = jnp.concatenate([jnp.zeros((1, rows), I32), dst[:-1]], axis=0)
    shape = (n_blocks, 1, rows)
    return (block_e.astype(I32), n_real_rows, tok[:1].reshape(1, 1, rows), tokn.reshape(shape),
            dstp.reshape(shape), plane)


def _final_body(base_ref, y8_ref, w_ref, g_ref, b_ref, o_ref):
    w = w_ref[...]
    ffn = None
    tm = w.shape[0]
    ns = y8_ref.shape[1] // tm
    for j in range(TOP_K):
        yj = _load_packed_rows(y8_ref, (j,), tm, ns) * w[:, j:j + 1]
        ffn = yj if ffn is None else ffn + yj
    o_ref[...] = _layer_norm(base_ref[...] + ffn, g_ref[...], b_ref[...])


def _final_call(base, y8, w8, ln_g, ln_b, tm, row_offset):
    t, d = base.shape
    off = row_offset // tm
    ns = d // (2 * LANES)
    return pl.pallas_call(
        _final_body,
        grid=(t // tm,),
        in_specs=[
            pl.BlockSpec((tm, d), lambda i: (i, 0)),
            pl.BlockSpec((TOP_K, tm * ns, LANES), lambda i: (0, i + off, 0)),
            pl.BlockSpec((tm, TOP_K), lambda i: (i + off, 0)),
            pl.BlockSpec((1, d), lambda i: (0, 0)),
            pl.BlockSpec((1, d), lambda i: (0, 0)),
        ],
        out_specs=pl.BlockSpec((tm, d), lambda i: (i, 0)),
        out_shape=jax.ShapeDtypeStruct((t, d), F32),
        compiler_params=_params("parallel"),
        name="final",
    )(base, y8, w8, ln_g, ln_b)


def _mixer_and_post(x, pos_tables, w, tm, seq_rows, attend, conv_state, alpha, hq_tokens, hq_buf, hq_offset):
    tm_qkv = 2 * tm if pos_tables[0].shape[0] % (2 * tm) == 0 else tm
    q, k, v = _qkv_call(x, w['qkv'], pos_tables, tm_qkv)
    attn = attend(q, k, v)
    cb, u_tail = _conv_call(x, w['b'], w['c'], w['h'], w['conv'], tm, seq_rows, conv_state)
    mp = _gate_call(x, attn, cb, w['ga'], w['gc'], w['attn_out'], w['conv_out'], tm)
    hq, base, logits_t = _post_call(x, mp, w['o'], w['ln1_g'], w['ln1_b'], w['router_t'],
                                    w['sh_gate'], w['sh_up'], w['sh_down'], min(tm, 2 * LANES), alpha,
                                    hq_tokens, hq_buf, hq_offset)
    return k, v, u_tail, hq, base, logits_t


def kernel(x_prompt, x_sample, cache_k, cache_v, state_conv, w_in, attn_sinks, conv_w, w_attn_out, w_conv_out, w_o, ln1_g, ln1_b, w_router, router_bias, w_exp_gate, w_exp_up, w_exp_down, w_sh_gate, w_sh_up, w_sh_down, ln2_g, ln2_b):
    depth, d, _ = w_in.shape
    batch, seq, _ = x_prompt.shape
    dec_batch, dec_seq, _ = x_sample.shape
    win_buf = cache_k.shape[2]
    cdim = conv_w.shape[2]
    assert dec_seq == 1 and win_buf == WINDOW and seq % WINDOW == 0
    alpha = (2 * depth) ** 0.25
    t_p, t_s = batch * seq, dec_batch * dec_seq
    tm_p, tm_s = _tile(seq, 4 * LANES), _tile(t_s, LANES)
    assert t_p % tm_s == 0 and t_s % tm_s == 0
    tab_p = _rope_tables(jnp.arange(seq))
    tab_s = _rope_tables(jnp.full((tm_s,), PAST_LEN, I32))

    yp = x_prompt.reshape(t_p, d)
    ys = x_sample.reshape(t_s, d)
    p_k, p_v, p_c, s_k, s_v, s_c = [], [], [], [], [], []
    for l in range(depth):
        wl = w_in[l].astype(BF16)
        o = 0
        w = {}
        for name, width in (('qkv', Q_DIM + 2 * KV_DIM), ('b', cdim), ('c', cdim), ('h', cdim), ('ga', d), ('gc', d)):
            w[name] = wl[:, o:o + width]
            o += width
        w.update(
            conv=conv_w[l], attn_out=w_attn_out[l].astype(BF16), conv_out=w_conv_out[l].astype(BF16),
            o=w_o[l].astype(BF16), ln1_g=ln1_g[l][None], ln1_b=ln1_b[l][None],
            router_t=w_router[l].T.astype(BF16), sh_gate=w_sh_gate[l].astype(BF16),
            sh_up=w_sh_up[l].astype(BF16), sh_down=w_sh_down[l].astype(BF16))
        sinks = attn_sinks[l]

        n_tok = t_p + t_s
        k, v, u_tail, hq_p, base_p, lg_p = _mixer_and_post(
            yp, tab_p, w, tm_p, seq,
            lambda q, k, v: _attn_prompt_call(q, k, v, sinks, batch, seq), None, alpha, n_tok, None, 0)
        keep = min(WINDOW, seq)
        p_k.append(k.reshape(batch, seq, N_KV_HEADS, HEAD_DIM)[:, seq - keep:])
        p_v.append(v.reshape(batch, seq, N_KV_HEADS, HEAD_DIM)[:, seq - keep:])
        tails = u_tail.reshape(batch, seq // tm_p, SUBLANES, cdim)
        p_c.append(tails[:, -1, SUBLANES - (CONV_WIDTH - 1):])

        new_kv = {}

        def attend_sample(q, k, v, l=l):
            new_kv['k'] = jnp.concatenate([cache_k[l][:, 1:], k.reshape(t_s, 1, N_KV_HEADS, HEAD_DIM)], axis=1)
            new_kv['v'] = jnp.concatenate([cache_v[l][:, 1:], v.reshape(t_s, 1, N_KV_HEADS, HEAD_DIM)], axis=1)
            return _attn_sample_call(q, new_kv['k'].reshape(t_s, win_buf, KV_DIM),
                                     new_kv['v'].reshape(t_s, win_buf, KV_DIM), sinks)

        state = (state_conv[l][:, 0], state_conv[l][:, 1])
        _, _, u_s, hq, base_s, lg_s = _mixer_and_post(ys, tab_s, w, tm_s, 1, attend_sample, state, alpha,
                                                      n_tok, hq_p, t_p)
        s_k.append(new_kv['k'])
        s_v.append(new_kv['v'])
        s_c.append(jnp.concatenate([state_conv[l][:, 1:], u_s[:, None]], axis=1))

        route_tile = SUBLANES * LANES
        t_pad = -(-n_tok // route_tile) * route_tile
        logits_t = jnp.concatenate([lg_p, lg_s, jnp.zeros((N_EXPERTS, t_pad - n_tok), F32)], axis=1)
        idx, wts, counts = _route_call(logits_t, router_bias[l], n_tok)
        idx, wts = idx[:, :n_tok], wts[:, :n_tok]
        block_e, n_real, tok0, tokn, dstp, plane = _dispatch_plan(idx, counts, n_tok, MOE_BLOCK_ROWS)
        y8 = _moe_call(block_e, n_real, tok0, tokn, dstp, hq, w_exp_gate[l], w_exp_up[l], w_exp_down[l],
                       TOP_K * plane, MOE_BLOCK_ROWS)
        y8 = y8.reshape(TOP_K, plane * (d // (2 * LANES)), LANES)
        w8 = wts.T
        yp = _final_call(base_p, y8, w8, ln2_g[l][None], ln2_b[l][None], min(tm_p, 2 * LANES), 0)
        ys = _final_call(base_s, y8, w8, ln2_g[l][None], ln2_b[l][None], tm_s, t_p)

    return (yp.reshape(batch, seq, d), ys.reshape(dec_batch, dec_seq, d), jnp.stack(p_k), jnp.stack(p_v),
            jnp.stack(p_c), jnp.stack(s_k), jnp.stack(s_v), jnp.stack(s_c))
```

```python
import functools

import jax
import jax.numpy as jnp
from jax import lax
from jax.experimental import pallas as pl
from jax.experimental.pallas import tpu as pltpu

N_HEADS = 16
N_KV_HEADS = 4
HEAD_DIM = 64
GROUP = N_HEADS // N_KV_HEADS
Q_DIM = N_HEADS * HEAD_DIM
KV_DIM = N_KV_HEADS * HEAD_DIM
ROT_DIM = HEAD_DIM // 4
ROPE_THETA = 500000.0
WINDOW = 128
CONV_WIDTH = 3
PAST_LEN = 16384
N_EXPERTS = 64
N_GROUPS = 8
TOPK_GROUPS = 4
TOP_K = 8
ROUTED_SCALE = 2.5
LN_EPS = 1e-5
MOE_BLOCK_ROWS = 256
MOE_COPY_GROUP = 32

LANES = 128
SUBLANES = 8
VMEM_LIMIT_BYTES = 48 * 1024 * 1024
NEG_BIG = -1e30

F32 = jnp.float32
BF16 = jnp.bfloat16
I32 = jnp.int32


def _tile(n, pref):
    t = pref
    while t > 1 and n % t:
        t //= 2
    return t


def _params(*sem):
    return pltpu.CompilerParams(dimension_semantics=sem, vmem_limit_bytes=VMEM_LIMIT_BYTES)


def _dot(a, b):
    return jnp.dot(a, b, preferred_element_type=F32)


def _dot_nt(a, b):
    return lax.dot_general(a, b, (((1,), (1,)), ((), ())), preferred_element_type=F32)


def _layer_norm(x, g, b):
    mu = jnp.mean(x, axis=-1, keepdims=True)
    xc = x - mu
    var = jnp.mean(xc * xc, axis=-1, keepdims=True)
    return xc * lax.rsqrt(var + LN_EPS) * g + b


def _pack_pair(lo, hi):
    lo32 = lax.bitcast_convert_type(lo.astype(F32), I32)
    hi32 = lax.bitcast_convert_type(hi.astype(F32), I32)
    return lax.shift_right_logical(lo32, 16) | (hi32 & -65536)


def _unpack_pair(p):
    lo = lax.bitcast_convert_type(lax.shift_left(p, 16), F32)
    hi = lax.bitcast_convert_type(p & -65536, F32)
    return lo, hi


def _store_packed_rows(ref, lead, x, n, first_row=0):
    ns = x.shape[1] // (2 * LANES)
    for c in range(ns):
        lo = x[:, 2 * c * LANES:(2 * c + 1) * LANES]
        hi = x[:, (2 * c + 1) * LANES:(2 * c + 2) * LANES]
        ref[lead + (pl.ds(first_row + c, n, stride=ns), slice(None))] = _pack_pair(lo, hi)


def _load_packed_rows(ref, lead, n, ns):
    parts = []
    for c in range(ns):
        parts += list(_unpack_pair(ref[lead + (pl.ds(c, n, stride=ns), slice(None))]))
    return jnp.concatenate(parts, axis=1)


def _rope_tables(pos):
    half = ROT_DIM // 2
    n = pos.shape[0]
    inv_freq = jnp.power(jnp.float32(ROPE_THETA), -jnp.arange(half, dtype=F32) * (2.0 / ROT_DIM))
    ang = pos.astype(F32)[:, None] * inv_freq[None, :]
    cos, sin = jnp.cos(ang), jnp.sin(ang)
    rest = HEAD_DIM - ROT_DIM
    cos_h = jnp.concatenate([cos, cos, jnp.ones((n, rest), F32)], axis=1)
    sa_h = jnp.concatenate([jnp.zeros((n, half), F32), sin, jnp.zeros((n, rest), F32)], axis=1)
    sb_h = jnp.concatenate([-sin, jnp.zeros((n, half + rest), F32)], axis=1)
    rep = LANES // HEAD_DIM
    return tuple(jnp.concatenate([t] * rep, axis=1) for t in (cos_h, sa_h, sb_h))


def _qkv_body(x_ref, w_ref, cos_ref, sa_ref, sb_ref, q_ref, k_ref, v_ref, xb_ref, *, nq):
    j = pl.program_id(1)

    @pl.when(j == 0)
    def _():
        xb_ref[...] = x_ref[...].astype(BF16)

    acc = _dot(xb_ref[...], w_ref[...])
    half = ROT_DIM // 2

    def rope(a):
        cos, sa, sb = cos_ref[...], sa_ref[...], sb_ref[...]
        outs = []
        for c in range(a.shape[1] // LANES):
            blk = a[:, c * LANES:(c + 1) * LANES]
            outs.append(blk * cos + pltpu.roll(blk, half, 1) * sa + pltpu.roll(blk, LANES - half, 1) * sb)
        return jnp.concatenate(outs, axis=1)

    @pl.when(j < nq)
    def _():
        q_ref[...] = (rope(acc) * (HEAD_DIM ** -0.5)).astype(BF16)

    @pl.when(j == nq)
    def _():
        k_ref[...] = rope(acc[:, :KV_DIM])
        v_ref[...] = acc[:, KV_DIM:]


def _qkv_call(x, w_qkv, tables, tm):
    t, d = x.shape
    tn = 2 * KV_DIM
    assert Q_DIM % tn == 0
    nq = Q_DIM // tn
    tab_blocks = tables[0].shape[0] // tm
    tab_spec = pl.BlockSpec((tm, LANES), lambda i, j: (i % tab_blocks, 0))
    return pl.pallas_call(
        functools.partial(_qkv_body, nq=nq),
        grid=(t // tm, nq + 1),
        in_specs=[
            pl.BlockSpec((tm, d), lambda i, j: (i, 0)),
            pl.BlockSpec((d, tn), lambda i, j: (0, j)),
            tab_spec, tab_spec, tab_spec,
        ],
        out_specs=[
            pl.BlockSpec((tm, tn), lambda i, j: (i, jnp.minimum(j, nq - 1))),
            pl.BlockSpec((tm, KV_DIM), lambda i, j: (i, 0)),
            pl.BlockSpec((tm, KV_DIM), lambda i, j: (i, 0)),
        ],
        out_shape=[
            jax.ShapeDtypeStruct((t, Q_DIM), BF16),
            jax.ShapeDtypeStruct((t, KV_DIM), F32),
            jax.ShapeDtypeStruct((t, KV_DIM), F32),
        ],
        scratch_shapes=[pltpu.VMEM((tm, d), BF16)],
        compiler_params=_params("parallel", "arbitrary"),
        name="qkv",
    )(x, w_qkv, *tables)


def _head_pair_operands(kv_chunk, odd):
    lane = lax.broadcasted_iota(I32, kv_chunk.shape, 1)
    own = jnp.where((lane >= HEAD_DIM) == odd, kv_chunk, 0.0)
    other = pltpu.roll(own, HEAD_DIM, 1)
    lo, hi = (other, own) if odd else (own, other)
    return lo.astype(BF16), hi.astype(BF16)


def _attend(q, kk, vv, valid, sinks_ref):
    heads = []
    for kh in range(N_KV_HEADS):
        c = (kh * HEAD_DIM) // LANES
        odd = bool((kh * HEAD_DIM) % LANES)
        k_ops = _head_pair_operands(kk[:, c * LANES:(c + 1) * LANES], odd)
        v_ops = _head_pair_operands(vv[:, c * LANES:(c + 1) * LANES], odd)
        for g in range(GROUP):
            h = kh * GROUP + g
            heads.append((h // 2, h, k_ops[h % 2], v_ops[h % 2]))
    scores = [jnp.where(valid, _dot_nt(q[:, ch * LANES:(ch + 1) * LANES], k_op), NEG_BIG)
              for ch, _, k_op, _ in heads]
    maxes = [jnp.maximum(jnp.max(s, axis=-1, keepdims=True), sinks_ref[h]) for s, (_, h, _, _) in zip(scores, heads)]
    probs = [jnp.exp(s - m) for s, m in zip(scores, maxes)]
    dens = [jnp.sum(p, axis=-1, keepdims=True) + jnp.exp(sinks_ref[h] - m)
            for p, m, (_, h, _, _) in zip(probs, maxes, heads)]
    outs = [_dot((p / den).astype(BF16), v_op) for p, den, (_, _, _, v_op) in zip(probs, dens, heads)]
    return jnp.concatenate([outs[2 * j] + outs[2 * j + 1] for j in range(N_HEADS // 2)], axis=1)


def _attn_prompt_body(sinks_ref, q_ref, kp_ref, kc_ref, vp_ref, vc_ref, o_ref):
    n = pl.program_id(1)
    w = WINDOW
    kk = jnp.concatenate([kp_ref[...], kc_ref[...]], axis=0)
    vv = jnp.concatenate([vp_ref[...], vc_ref[...]], axis=0)
    a = lax.broadcasted_iota(I32, (w, 2 * w), 0)
    c = lax.broadcasted_iota(I32, (w, 2 * w), 1)
    valid = (c > a) & (c <= a + w) & ((n > 0) | (c >= w))
    o_ref[...] = _attend(q_ref[...], kk, vv, valid, sinks_ref).astype(BF16)


def _attn_prompt_call(q, k, v, sinks, batch, seq):
    w = WINDOW
    nb = seq // w
    cur = lambda b, n: (b * nb + n, 0)
    prev = lambda b, n: (b * nb + jnp.maximum(n - 1, 0), 0)
    return pl.pallas_call(
        _attn_prompt_body,
        grid=(batch, nb),
        in_specs=[
            pl.BlockSpec(memory_space=pltpu.SMEM),
            pl.BlockSpec((w, Q_DIM), cur),
            pl.BlockSpec((w, KV_DIM), prev),
            pl.BlockSpec((w, KV_DIM), cur),
            pl.BlockSpec((w, KV_DIM), prev),
            pl.BlockSpec((w, KV_DIM), cur),
        ],
        out_specs=pl.BlockSpec((w, Q_DIM), cur),
        out_shape=jax.ShapeDtypeStruct((batch * seq, Q_DIM), BF16),
        compiler_params=_params("parallel", "parallel"),
        name="attn_prompt",
    )(sinks, q, k, k, v, v)


def _attn_sample_body(sinks_ref, q_ref, k_ref, v_ref, o_ref, *, bt, nkeys):
    kk = k_ref[...].reshape(bt * nkeys, KV_DIM)
    vv = v_ref[...].reshape(bt * nkeys, KV_DIM)
    row_b = lax.broadcasted_iota(I32, (bt, bt * nkeys), 0)
    key_b = jnp.concatenate([jnp.full((bt, nkeys), b, I32) for b in range(bt)], axis=1)
    o_ref[...] = _attend(q_ref[...], kk, vv, row_b == key_b, sinks_ref).astype(BF16)


def _attn_sample_call(q, k_win, v_win, sinks):
    b, nkeys, _ = k_win.shape
    bt = _tile(b, SUBLANES)
    return pl.pallas_call(
        functools.partial(_attn_sample_body, bt=bt, nkeys=nkeys),
        grid=(b // bt,),
        in_specs=[
            pl.BlockSpec(memory_space=pltpu.SMEM),
            pl.BlockSpec((bt, Q_DIM), lambda i: (i, 0)),
            pl.BlockSpec((bt, nkeys, KV_DIM), lambda i: (i, 0, 0)),
            pl.BlockSpec((bt, nkeys, KV_DIM), lambda i: (i, 0, 0)),
        ],
        out_specs=pl.BlockSpec((bt, Q_DIM), lambda i: (i, 0)),
        out_shape=jax.ShapeDtypeStruct((b, Q_DIM), BF16),
        compiler_params=_params("parallel"),
        name="attn_sample",
    )(sinks, q, k_win, v_win)


def _conv_body(*refs, decode, tiles_per_seq, tm, tail):
    if decode:
        x_ref, wb_ref, wc_ref, wh_ref, cw_ref, s0_ref, s1_ref, cb_ref, ut_ref, xb_ref = refs
    else:
        x_ref, wb_ref, wc_ref, wh_ref, cw_ref, cb_ref, ut_ref, xb_ref, carry_ref = refs
    i = pl.program_id(0)
    c = pl.program_id(1)

    @pl.when(c == 0)
    def _():
        xb_ref[...] = x_ref[...].astype(BF16)

    xb = xb_ref[...]
    b_g = _dot(xb, wb_ref[...])
    u = _dot(xb, wc_ref[...]) * _dot(xb, wh_ref[...])
    if decode:
        u_m1, u_m2 = s1_ref[...], s0_ref[...]
    else:
        @pl.when(i % tiles_per_seq == 0)
        def _():
            carry_ref[c] = jnp.zeros(carry_ref.shape[1:], F32)

        prev = carry_ref[c]
        p_m2, p_m1 = prev[SUBLANES - 2:SUBLANES - 1], prev[SUBLANES - 1:SUBLANES]
        r = lax.broadcasted_iota(I32, u.shape, 0)
        u_m1 = jnp.where(r == 0, p_m1, pltpu.roll(u, 1, 0))
        u_m2 = jnp.where(r == 0, p_m2, jnp.where(r == 1, p_m1, pltpu.roll(u, 2, 0)))
        carry_ref[c] = u[tm - SUBLANES:]
    cw = cw_ref[...]
    conv = cw[0:1] * u_m2 + cw[1:2] * u_m1 + cw[2:3] * u
    cb_ref[...] = (b_g * conv).astype(BF16)
    ut_ref[...] = u[tm - tail:]


def _conv_call(x, w_b, w_c, w_h, conv_w, tm, seq_rows, state=None):
    t, d = x.shape
    cdim = w_b.shape[1]
    tc = _tile(cdim, 2 * LANES)
    decode = state is not None
    tail = tm if decode else SUBLANES
    w_spec = pl.BlockSpec((d, tc), lambda i, c: (0, c))
    in_specs = [pl.BlockSpec((tm, d), lambda i, c: (i, 0)), w_spec, w_spec, w_spec,
                pl.BlockSpec((CONV_WIDTH, tc), lambda i, c: (0, c))]
    args = [x, w_b, w_c, w_h, conv_w]
    scratch = [pltpu.VMEM((tm, d), BF16)]
    if decode:
        in_specs += [pl.BlockSpec((tm, tc), lambda i, c: (i, c))] * 2
        args += list(state)
    else:
        scratch.append(pltpu.VMEM((cdim // tc, SUBLANES, tc), F32))
    return pl.pallas_call(
        functools.partial(_conv_body, decode=decode, tiles_per_seq=max(seq_rows // tm, 1), tm=tm, tail=tail),
        grid=(t // tm, cdim // tc),
        in_specs=in_specs,
        out_specs=[pl.BlockSpec((tm, tc), lambda i, c: (i, c)), pl.BlockSpec((tail, tc), lambda i, c: (i, c))],
        out_shape=[jax.ShapeDtypeStruct((t, cdim), BF16), jax.ShapeDtypeStruct((t // tm * tail, cdim), F32)],
        scratch_shapes=scratch,
        compiler_params=_params("arbitrary", "arbitrary"),
        name="conv",
    )(*args)


def _gate_body(x_ref, at_ref, cb_ref, wga_ref, wgc_ref, wa_ref, wco_ref, o_ref, xb_ref):
    @pl.when(pl.program_id(1) == 0)
    def _():
        xb_ref[...] = x_ref[...].astype(BF16)

    xb = xb_ref[...]
    g_a = _dot(xb, wga_ref[...])
    g_c = _dot(xb, wgc_ref[...])
    a = _dot(at_ref[...], wa_ref[...])
    c = _dot(cb_ref[...], wco_ref[...])
    o_ref[...] = (jax.nn.sigmoid(g_a) * a + jax.nn.sigmoid(g_c) * c).astype(BF16)


def _gate_call(x, attn, cb, w_ga, w_gc, w_a, w_co, tm):
    t, d = x.shape
    tn = _tile(d, 4 * LANES)
    row = lambda i, n: (i, 0)
    col = lambda i, n: (0, n)
    return pl.pallas_call(
        _gate_body,
        grid=(t // tm, d // tn),
        in_specs=[
            pl.BlockSpec((tm, d), row),
            pl.BlockSpec((tm, attn.shape[1]), row),
            pl.BlockSpec((tm, cb.shape[1]), row),
            pl.BlockSpec((d, tn), col),
            pl.BlockSpec((d, tn), col),
            pl.BlockSpec((w_a.shape[0], tn), col),
            pl.BlockSpec((w_co.shape[0], tn), col),
        ],
        out_specs=pl.BlockSpec((tm, tn), lambda i, n: (i, n)),
        out_shape=jax.ShapeDtypeStruct((t, d), BF16),
        scratch_shapes=[pltpu.VMEM((tm, d), BF16)],
        compiler_params=_params("parallel", "arbitrary"),
        name="gate",
    )(x, attn, cb, w_ga, w_gc, w_a, w_co)


def _post_body(x_ref, mp_ref, wo_ref, g_ref, b_ref, wr_ref, wsg_ref, wsu_ref, wsd_ref, *rest, alpha):
    hq_ref, base_ref, lg_ref = rest[-3:]
    tm = x_ref.shape[0]
    halves = 2 if tm % (2 * LANES) == 0 else 1
    hr = tm // halves
    ns = x_ref.shape[1] // (2 * LANES)
    parts = [slice(i * hr, (i + 1) * hr) for i in range(halves)]
    mixed = [_dot(mp_ref[p, :], wo_ref[...]) for p in parts]
    hs = [_layer_norm(alpha * x_ref[p, :] + m, g_ref[...], b_ref[...]) for p, m in zip(parts, mixed)]
    hbs = [h.astype(BF16) for h in hs]
    for p, hb in zip(parts, hbs):
        lg_ref[:, p] = _dot_nt(wr_ref[...], hb)
    gates = [_dot(hb, wsg_ref[...]) for hb in hbs]
    ups = [_dot(hb, wsu_ref[...]) for hb in hbs]
    acts = [(jax.nn.silu(g) * u).astype(BF16) for g, u in zip(gates, ups)]
    for p, h, a in zip(parts, hs, acts):
        base_ref[p, :] = alpha * h + _dot(a, wsd_ref[...])
    for i, hb in enumerate(hbs):
        _store_packed_rows(hq_ref, (), hb, hr, first_row=i * hr * ns)


def _post_call(x, mp, w_o, ln_g, ln_b, w_r_t, w_sg, w_su, w_sd, tm, alpha, hq_tokens, hq_buf, hq_offset):
    t, d = x.shape
    n_e, f = w_r_t.shape[0], w_sg.shape[1]
    ns = d // (2 * LANES)
    off = hq_offset // tm
    row = lambda i: (i, 0)
    whole = lambda shape: pl.BlockSpec(shape, lambda i: (0, 0), pipeline_mode=pl.Buffered(1))
    in_specs = [
        pl.BlockSpec((tm, d), row),
        pl.BlockSpec((tm, d), row),
        whole((d, d)), whole((1, d)), whole((1, d)), whole((n_e, d)), whole((d, f)), whole((d, f)), whole((f, d)),
    ]
    args = [x, mp, w_o, ln_g, ln_b, w_r_t, w_sg, w_su, w_sd]
    aliases = {}
    if hq_buf is not None:
        aliases = {len(args): 0}
        in_specs.append(pl.BlockSpec(memory_space=pl.ANY))
        args.append(hq_buf)
    return pl.pallas_call(
        functools.partial(_post_body, alpha=alpha),
        grid=(t // tm,),
        in_specs=in_specs,
        out_specs=[
            pl.BlockSpec((tm * ns, LANES), lambda i: (i + off, 0)),
            pl.BlockSpec((tm, d), row),
            pl.BlockSpec((n_e, tm), lambda i: (0, i)),
        ],
        out_shape=[
            jax.ShapeDtypeStruct((hq_tokens * ns, LANES), I32),
            jax.ShapeDtypeStruct((t, d), F32),
            jax.ShapeDtypeStruct((n_e, t), F32),
        ],
        input_output_aliases=aliases,
        compiler_params=_params("parallel"),
        name="post",
    )(*args)


def _route_body(bias_ref, lg_ref, idx_ref, w_ref, cnt_ref, *, n_tok):
    per_group = N_EXPERTS // N_GROUPS
    neg_inf = jnp.float32(-jnp.inf)
    i = pl.program_id(0)
    scores = [jax.nn.sigmoid(lg_ref[e]) for e in range(N_EXPERTS)]
    choice = [scores[e] + bias_ref[e] for e in range(N_EXPERTS)]

    group_score = []
    for g in range(N_GROUPS):
        vals = choice[g * per_group:(g + 1) * per_group]
        m1 = functools.reduce(jnp.maximum, vals)
        m2 = jnp.full_like(m1, neg_inf)
        found = jnp.zeros(m1.shape, jnp.bool_)
        for v in vals:
            eq = v == m1
            m2 = jnp.maximum(m2, jnp.where(eq & ~found, neg_inf, v))
            found = found | eq
        group_score.append(m1 + m2)

    masked = []
    for g in range(N_GROUPS):
        ahead = jnp.zeros(group_score[g].shape, I32)
        for o in range(N_GROUPS):
            if o == g:
                continue
            beats = group_score[o] > group_score[g]
            if o < g:
                beats = beats | (group_score[o] == group_score[g])
            ahead = ahead + beats.astype(I32)
        keep = ahead < TOPK_GROUPS
        masked += [jnp.where(keep, choice[e], neg_inf) for e in range(g * per_group, (g + 1) * per_group)]

    shape = masked[0].shape
    token = (i * SUBLANES + lax.broadcasted_iota(I32, shape, 0)) * LANES + lax.broadcasted_iota(I32, shape, 1)
    real = (token < n_tok).astype(I32)
    chosen = [jnp.zeros(shape, I32) for _ in range(N_EXPERTS)]
    picked_w = []
    for r in range(TOP_K):
        best = functools.reduce(jnp.maximum, masked)
        sel = jnp.full(shape, N_EXPERTS, I32)
        for e in reversed(range(N_EXPERTS)):
            sel = jnp.where(masked[e] == best, e, sel)
        w = jnp.zeros(shape, F32)
        for e in range(N_EXPERTS):
            hit = sel == e
            w = jnp.where(hit, scores[e], w)
            masked[e] = jnp.where(hit, neg_inf, masked[e])
            chosen[e] = jnp.where(hit, real, chosen[e])
        idx_ref[r] = sel
        picked_w.append(w)
    total = functools.reduce(lambda a, b: a + b, picked_w)
    for r in range(TOP_K):
        w_ref[r] = picked_w[r] / total * ROUTED_SCALE

    @pl.when(i == 0)
    def _():
        cnt_ref[...] = jnp.zeros(cnt_ref.shape, I32)

    for e in range(N_EXPERTS):
        cnt_ref[e] += chosen[e]


def _route_call(logits_t, bias, n_tok):
    n_e, t = logits_t.shape
    rows = t // LANES
    lg3 = logits_t.reshape(n_e, rows, LANES)
    pick = pl.BlockSpec((TOP_K, SUBLANES, LANES), lambda i: (0, i, 0))
    idx, w, cnt = pl.pallas_call(
        functools.partial(_route_body, n_tok=n_tok),
        grid=(rows // SUBLANES,),
        in_specs=[
            pl.BlockSpec(memory_space=pltpu.SMEM),
            pl.BlockSpec((n_e, SUBLANES, LANES), lambda i: (0, i, 0)),
        ],
        out_specs=[pick, pick, pl.BlockSpec((n_e, SUBLANES, LANES), lambda i: (0, 0, 0))],
        out_shape=[
            jax.ShapeDtypeStruct((TOP_K, rows, LANES), I32),
            jax.ShapeDtypeStruct((TOP_K, rows, LANES), F32),
            jax.ShapeDtypeStruct((n_e, SUBLANES, LANES), I32),
        ],
        compiler_params=_params("arbitrary"),
        name="route",
    )(bias, lg3)
    return idx.reshape(TOP_K, t), w.reshape(TOP_K, t), jnp.sum(cnt, axis=(1, 2))


def _moe_body(be_ref, nreal_ref, tok0_ref, tokn_ref, dstp_ref, hq_hbm, wg_ref, wu_ref, wd_ref, out_hbm,
              xb0, xb1, yb0, yb1, wgb, wub, wdb, gsem, ssem, *, rows):
    b = pl.program_id(0)
    nb = pl.num_programs(0)
    xbufs, ybufs = (xb0, xb1), (yb0, yb1)
    ns = xb0.shape[0] // rows
    grp = min(MOE_COPY_GROUP, rows)

    def n_real(k):
        return jnp.where((k >= 0) & (k < nb), nreal_ref[jnp.clip(k, 0, nb - 1)], 0)

    def token(ref, i):
        return ref.at[pl.ds(pl.multiple_of(i * ns, ns), ns), :]

    def start_groups(make_copy, n):
        for r in range(rows):
            @pl.when(n > (r // grp) * grp)
            def _(r=r):
                make_copy(r).start(priority=r % 2)

    def wait_groups(src, dst, sem, n):
        for g in range(rows // grp):
            @pl.when(n > g * grp)
            def _(g=g):
                part = pl.ds(g * grp * ns, grp * ns)
                pltpu.make_async_copy(src.at[part, :], dst.at[part, :], sem).wait()

    def gather(idx_ref, dst_buf, sem, n):
        start_groups(lambda r: pltpu.make_async_copy(token(hq_hbm, idx_ref[0, 0, r]), token(dst_buf, r), sem), n)

    @pl.when(b == 0)
    def _():
        xb0[...] = jnp.zeros(xb0.shape, I32)
        xb1[...] = jnp.zeros(xb1.shape, I32)
        gather(tok0_ref, xb0, gsem.at[0], n_real(0))

    @pl.when((b == 0) | (be_ref[b] != be_ref[jnp.maximum(b - 1, 0)]))
    def _():
        wgb[...] = wg_ref[...].astype(BF16)
        wub[...] = wu_ref[...].astype(BF16)
        wdb[...] = wd_ref[...].astype(BF16)

    def block(s):
        o = 1 - s
        wait_groups(hq_hbm, xbufs[s], gsem.at[s], n_real(b))
        gather(tokn_ref, xbufs[o], gsem.at[o], n_real(b + 1))
        start_groups(lambda r: pltpu.make_async_copy(token(ybufs[o], r), token(out_hbm, dstp_ref[0, 0, r]),
                                                     ssem.at[o]), n_real(b - 1))
        xb = _load_packed_rows(xbufs[s], (), rows, ns).astype(BF16)
        act = jax.nn.silu(_dot(xb, wgb[...])) * _dot(xb, wub[...])
        y = _dot(act.astype(BF16), wdb[...]).astype(BF16)
        wait_groups(ybufs[s], out_hbm, ssem.at[s], n_real(b - 2))
        _store_packed_rows(ybufs[s], (), y, rows)

    for s in range(2):
        @pl.when(b % 2 == s)
        def _(s=s):
            block(s)

    @pl.when(b == nb - 1)
    def _():
        wait_groups(yb0, out_hbm, ssem.at[0], n_real(b - 1))


def _moe_call(block_e, n_real, tok0, tokn, dstp, hq, w_gate, w_up, w_down, out_rows, rows):
    n_blocks = block_e.shape[0]
    assert n_blocks % 2 == 0 and rows % min(MOE_COPY_GROUP, rows) == 0
    n_e, d, f = w_gate.shape
    ns = d // (2 * LANES)
    per_block = pl.BlockSpec((1, 1, rows), lambda b, be, nr: (b, 0, 0), memory_space=pltpu.SMEM)
    expert = lambda b, be, nr: (be[b], 0, 0)
    return pl.pallas_call(
        functools.partial(_moe_body, rows=rows),
        grid_spec=pltpu.PrefetchScalarGridSpec(
            num_scalar_prefetch=2,
            grid=(n_blocks,),
            in_specs=[
                pl.BlockSpec((1, 1, rows), lambda b, be, nr: (0, 0, 0), memory_space=pltpu.SMEM),
                per_block, per_block,
                pl.BlockSpec(memory_space=pl.ANY),
                pl.BlockSpec((None, d, f), expert),
                pl.BlockSpec((None, d, f), expert),
                pl.BlockSpec((None, f, d), expert),
            ],
            out_specs=pl.BlockSpec(memory_space=pl.ANY),
            scratch_shapes=[pltpu.VMEM((rows * ns, LANES), I32)] * 4 + [
                pltpu.VMEM((d, f), BF16),
                pltpu.VMEM((d, f), BF16),
                pltpu.VMEM((f, d), BF16),
                pltpu.SemaphoreType.DMA((2,)),
                pltpu.SemaphoreType.DMA((2,)),
            ],
        ),
        out_shape=jax.ShapeDtypeStruct((out_rows * ns, LANES), I32),
        compiler_params=_params("arbitrary"),
        name="moe",
    )(block_e, n_real, tok0, tokn, dstp, hq, w_gate, w_up, w_down)


def _dispatch_plan(idx, counts, n_tok, rows):
    m = n_tok * TOP_K
    n_real = (m + N_EXPERTS * (rows - 1) + rows - 1) // rows
    n_blocks = n_real + 1 + (n_real + 1) % 2
    p = n_blocks * rows
    n_spare = p - m
    plane = -(-(n_tok + -(-n_spare // TOP_K)) // SUBLANES) * SUBLANES
    flat_e = idx.T.reshape(m)
    bits = max(1, (m - 1).bit_length())
    assert (N_EXPERTS << bits) < 2 ** 31
    order = jnp.sort((flat_e << bits) | jnp.arange(m, dtype=I32)) & ((1 << bits) - 1)
    padded = (counts + rows - 1) // rows * rows
    pad_end = jnp.cumsum(padded)
    pad_start = pad_end - padded
    start = jnp.cumsum(counts) - counts
    block_first = jnp.arange(n_blocks, dtype=I32) * rows
    block_e = jnp.minimum(jnp.sum((pad_end[None, :] <= block_first[:, None]).astype(I32), axis=1), N_EXPERTS - 1)
    slot = block_first[:, None] + jnp.arange(rows, dtype=I32)[None, :]
    q = slot - pad_start[block_e][:, None]
    cnt_b = counts[block_e][:, None]
    start_b = start[block_e][:, None]
    real = q < cnt_b
    assign = order[jnp.clip(start_b + q, 0, m - 1)]
    t_of, j_of = assign // TOP_K, assign % TOP_K
    spare_rank = slot - (start_b + jnp.minimum(q, cnt_b))
    tok = jnp.where(real, t_of, 0)
    dst = jnp.where(real, j_of * plane + t_of, (spare_rank % TOP_K) * plane + n_tok + spare_rank // TOP_K)
    n_real_rows = jnp.clip(cnt_b[:, 0] - q[:, 0], 0, rows).astype(I32)
    tokn = jnp.concatenate([tok[1:], jnp.zeros((1, rows), I32)], axis=0)
    dstp = jnp.concatenate([jnp.zeros((1, rows), I32), dst[:-1]], axis=0)
    shape = (n_blocks, 1, rows)
    return (block_e.astype(I32), n_real_rows, tok[:1].reshape(1, 1, rows), tokn.reshape(shape),
            dstp.reshape(shape), plane)


def _final_body(base_ref, y8_ref, w_ref, g_ref, b_ref, o_ref):
    w = w_ref[...]
    ffn = None
    tm = w.shape[0]
    ns = y8_ref.shape[1] // tm
    for j in range(TOP_K):
        yj = _load_packed_rows(y8_ref, (j,), tm, ns) * w[:, j:j + 1]
        ffn = yj if ffn is None else ffn + yj
    o_ref[...] = _layer_norm(base_ref[...] + ffn, g_ref[...], b_ref[...])


def _final_call(base, y8, w8, ln_g, ln_b, tm, row_offset):
    t, d = base.shape
    off = row_offset // tm
    ns = d // (2 * LANES)
    return pl.pallas_call(
        _final_body,
        grid=(t // tm,),
        in_specs=[
            pl.BlockSpec((tm, d), lambda i: (i, 0)),
            pl.BlockSpec((TOP_K, tm * ns, LANES), lambda i: (0, i + off, 0)),
            pl.BlockSpec((tm, TOP_K), lambda i: (i + off, 0)),
            pl.BlockSpec((1, d), lambda i: (0, 0)),
            pl.BlockSpec((1, d), lambda i: (0, 0)),
        ],
        out_specs=pl.BlockSpec((tm, d), lambda i: (i, 0)),
        out_shape=jax.ShapeDtypeStruct((t, d), F32),
        compiler_params=_params("parallel"),
        name="final",
    )(base, y8, w8, ln_g, ln_b)


def _mixer_and_post(x, pos_tables, w, tm, seq_rows, attend, conv_state, alpha, hq_tokens, hq_buf, hq_offset):
    tm_qkv = 2 * tm if pos_tables[0].shape[0] % (2 * tm) == 0 else tm
    q, k, v = _qkv_call(x, w['qkv'], pos_tables, tm_qkv)
    attn = attend(q, k, v)
    cb, u_tail = _conv_call(x, w['b'], w['c'], w['h'], w['conv'], tm, seq_rows, conv_state)
    mp = _gate_call(x, attn, cb, w['ga'], w['gc'], w['attn_out'], w['conv_out'], tm)
    hq, base, logits_t = _post_call(x, mp, w['o'], w['ln1_g'], w['ln1_b'], w['router_t'],
                                    w['sh_gate'], w['sh_up'], w['sh_down'], tm, alpha,
                                    hq_tokens, hq_buf, hq_offset)
    return k, v, u_tail, hq, base, logits_t


def kernel(x_prompt, x_sample, cache_k, cache_v, state_conv, w_in, attn_sinks, conv_w, w_attn_out, w_conv_out, w_o, ln1_g, ln1_b, w_router, router_bias, w_exp_gate, w_exp_up, w_exp_down, w_sh_gate, w_sh_up, w_sh_down, ln2_g, ln2_b):
    depth, d, _ = w_in.shape
    batch, seq, _ = x_prompt.shape
    dec_batch, dec_seq, _ = x_sample.shape
    win_buf = cache_k.shape[2]
    cdim = conv_w.shape[2]
    assert dec_seq == 1 and win_buf == WINDOW and seq % WINDOW == 0
    alpha = (2 * depth) ** 0.25
    t_p, t_s = batch * seq, dec_batch * dec_seq
    tm_p, tm_s = _tile(seq, 4 * LANES), _tile(t_s, LANES)
    assert t_p % tm_s == 0 and t_s % tm_s == 0
    tab_p = _rope_tables(jnp.arange(seq))
    tab_s = _rope_tables(jnp.full((tm_s,), PAST_LEN, I32))

    yp = x_prompt.reshape(t_p, d)
    ys = x_sample.reshape(t_s, d)
    p_k, p_v, p_c, s_k, s_v, s_c = [], [], [], [], [], []
    for l in range(depth):
        wl = w_in[l].astype(BF16)
        o = 0
        w = {}
        for name, width in (('qkv', Q_DIM + 2 * KV_DIM), ('b', cdim), ('c', cdim), ('h', cdim), ('ga', d), ('gc', d)):
            w[name] = wl[:, o:o + width]
            o += width
        w.update(
            conv=conv_w[l], attn_out=w_attn_out[l].astype(BF16), conv_out=w_conv_out[l].astype(BF16),
            o=w_o[l].astype(BF16), ln1_g=ln1_g[l][None], ln1_b=ln1_b[l][None],
            router_t=w_router[l].T.astype(BF16), sh_gate=w_sh_gate[l].astype(BF16),
            sh_up=w_sh_up[l].astype(BF16), sh_down=w_sh_down[l].astype(BF16))
        sinks = attn_sinks[l]

        n_tok = t_p + t_s
        k, v, u_tail, hq_p, base_p, lg_p = _mixer_and_post(
            yp, tab_p, w, tm_p, seq,
            lambda q, k, v: _attn_prompt_call(q, k, v, sinks, batch, seq), None, alpha, n_tok, None, 0)
        keep = min(WINDOW, seq)
        for kv, acc in ((k, p_k), (v, p_v)):
            tail_rows = kv.reshape(batch, seq, KV_DIM)[:, seq - keep:]
            acc.append(tail_rows.reshape(batch, keep, N_KV_HEADS, HEAD_DIM))
        tails = u_tail.reshape(batch, seq // tm_p, SUBLANES, cdim)
        p_c.append(tails[:, -1, SUBLANES - (CONV_WIDTH - 1):])

        new_kv = {}

        def attend_sample(q, k, v, l=l):
            new_kv['k'] = jnp.concatenate([cache_k[l][:, 1:], k.reshape(t_s, 1, N_KV_HEADS, HEAD_DIM)], axis=1)
            new_kv['v'] = jnp.concatenate([cache_v[l][:, 1:], v.reshape(t_s, 1, N_KV_HEADS, HEAD_DIM)], axis=1)
            return _attn_sample_call(q, new_kv['k'].reshape(t_s, win_buf, KV_DIM),
                                     new_kv['v'].reshape(t_s, win_buf, KV_DIM), sinks)

        state = (state_conv[l][:, 0], state_conv[l][:, 1])
        _, _, u_s, hq, base_s, lg_s = _mixer_and_post(ys, tab_s, w, tm_s, 1, attend_sample, state, alpha,
                                                      n_tok, hq_p, t_p)
        s_k.append(new_kv['k'])
        s_v.append(new_kv['v'])
        s_c.append(jnp.concatenate([state_conv[l][:, 1:], u_s[:, None]], axis=1))

        route_tile = SUBLANES * LANES
        t_pad = -(-n_tok // route_tile) * route_tile
        logits_t = jnp.concatenate([lg_p, lg_s, jnp.zeros((N_EXPERTS, t_pad - n_tok), F32)], axis=1)
        idx, wts, counts = _route_call(logits_t, router_bias[l], n_tok)
        idx, wts = idx[:, :n_tok], wts[:, :n_tok]
        block_e, n_real, tok0, tokn, dstp, plane = _dispatch_plan(idx, counts, n_tok, MOE_BLOCK_ROWS)
        y8 = _moe_call(block_e, n_real, tok0, tokn, dstp, hq, w_exp_gate[l], w_exp_up[l], w_exp_down[l],
                       TOP_K * plane, MOE_BLOCK_ROWS)
        y8 = y8.reshape(TOP_K, plane * (d // (2 * LANES)), LANES)
        w8 = wts.T
        yp = _final_call(base_p, y8, w8, ln2_g[l][None], ln2_b[l][None], min(tm_p, 2 * LANES), 0)
        ys = _final_call(base_s, y8, w8, ln2_g[l][None], ln2_b[l][None], tm_s, t_p)

    return (yp.reshape(batch, seq, d), ys.reshape(dec_batch, dec_seq, d), jnp.stack(p_k), jnp.stack(p_v),
            jnp.stack(p_c), jnp.stack(s_k), jnp.stack(s_v), jnp.stack(s_c))
```

```python
import functools

import jax
import jax.numpy as jnp
from jax import lax
from jax.experimental import pallas as pl
from jax.experimental.pallas import tpu as pltpu

N_HEADS = 16
N_KV_HEADS = 4
HEAD_DIM = 64
GROUP = N_HEADS // N_KV_HEADS
Q_DIM = N_HEADS * HEAD_DIM
KV_DIM = N_KV_HEADS * HEAD_DIM
ROT_DIM = HEAD_DIM // 4
ROPE_THETA = 500000.0
WINDOW = 128
CONV_WIDTH = 3
PAST_LEN = 16384
N_EXPERTS = 64
N_GROUPS = 8
TOPK_GROUPS = 4
TOP_K = 8
ROUTED_SCALE = 2.5
LN_EPS = 1e-5
MOE_BLOCK_ROWS = 256
MOE_COPY_GROUP = 32

LANES = 128
SUBLANES = 8
VMEM_LIMIT_BYTES = 48 * 1024 * 1024
NEG_BIG = -1e30

F32 = jnp.float32
BF16 = jnp.bfloat16
I32 = jnp.int32


def _tile(n, pref):
    t = pref
    while t > 1 and n % t:
        t //= 2
    return t


def _params(*sem):
    return pltpu.CompilerParams(dimension_semantics=sem, vmem_limit_bytes=VMEM_LIMIT_BYTES)


def _dot(a, b):
    return jnp.dot(a, b, preferred_element_type=F32)


def _dot_nt(a, b):
    return lax.dot_general(a, b, (((1,), (1,)), ((), ())), preferred_element_type=F32)


def _layer_norm(x, g, b):
    mu = jnp.mean(x, axis=-1, keepdims=True)
    xc = x - mu
    var = jnp.mean(xc * xc, axis=-1, keepdims=True)
    return xc * lax.rsqrt(var + LN_EPS) * g + b


def _pack_pair(lo, hi):
    lo32 = lax.bitcast_convert_type(lo.astype(F32), I32)
    hi32 = lax.bitcast_convert_type(hi.astype(F32), I32)
    return lax.shift_right_logical(lo32, 16) | (hi32 & -65536)


def _unpack_pair(p):
    lo = lax.bitcast_convert_type(lax.shift_left(p, 16), F32)
    hi = lax.bitcast_convert_type(p & -65536, F32)
    return lo, hi


def _store_packed_rows(ref, lead, x, n, first_row=0):
    ns = x.shape[1] // (2 * LANES)
    for c in range(ns):
        lo = x[:, 2 * c * LANES:(2 * c + 1) * LANES]
        hi = x[:, (2 * c + 1) * LANES:(2 * c + 2) * LANES]
        ref[lead + (pl.ds(first_row + c, n, stride=ns), slice(None))] = _pack_pair(lo, hi)


def _load_packed_rows(ref, lead, n, ns):
    parts = []
    for c in range(ns):
        parts += list(_unpack_pair(ref[lead + (pl.ds(c, n, stride=ns), slice(None))]))
    return jnp.concatenate(parts, axis=1)


def _rope_tables(pos):
    half = ROT_DIM // 2
    n = pos.shape[0]
    inv_freq = jnp.power(jnp.float32(ROPE_THETA), -jnp.arange(half, dtype=F32) * (2.0 / ROT_DIM))
    ang = pos.astype(F32)[:, None] * inv_freq[None, :]
    cos, sin = jnp.cos(ang), jnp.sin(ang)
    rest = HEAD_DIM - ROT_DIM
    cos_h = jnp.concatenate([cos, cos, jnp.ones((n, rest), F32)], axis=1)
    sa_h = jnp.concatenate([jnp.zeros((n, half), F32), sin, jnp.zeros((n, rest), F32)], axis=1)
    sb_h = jnp.concatenate([-sin, jnp.zeros((n, half + rest), F32)], axis=1)
    rep = LANES // HEAD_DIM
    return tuple(jnp.concatenate([t] * rep, axis=1) for t in (cos_h, sa_h, sb_h))


def _qkv_body(x_ref, w_ref, cos_ref, sa_ref, sb_ref, q_ref, k_ref, v_ref, xb_ref, *, nq):
    j = pl.program_id(1)

    @pl.when(j == 0)
    def _():
        xb_ref[...] = x_ref[...].astype(BF16)

    acc = _dot(xb_ref[...], w_ref[...])
    half = ROT_DIM // 2
    is_q = j < nq
    scale = jnp.where(is_q, jnp.float32(HEAD_DIM ** -0.5), jnp.float32(1.0))
    cos, sa, sb = cos_ref[...] * scale, sa_ref[...] * scale, sb_ref[...] * scale
    outs = []
    for c in range(acc.shape[1] // LANES):
        blk = acc[:, c * LANES:(c + 1) * LANES]
        roped = blk * cos + pltpu.roll(blk, half, 1) * sa + pltpu.roll(blk, LANES - half, 1) * sb
        outs.append(roped if c * LANES < KV_DIM else jnp.where(is_q, roped, blk))
    out = jnp.concatenate(outs, axis=1)

    @pl.when(is_q)
    def _():
        q_ref[...] = out.astype(BF16)

    @pl.when(j == nq)
    def _():
        k_ref[...] = out[:, :KV_DIM]
        v_ref[...] = out[:, KV_DIM:]


def _qkv_call(x, w_qkv, tables, tm):
    t, d = x.shape
    tn = 2 * KV_DIM
    assert Q_DIM % tn == 0
    nq = Q_DIM // tn
    tab_blocks = tables[0].shape[0] // tm
    tab_spec = pl.BlockSpec((tm, LANES), lambda i, j: (i % tab_blocks, 0))
    return pl.pallas_call(
        functools.partial(_qkv_body, nq=nq),
        grid=(t // tm, nq + 1),
        in_specs=[
            pl.BlockSpec((tm, d), lambda i, j: (i, 0)),
            pl.BlockSpec((d, tn), lambda i, j: (0, j)),
            tab_spec, tab_spec, tab_spec,
        ],
        out_specs=[
            pl.BlockSpec((tm, tn), lambda i, j: (i, jnp.minimum(j, nq - 1))),
            pl.BlockSpec((tm, KV_DIM), lambda i, j: (i, 0)),
            pl.BlockSpec((tm, KV_DIM), lambda i, j: (i, 0)),
        ],
        out_shape=[
            jax.ShapeDtypeStruct((t, Q_DIM), BF16),
            jax.ShapeDtypeStruct((t, KV_DIM), F32),
            jax.ShapeDtypeStruct((t, KV_DIM), F32),
        ],
        scratch_shapes=[pltpu.VMEM((tm, d), BF16)],
        compiler_params=_params("parallel", "arbitrary"),
        name="qkv",
    )(x, w_qkv, *tables)


def _head_pair_operands(kv_chunk, odd):
    lane = lax.broadcasted_iota(I32, kv_chunk.shape, 1)
    own = jnp.where((lane >= HEAD_DIM) == odd, kv_chunk, 0.0)
    other = pltpu.roll(own, HEAD_DIM, 1)
    lo, hi = (other, own) if odd else (own, other)
    return lo.astype(BF16), hi.astype(BF16)


def _attend(q, kk, vv, valid, sinks_ref):
    heads = []
    for kh in range(N_KV_HEADS):
        c = (kh * HEAD_DIM) // LANES
        odd = bool((kh * HEAD_DIM) % LANES)
        k_ops = _head_pair_operands(kk[:, c * LANES:(c + 1) * LANES], odd)
        v_ops = _head_pair_operands(vv[:, c * LANES:(c + 1) * LANES], odd)
        for g in range(GROUP):
            h = kh * GROUP + g
            heads.append((h // 2, h, k_ops[h % 2], v_ops[h % 2]))
    scores = [jnp.where(valid, _dot_nt(q[:, ch * LANES:(ch + 1) * LANES], k_op), NEG_BIG)
              for ch, _, k_op, _ in heads]
    maxes = [jnp.maximum(jnp.max(s, axis=-1, keepdims=True), sinks_ref[h]) for s, (_, h, _, _) in zip(scores, heads)]
    probs = [jnp.exp(s - m) for s, m in zip(scores, maxes)]
    dens = [jnp.sum(p, axis=-1, keepdims=True) + jnp.exp(sinks_ref[h] - m)
            for p, m, (_, h, _, _) in zip(probs, maxes, heads)]
    outs = [_dot((p / den).astype(BF16), v_op) for p, den, (_, _, _, v_op) in zip(probs, dens, heads)]
    return jnp.concatenate([outs[2 * j] + outs[2 * j + 1] for j in range(N_HEADS // 2)], axis=1)


def _attn_prompt_body(sinks_ref, q_ref, kp_ref, kc_ref, vp_ref, vc_ref, o_ref):
    n = pl.program_id(1)
    w = WINDOW
    kk = jnp.concatenate([kp_ref[...], kc_ref[...]], axis=0)
    vv = jnp.concatenate([vp_ref[...], vc_ref[...]], axis=0)
    a = lax.broadcasted_iota(I32, (w, 2 * w), 0)
    c = lax.broadcasted_iota(I32, (w, 2 * w), 1)
    valid = (c > a) & (c <= a + w) & ((n > 0) | (c >= w))
    o_ref[...] = _attend(q_ref[...], kk, vv, valid, sinks_ref).astype(BF16)


def _attn_prompt_call(q, k, v, sinks, batch, seq):
    w = WINDOW
    nb = seq // w
    cur = lambda b, n: (b * nb + n, 0)
    prev = lambda b, n: (b * nb + jnp.maximum(n - 1, 0), 0)
    return pl.pallas_call(
        _attn_prompt_body,
        grid=(batch, nb),
        in_specs=[
            pl.BlockSpec(memory_space=pltpu.SMEM),
            pl.BlockSpec((w, Q_DIM), cur),
            pl.BlockSpec((w, KV_DIM), prev),
            pl.BlockSpec((w, KV_DIM), cur),
            pl.BlockSpec((w, KV_DIM), prev),
            pl.BlockSpec((w, KV_DIM), cur),
        ],
        out_specs=pl.BlockSpec((w, Q_DIM), cur),
        out_shape=jax.ShapeDtypeStruct((batch * seq, Q_DIM), BF16),
        compiler_params=_params("parallel", "parallel"),
        name="attn_prompt",
    )(sinks, q, k, k, v, v)


def _attn_sample_body(sinks_ref, q_ref, k_ref, v_ref, o_ref, *, bt, nkeys):
    kk = k_ref[...].reshape(bt * nkeys, KV_DIM)
    vv = v_ref[...].reshape(bt * nkeys, KV_DIM)
    row_b = lax.broadcasted_iota(I32, (bt, bt * nkeys), 0)
    key_b = jnp.concatenate([jnp.full((bt, nkeys), b, I32) for b in range(bt)], axis=1)
    o_ref[...] = _attend(q_ref[...], kk, vv, row_b == key_b, sinks_ref).astype(BF16)


def _attn_sample_call(q, k_win, v_win, sinks):
    b, nkeys, _ = k_win.shape
    bt = _tile(b, SUBLANES)
    return pl.pallas_call(
        functools.partial(_attn_sample_body, bt=bt, nkeys=nkeys),
        grid=(b // bt,),
        in_specs=[
            pl.BlockSpec(memory_space=pltpu.SMEM),
            pl.BlockSpec((bt, Q_DIM), lambda i: (i, 0)),
            pl.BlockSpec((bt, nkeys, KV_DIM), lambda i: (i, 0, 0)),
            pl.BlockSpec((bt, nkeys, KV_DIM), lambda i: (i, 0, 0)),
        ],
        out_specs=pl.BlockSpec((bt, Q_DIM), lambda i: (i, 0)),
        out_shape=jax.ShapeDtypeStruct((b, Q_DIM), BF16),
        compiler_params=_params("parallel"),
        name="attn_sample",
    )(sinks, q, k_win, v_win)


def _conv_body(*refs, decode, tiles_per_seq, tm, tail):
    if decode:
        x_ref, wb_ref, wc_ref, wh_ref, cw_ref, s0_ref, s1_ref, cb_ref, ut_ref, xb_ref = refs
    else:
        x_ref, wb_ref, wc_ref, wh_ref, cw_ref, cb_ref, ut_ref, xb_ref, carry_ref = refs
    i = pl.program_id(0)
    c = pl.program_id(1)

    @pl.when(c == 0)
    def _():
        xb_ref[...] = x_ref[...].astype(BF16)

    if not decode:
        @pl.when(i % tiles_per_seq == 0)
        def _():
            carry_ref[c] = jnp.zeros(carry_ref.shape[1:], F32)

    xb = xb_ref[...]
    tc = wb_ref.shape[1]
    sub = min(tc, 2 * LANES)
    cols = [slice(k * sub, (k + 1) * sub) for k in range(tc // sub)]
    b_gs = [_dot(xb, wb_ref[:, s]) for s in cols]
    us = [_dot(xb, wc_ref[:, s]) * _dot(xb, wh_ref[:, s]) for s in cols]
    cw = cw_ref[...]
    for s, b_g, u in zip(cols, b_gs, us):
        if decode:
            u_m1, u_m2 = s1_ref[:, s], s0_ref[:, s]
        else:
            prev = carry_ref[c]
            p_m2, p_m1 = prev[SUBLANES - 2:SUBLANES - 1, s], prev[SUBLANES - 1:SUBLANES, s]
            r = lax.broadcasted_iota(I32, u.shape, 0)
            u_m1 = jnp.where(r == 0, p_m1, pltpu.roll(u, 1, 0))
            u_m2 = jnp.where(r == 0, p_m2, jnp.where(r == 1, p_m1, pltpu.roll(u, 2, 0)))
        conv = cw[0:1, s] * u_m2 + cw[1:2, s] * u_m1 + cw[2:3, s] * u
        cb_ref[:, s] = (b_g * conv).astype(BF16)
        ut_ref[:, s] = u[tm - tail:]
    if not decode:
        carry_ref[c] = jnp.concatenate([u[tm - SUBLANES:] for u in us], axis=1)


def _conv_call(x, w_b, w_c, w_h, conv_w, tm, seq_rows, state=None):
    t, d = x.shape
    cdim = w_b.shape[1]
    tc = _tile(cdim, 4 * LANES)
    decode = state is not None
    tail = tm if decode else SUBLANES
    w_spec = pl.BlockSpec((d, tc), lambda i, c: (0, c))
    in_specs = [pl.BlockSpec((tm, d), lambda i, c: (i, 0)), w_spec, w_spec, w_spec,
                pl.BlockSpec((CONV_WIDTH, tc), lambda i, c: (0, c))]
    args = [x, w_b, w_c, w_h, conv_w]
    scratch = [pltpu.VMEM((tm, d), BF16)]
    if decode:
        in_specs += [pl.BlockSpec((tm, tc), lambda i, c: (i, c))] * 2
        args += list(state)
    else:
        scratch.append(pltpu.VMEM((cdim // tc, SUBLANES, tc), F32))
    return pl.pallas_call(
        functools.partial(_conv_body, decode=decode, tiles_per_seq=max(seq_rows // tm, 1), tm=tm, tail=tail),
        grid=(t // tm, cdim // tc),
        in_specs=in_specs,
        out_specs=[pl.BlockSpec((tm, tc), lambda i, c: (i, c)), pl.BlockSpec((tail, tc), lambda i, c: (i, c))],
        out_shape=[jax.ShapeDtypeStruct((t, cdim), BF16), jax.ShapeDtypeStruct((t // tm * tail, cdim), F32)],
        scratch_shapes=scratch,
        compiler_params=_params("arbitrary", "arbitrary"),
        name="conv",
    )(*args)


def _gate_body(x_ref, at_ref, cb_ref, wga_ref, wgc_ref, wa_ref, wco_ref, o_ref, xb_ref):
    @pl.when(pl.program_id(1) == 0)
    def _():
        xb_ref[...] = x_ref[...].astype(BF16)

    xb = xb_ref[...]
    g_a = _dot(xb, wga_ref[...])
    g_c = _dot(xb, wgc_ref[...])
    a = _dot(at_ref[...], wa_ref[...])
    c = _dot(cb_ref[...], wco_ref[...])
    o_ref[...] = (jax.nn.sigmoid(g_a) * a + jax.nn.sigmoid(g_c) * c).astype(BF16)


def _gate_call(x, attn, cb, w_ga, w_gc, w_a, w_co, tm):
    t, d = x.shape
    tn = _tile(d, 4 * LANES)
    row = lambda i, n: (i, 0)
    col = lambda i, n: (0, n)
    return pl.pallas_call(
        _gate_body,
        grid=(t // tm, d // tn),
        in_specs=[
            pl.BlockSpec((tm, d), row),
            pl.BlockSpec((tm, attn.shape[1]), row),
            pl.BlockSpec((tm, cb.shape[1]), row),
            pl.BlockSpec((d, tn), col),
            pl.BlockSpec((d, tn), col),
            pl.BlockSpec((w_a.shape[0], tn), col),
            pl.BlockSpec((w_co.shape[0], tn), col),
        ],
        out_specs=pl.BlockSpec((tm, tn), lambda i, n: (i, n)),
        out_shape=jax.ShapeDtypeStruct((t, d), BF16),
        scratch_shapes=[pltpu.VMEM((tm, d), BF16)],
        compiler_params=_params("parallel", "arbitrary"),
        name="gate",
    )(x, attn, cb, w_ga, w_gc, w_a, w_co)


def _post_body(x_ref, mp_ref, wo_ref, g_ref, b_ref, wr_ref, wsg_ref, wsu_ref, wsd_ref, *rest, alpha):
    hq_ref, base_ref, lg_ref = rest[-3:]
    tm = x_ref.shape[0]
    halves = 2 if tm % (2 * LANES) == 0 else 1
    hr = tm // halves
    ns = x_ref.shape[1] // (2 * LANES)
    parts = [slice(i * hr, (i + 1) * hr) for i in range(halves)]
    mixed = [_dot(mp_ref[p, :], wo_ref[...]) for p in parts]
    hs = [_layer_norm(alpha * x_ref[p, :] + m, g_ref[...], b_ref[...]) for p, m in zip(parts, mixed)]
    hbs = [h.astype(BF16) for h in hs]
    for p, hb in zip(parts, hbs):
        lg_ref[:, p] = _dot_nt(wr_ref[...], hb)
    gates = [_dot(hb, wsg_ref[...]) for hb in hbs]
    ups = [_dot(hb, wsu_ref[...]) for hb in hbs]
    acts = [(jax.nn.silu(g) * u).astype(BF16) for g, u in zip(gates, ups)]
    for p, h, a in zip(parts, hs, acts):
        base_ref[p, :] = alpha * h + _dot(a, wsd_ref[...])
    for i, hb in enumerate(hbs):
        _store_packed_rows(hq_ref, (), hb, hr, first_row=i * hr * ns)


def _post_call(x, mp, w_o, ln_g, ln_b, w_r_t, w_sg, w_su, w_sd, tm, alpha, hq_tokens, hq_buf, hq_offset):
    t, d = x.shape
    n_e, f = w_r_t.shape[0], w_sg.shape[1]
    ns = d // (2 * LANES)
    off = hq_offset // tm
    row = lambda i: (i, 0)
    whole = lambda shape: pl.BlockSpec(shape, lambda i: (0, 0), pipeline_mode=pl.Buffered(1))
    in_specs = [
        pl.BlockSpec((tm, d), row),
        pl.BlockSpec((tm, d), row),
        whole((d, d)), whole((1, d)), whole((1, d)), whole((n_e, d)), whole((d, f)), whole((d, f)), whole((f, d)),
    ]
    args = [x, mp, w_o, ln_g, ln_b, w_r_t, w_sg, w_su, w_sd]
    aliases = {}
    if hq_buf is not None:
        aliases = {len(args): 0}
        in_specs.append(pl.BlockSpec(memory_space=pl.ANY))
        args.append(hq_buf)
    return pl.pallas_call(
        functools.partial(_post_body, alpha=alpha),
        grid=(t // tm,),
        in_specs=in_specs,
        out_specs=[
            pl.BlockSpec((tm * ns, LANES), lambda i: (i + off, 0)),
            pl.BlockSpec((tm, d), row),
            pl.BlockSpec((n_e, tm), lambda i: (0, i)),
        ],
        out_shape=[
            jax.ShapeDtypeStruct((hq_tokens * ns, LANES), I32),
            jax.ShapeDtypeStruct((t, d), F32),
            jax.ShapeDtypeStruct((n_e, t), F32),
        ],
        input_output_aliases=aliases,
        compiler_params=_params("parallel"),
        name="post",
    )(*args)


def _route_body(bias_ref, lg_ref, idx_ref, w_ref, cnt_ref, *, n_tok):
    per_group = N_EXPERTS // N_GROUPS
    neg_inf = jnp.float32(-jnp.inf)
    i = pl.program_id(0)
    scores = [jax.nn.sigmoid(lg_ref[e]) for e in range(N_EXPERTS)]
    choice = [scores[e] + bias_ref[e] for e in range(N_EXPERTS)]

    group_score = []
    for g in range(N_GROUPS):
        vals = choice[g * per_group:(g + 1) * per_group]
        m1 = functools.reduce(jnp.maximum, vals)
        m2 = jnp.full_like(m1, neg_inf)
        found = jnp.zeros(m1.shape, jnp.bool_)
        for v in vals:
            eq = v == m1
            m2 = jnp.maximum(m2, jnp.where(eq & ~found, neg_inf, v))
            found = found | eq
        group_score.append(m1 + m2)

    masked = []
    for g in range(N_GROUPS):
        ahead = jnp.zeros(group_score[g].shape, I32)
        for o in range(N_GROUPS):
            if o == g:
                continue
            beats = group_score[o] > group_score[g]
            if o < g:
                beats = beats | (group_score[o] == group_score[g])
            ahead = ahead + beats.astype(I32)
        keep = ahead < TOPK_GROUPS
        masked += [jnp.where(keep, choice[e], neg_inf) for e in range(g * per_group, (g + 1) * per_group)]

    shape = masked[0].shape
    token = (i * SUBLANES + lax.broadcasted_iota(I32, shape, 0)) * LANES + lax.broadcasted_iota(I32, shape, 1)
    real = (token < n_tok).astype(I32)
    chosen = [jnp.zeros(shape, I32) for _ in range(N_EXPERTS)]
    picked_w = []
    for r in range(TOP_K):
        best = functools.reduce(jnp.maximum, masked)
        sel = jnp.full(shape, N_EXPERTS, I32)
        for e in reversed(range(N_EXPERTS)):
            sel = jnp.where(masked[e] == best, e, sel)
        w = jnp.zeros(shape, F32)
        for e in range(N_EXPERTS):
            hit = sel == e
            w = jnp.where(hit, scores[e], w)
            masked[e] = jnp.where(hit, neg_inf, masked[e])
            chosen[e] = jnp.where(hit, real, chosen[e])
        idx_ref[r] = sel
        picked_w.append(w)
    total = functools.reduce(lambda a, b: a + b, picked_w)
    for r in range(TOP_K):
        w_ref[r] = picked_w[r] / total * ROUTED_SCALE

    @pl.when(i == 0)
    def _():
        cnt_ref[...] = jnp.zeros(cnt_ref.shape, I32)

    for e in range(N_EXPERTS):
        cnt_ref[e] += chosen[e]


def _route_call(logits_t, bias, n_tok):
    n_e, t = logits_t.shape
    rows = t // LANES
    lg3 = logits_t.reshape(n_e, rows, LANES)
    pick = pl.BlockSpec((TOP_K, SUBLANES, LANES), lambda i: (0, i, 0))
    idx, w, cnt = pl.pallas_call(
        functools.partial(_route_body, n_tok=n_tok),
        grid=(rows // SUBLANES,),
        in_specs=[
            pl.BlockSpec(memory_space=pltpu.SMEM),
            pl.BlockSpec((n_e, SUBLANES, LANES), lambda i: (0, i, 0)),
        ],
        out_specs=[pick, pick, pl.BlockSpec((n_e, SUBLANES, LANES), lambda i: (0, 0, 0))],
        out_shape=[
            jax.ShapeDtypeStruct((TOP_K, rows, LANES), I32),
            jax.ShapeDtypeStruct((TOP_K, rows, LANES), F32),
            jax.ShapeDtypeStruct((n_e, SUBLANES, LANES), I32),
        ],
        compiler_params=_params("arbitrary"),
        name="route",
    )(bias, lg3)
    return idx.reshape(TOP_K, t), w.reshape(TOP_K, t), jnp.sum(cnt, axis=(1, 2))


def _moe_body(be_ref, nreal_ref, tok0_ref, tokn_ref, dstp_ref, hq_hbm, wg_ref, wu_ref, wd_ref, out_hbm,
              xb0, xb1, yb0, yb1, wgb, wub, wdb, gsem, ssem, *, rows):
    b = pl.program_id(0)
    nb = pl.num_programs(0)
    xbufs, ybufs = (xb0, xb1), (yb0, yb1)
    ns = xb0.shape[0] // rows
    grp = min(MOE_COPY_GROUP, rows)

    def n_real(k):
        return jnp.where((k >= 0) & (k < nb), nreal_ref[jnp.clip(k, 0, nb - 1)], 0)

    def token(ref, i):
        return ref.at[pl.ds(pl.multiple_of(i * ns, ns), ns), :]

    def start_groups(make_copy, n):
        for r in range(rows):
            @pl.when(n > (r // grp) * grp)
            def _(r=r):
                make_copy(r).start(priority=r % 2)

    def wait_groups(src, dst, sem, n):
        for g in range(rows // grp):
            @pl.when(n > g * grp)
            def _(g=g):
                part = pl.ds(g * grp * ns, grp * ns)
                pltpu.make_async_copy(src.at[part, :], dst.at[part, :], sem).wait()

    def gather(idx_ref, dst_buf, sem, n):
        start_groups(lambda r: pltpu.make_async_copy(token(hq_hbm, idx_ref[0, 0, r]), token(dst_buf, r), sem), n)

    @pl.when(b == 0)
    def _():
        xb0[...] = jnp.zeros(xb0.shape, I32)
        xb1[...] = jnp.zeros(xb1.shape, I32)
        gather(tok0_ref, xb0, gsem.at[0], n_real(0))

    @pl.when((b == 0) | (be_ref[b] != be_ref[jnp.maximum(b - 1, 0)]))
    def _():
        wgb[...] = wg_ref[...].astype(BF16)
        wub[...] = wu_ref[...].astype(BF16)
        wdb[...] = wd_ref[...].astype(BF16)

    def block(s):
        o = 1 - s
        wait_groups(hq_hbm, xbufs[s], gsem.at[s], n_real(b))
        gather(tokn_ref, xbufs[o], gsem.at[o], n_real(b + 1))
        start_groups(lambda r: pltpu.make_async_copy(token(ybufs[o], r), token(out_hbm, dstp_ref[0, 0, r]),
                                                     ssem.at[o]), n_real(b - 1))
        xb = _load_packed_rows(xbufs[s], (), rows, ns).astype(BF16)
        act = jax.nn.silu(_dot(xb, wgb[...])) * _dot(xb, wub[...])
        y = _dot(act.astype(BF16), wdb[...]).astype(BF16)
        wait_groups(ybufs[s], out_hbm, ssem.at[s], n_real(b - 2))
        _store_packed_rows(ybufs[s], (), y, rows)

    for s in range(2):
        @pl.when(b % 2 == s)
        def _(s=s):
            block(s)

    @pl.when(b == nb - 1)
    def _():
        wait_groups(yb0, out_hbm, ssem.at[0], n_real(b - 1))


def _moe_call(block_e, n_real, tok, dst, hq, w_gate, w_up, w_down, out_rows, rows):
    n_blocks = block_e.shape[0]
    assert n_blocks % 2 == 0 and rows % min(MOE_COPY_GROUP, rows) == 0
    n_e, d, f = w_gate.shape
    ns = d // (2 * LANES)
    shifted = lambda off: pl.BlockSpec(
        (1, 1, rows), lambda b, be, nr: (jnp.clip(b + off, 0, n_blocks - 1), 0, 0), memory_space=pltpu.SMEM)
    expert = lambda b, be, nr: (be[b], 0, 0)
    return pl.pallas_call(
        functools.partial(_moe_body, rows=rows),
        grid_spec=pltpu.PrefetchScalarGridSpec(
            num_scalar_prefetch=2,
            grid=(n_blocks,),
            in_specs=[
                pl.BlockSpec((1, 1, rows), lambda b, be, nr: (0, 0, 0), memory_space=pltpu.SMEM),
                shifted(1), shifted(-1),
                pl.BlockSpec(memory_space=pl.ANY),
                pl.BlockSpec((None, d, f), expert),
                pl.BlockSpec((None, d, f), expert),
                pl.BlockSpec((None, f, d), expert),
            ],
            out_specs=pl.BlockSpec(memory_space=pl.ANY),
            scratch_shapes=[pltpu.VMEM((rows * ns, LANES), I32)] * 4 + [
                pltpu.VMEM((d, f), BF16),
                pltpu.VMEM((d, f), BF16),
                pltpu.VMEM((f, d), BF16),
                pltpu.SemaphoreType.DMA((2,)),
                pltpu.SemaphoreType.DMA((2,)),
            ],
        ),
        out_shape=jax.ShapeDtypeStruct((out_rows * ns, LANES), I32),
        compiler_params=_params("arbitrary"),
        name="moe",
    )(block_e, n_real, tok, tok, dst, hq, w_gate, w_up, w_down)


def _dispatch_plan(idx, counts, n_tok, rows):
    m = n_tok * TOP_K
    n_real = (m + N_EXPERTS * (rows - 1) + rows - 1) // rows
    n_blocks = n_real + 1 + (n_real + 1) % 2
    p = n_blocks * rows
    n_spare = p - m
    plane = -(-(n_tok + -(-n_spare // TOP_K)) // SUBLANES) * SUBLANES
    flat_e = idx.T.reshape(m)
    order = jnp.argsort(flat_e).astype(I32)
    padded = (counts + rows - 1) // rows * rows
    pad_end = jnp.cumsum(padded)
    pad_start = pad_end - padded
    start = jnp.cumsum(counts) - counts
    block_first = jnp.arange(n_blocks, dtype=I32) * rows
    block_e = jnp.minimum(jnp.sum((pad_end[None, :] <= block_first[:, None]).astype(I32), axis=1), N_EXPERTS - 1)
    slot = block_first[:, None] + jnp.arange(rows, dtype=I32)[None, :]
    q = slot - pad_start[block_e][:, None]
    cnt_b = counts[block_e][:, None]
    start_b = start[block_e][:, None]
    real = q < cnt_b
    assign = order[jnp.clip(start_b + q, 0, m - 1)]
    t_of, j_of = assign // TOP_K, assign % TOP_K
    spare_rank = slot - (start_b + jnp.minimum(q, cnt_b))
    tok = jnp.where(real, t_of, 0)
    dst = jnp.where(real, j_of * plane + t_of, (spare_rank % TOP_K) * plane + n_tok + spare_rank // TOP_K)
    n_real_rows = jnp.clip(cnt_b[:, 0] - q[:, 0], 0, rows).astype(I32)
    shape = (n_blocks, 1, rows)
    return block_e.astype(I32), n_real_rows, tok.reshape(shape), dst.reshape(shape), plane


def _final_body(base_ref, y8_ref, w_ref, g_ref, b_ref, o_ref):
    w = w_ref[...]
    ffn = None
    tm = w.shape[0]
    ns = y8_ref.shape[1] // tm
    for j in range(TOP_K):
        yj = _load_packed_rows(y8_ref, (j,), tm, ns) * w[:, j:j + 1]
        ffn = yj if ffn is None else ffn + yj
    o_ref[...] = _layer_norm(base_ref[...] + ffn, g_ref[...], b_ref[...])


def _final_call(base, y8, w8, ln_g, ln_b, tm, row_offset):
    t, d = base.shape
    off = row_offset // tm
    ns = d // (2 * LANES)
    return pl.pallas_call(
        _final_body,
        grid=(t // tm,),
        in_specs=[
            pl.BlockSpec((tm, d), lambda i: (i, 0)),
            pl.BlockSpec((TOP_K, tm * ns, LANES), lambda i: (0, i + off, 0)),
            pl.BlockSpec((tm, TOP_K), lambda i: (i + off, 0)),
            pl.BlockSpec((1, d), lambda i: (0, 0)),
            pl.BlockSpec((1, d), lambda i: (0, 0)),
        ],
        out_specs=pl.BlockSpec((tm, d), lambda i: (i, 0)),
        out_shape=jax.ShapeDtypeStruct((t, d), F32),
        compiler_params=_params("parallel"),
        name="final",
    )(base, y8, w8, ln_g, ln_b)


def _mixer_and_post(x, pos_tables, w, tm, seq_rows, attend, conv_state, alpha, hq_tokens, hq_buf, hq_offset):
    tm_qkv = 2 * tm if pos_tables[0].shape[0] % (2 * tm) == 0 else tm
    q, k, v = _qkv_call(x, w['qkv'], pos_tables, tm_qkv)
    attn = attend(q, k, v)
    cb, u_tail = _conv_call(x, w['b'], w['c'], w['h'], w['conv'], tm, seq_rows, conv_state)
    mp = _gate_call(x, attn, cb, w['ga'], w['gc'], w['attn_out'], w['conv_out'], tm)
    hq, base, logits_t = _post_call(x, mp, w['o'], w['ln1_g'], w['ln1_b'], w['router_t'],
                                    w['sh_gate'], w['sh_up'], w['sh_down'], tm, alpha,
                                    hq_tokens, hq_buf, hq_offset)
    return k, v, u_tail, hq, base, logits_t


def kernel(x_prompt, x_sample, cache_k, cache_v, state_conv, w_in, attn_sinks, conv_w, w_attn_out, w_conv_out, w_o, ln1_g, ln1_b, w_router, router_bias, w_exp_gate, w_exp_up, w_exp_down, w_sh_gate, w_sh_up, w_sh_down, ln2_g, ln2_b):
    depth, d, _ = w_in.shape
    batch, seq, _ = x_prompt.shape
    dec_batch, dec_seq, _ = x_sample.shape
    win_buf = cache_k.shape[2]
    cdim = conv_w.shape[2]
    assert dec_seq == 1 and win_buf == WINDOW and seq % WINDOW == 0
    alpha = (2 * depth) ** 0.25
    t_p, t_s = batch * seq, dec_batch * dec_seq
    tm_p, tm_s = _tile(seq, 4 * LANES), _tile(t_s, LANES)
    assert t_p % tm_s == 0 and t_s % tm_s == 0
    tab_p = _rope_tables(jnp.arange(seq))
    tab_s = _rope_tables(jnp.full((tm_s,), PAST_LEN, I32))

    yp = x_prompt.reshape(t_p, d)
    ys = x_sample.reshape(t_s, d)
    p_k, p_v, p_c, s_k, s_v, s_c = [], [], [], [], [], []
    for l in range(depth):
        wl = w_in[l].astype(BF16)
        o = 0
        w = {}
        for name, width in (('qkv', Q_DIM + 2 * KV_DIM), ('b', cdim), ('c', cdim), ('h', cdim), ('ga', d), ('gc', d)):
            w[name] = wl[:, o:o + width]
            o += width
        w.update(
            conv=conv_w[l], attn_out=w_attn_out[l].astype(BF16), conv_out=w_conv_out[l].astype(BF16),
            o=w_o[l].astype(BF16), ln1_g=ln1_g[l][None], ln1_b=ln1_b[l][None],
            router_t=w_router[l].T.astype(BF16), sh_gate=w_sh_gate[l].astype(BF16),
            sh_up=w_sh_up[l].astype(BF16), sh_down=w_sh_down[l].astype(BF16))
        sinks = attn_sinks[l]

        n_tok = t_p + t_s
        k, v, u_tail, hq_p, base_p, lg_p = _mixer_and_post(
            yp, tab_p, w, tm_p, seq,
            lambda q, k, v: _attn_prompt_call(q, k, v, sinks, batch, seq), None, alpha, n_tok, None, 0)
        keep = min(WINDOW, seq)
        for kv, acc in ((k, p_k), (v, p_v)):
            tail_rows = kv.reshape(batch, seq, KV_DIM)[:, seq - keep:]
            acc.append(tail_rows.reshape(batch, keep, N_KV_HEADS, HEAD_DIM))
        tails = u_tail.reshape(batch, seq // tm_p, SUBLANES, cdim)
        p_c.append(tails[:, -1, SUBLANES - (CONV_WIDTH - 1):])

        new_kv = {}

        def attend_sample(q, k, v, l=l):
            new_kv['k'] = jnp.concatenate([cache_k[l][:, 1:], k.reshape(t_s, 1, N_KV_HEADS, HEAD_DIM)], axis=1)
            new_kv['v'] = jnp.concatenate([cache_v[l][:, 1:], v.reshape(t_s, 1, N_KV_HEADS, HEAD_DIM)], axis=1)
            return _attn_sample_call(q, new_kv['k'].reshape(t_s, win_buf, KV_DIM),
                                     new_kv['v'].reshape(t_s, win_buf, KV_DIM), sinks)

        state = (state_conv[l][:, 0], state_conv[l][:, 1])
        _, _, u_s, hq, base_s, lg_s = _mixer_and_post(ys, tab_s, w, tm_s, 1, attend_sample, state, alpha,
                                                      n_tok, hq_p, t_p)
        s_k.append(new_kv['k'])
        s_v.append(new_kv['v'])
        s_c.append(jnp.concatenate([state_conv[l][:, 1:], u_s[:, None]], axis=1))

        route_tile = SUBLANES * LANES
        t_pad = -(-n_tok // route_tile) * route_tile
        logits_t = jnp.concatenate([lg_p, lg_s, jnp.zeros((N_EXPERTS, t_pad - n_tok), F32)], axis=1)
        idx, wts, counts = _route_call(logits_t, router_bias[l], n_tok)
        idx, wts = idx[:, :n_tok], wts[:, :n_tok]
        block_e, n_real, tok, dst, plane = _dispatch_plan(idx, counts, n_tok, MOE_BLOCK_ROWS)
        y8 = _moe_call(block_e, n_real, tok, dst, hq, w_exp_gate[l], w_exp_up[l], w_exp_down[l],
                       TOP_K * plane, MOE_BLOCK_ROWS)
        y8 = y8.reshape(TOP_K, plane * (d // (2 * LANES)), LANES)
        w8 = wts.T
        yp = _final_call(base_p, y8, w8, ln2_g[l][None], ln2_b[l][None], min(tm_p, 2 * LANES), 0)
        ys = _final_call(base_s, y8, w8, ln2_g[l][None], ln2_b[l][None], tm_s, t_p)

    return (yp.reshape(batch, seq, d), ys.reshape(dec_batch, dec_seq, d), jnp.stack(p_k), jnp.stack(p_v),
            jnp.stack(p_c), jnp.stack(s_k), jnp.stack(s_v), jnp.stack(s_c))
```

```python
import functools

import jax
import jax.numpy as jnp
from jax import lax
from jax.experimental import pallas as pl
from jax.experimental.pallas import tpu as pltpu

N_HEADS = 16
N_KV_HEADS = 4
HEAD_DIM = 64
GROUP = N_HEADS // N_KV_HEADS
Q_DIM = N_HEADS * HEAD_DIM
KV_DIM = N_KV_HEADS * HEAD_DIM
ROT_DIM = HEAD_DIM // 4
ROPE_THETA = 500000.0
WINDOW = 128
CONV_WIDTH = 3
PAST_LEN = 16384
N_EXPERTS = 64
N_GROUPS = 8
TOPK_GROUPS = 4
TOP_K = 8
ROUTED_SCALE = 2.5
LN_EPS = 1e-5
MOE_BLOCK_ROWS = 256
MOE_COPY_GROUP = 32

LANES = 128
SUBLANES = 8
VMEM_LIMIT_BYTES = 48 * 1024 * 1024
NEG_BIG = -1e30

F32 = jnp.float32
BF16 = jnp.bfloat16
I32 = jnp.int32


def _tile(n, pref):
    t = pref
    while t > 1 and n % t:
        t //= 2
    return t


def _params(*sem):
    return pltpu.CompilerParams(dimension_semantics=sem, vmem_limit_bytes=VMEM_LIMIT_BYTES)


def _dot(a, b):
    return jnp.dot(a, b, preferred_element_type=F32)


def _dot_nt(a, b):
    return lax.dot_general(a, b, (((1,), (1,)), ((), ())), preferred_element_type=F32)


def _layer_norm(x, g, b):
    mu = jnp.mean(x, axis=-1, keepdims=True)
    xc = x - mu
    var = jnp.mean(xc * xc, axis=-1, keepdims=True)
    return xc * lax.rsqrt(var + LN_EPS) * g + b


def _pack_pair(lo, hi):
    lo32 = lax.bitcast_convert_type(lo.astype(F32), I32)
    hi32 = lax.bitcast_convert_type(hi.astype(F32), I32)
    return lax.shift_right_logical(lo32, 16) | (hi32 & -65536)


def _unpack_pair(p):
    lo = lax.bitcast_convert_type(lax.shift_left(p, 16), F32)
    hi = lax.bitcast_convert_type(p & -65536, F32)
    return lo, hi


def _store_packed_rows(ref, lead, x, n, first_row=0):
    ns = x.shape[1] // (2 * LANES)
    for c in range(ns):
        lo = x[:, 2 * c * LANES:(2 * c + 1) * LANES]
        hi = x[:, (2 * c + 1) * LANES:(2 * c + 2) * LANES]
        ref[lead + (pl.ds(first_row + c, n, stride=ns), slice(None))] = _pack_pair(lo, hi)


def _load_packed_rows(ref, lead, n, ns):
    parts = []
    for c in range(ns):
        parts += list(_unpack_pair(ref[lead + (pl.ds(c, n, stride=ns), slice(None))]))
    return jnp.concatenate(parts, axis=1)


def _rope_tables(pos):
    half = ROT_DIM // 2
    n = pos.shape[0]
    inv_freq = jnp.power(jnp.float32(ROPE_THETA), -jnp.arange(half, dtype=F32) * (2.0 / ROT_DIM))
    ang = pos.astype(F32)[:, None] * inv_freq[None, :]
    cos, sin = jnp.cos(ang), jnp.sin(ang)
    rest = HEAD_DIM - ROT_DIM
    cos_h = jnp.concatenate([cos, cos, jnp.ones((n, rest), F32)], axis=1)
    sa_h = jnp.concatenate([jnp.zeros((n, half), F32), sin, jnp.zeros((n, rest), F32)], axis=1)
    sb_h = jnp.concatenate([-sin, jnp.zeros((n, half + rest), F32)], axis=1)
    rep = LANES // HEAD_DIM
    return tuple(jnp.concatenate([t] * rep, axis=1) for t in (cos_h, sa_h, sb_h))


def _qkv_body(x_ref, w_ref, cos_ref, sa_ref, sb_ref, q_ref, k_ref, v_ref, xb_ref, *, nq):
    j = pl.program_id(1)

    @pl.when(j == 0)
    def _():
        xb_ref[...] = x_ref[...].astype(BF16)

    acc = _dot(xb_ref[...], w_ref[...])
    half = ROT_DIM // 2
    is_q = j < nq
    scale = jnp.where(is_q, jnp.float32(HEAD_DIM ** -0.5), jnp.float32(1.0))
    cos, sa, sb = cos_ref[...] * scale, sa_ref[...] * scale, sb_ref[...] * scale
    outs = []
    for c in range(acc.shape[1] // LANES):
        blk = acc[:, c * LANES:(c + 1) * LANES]
        roped = blk * cos + pltpu.roll(blk, half, 1) * sa + pltpu.roll(blk, LANES - half, 1) * sb
        outs.append(roped if c * LANES < KV_DIM else jnp.where(is_q, roped, blk))
    out = jnp.concatenate(outs, axis=1)

    @pl.when(is_q)
    def _():
        q_ref[...] = out.astype(BF16)

    @pl.when(j == nq)
    def _():
        k_ref[...] = out[:, :KV_DIM]
        v_ref[...] = out[:, KV_DIM:]


def _qkv_call(x, w_qkv, tables, tm):
    t, d = x.shape
    tn = 2 * KV_DIM
    assert Q_DIM % tn == 0
    nq = Q_DIM // tn
    tab_blocks = tables[0].shape[0] // tm
    tab_spec = pl.BlockSpec((tm, LANES), lambda i, j: (i % tab_blocks, 0))
    return pl.pallas_call(
        functools.partial(_qkv_body, nq=nq),
        grid=(t // tm, nq + 1),
        in_specs=[
            pl.BlockSpec((tm, d), lambda i, j: (i, 0)),
            pl.BlockSpec((d, tn), lambda i, j: (0, j)),
            tab_spec, tab_spec, tab_spec,
        ],
        out_specs=[
            pl.BlockSpec((tm, tn), lambda i, j: (i, jnp.minimum(j, nq - 1))),
            pl.BlockSpec((tm, KV_DIM), lambda i, j: (i, 0)),
            pl.BlockSpec((tm, KV_DIM), lambda i, j: (i, 0)),
        ],
        out_shape=[
            jax.ShapeDtypeStruct((t, Q_DIM), BF16),
            jax.ShapeDtypeStruct((t, KV_DIM), F32),
            jax.ShapeDtypeStruct((t, KV_DIM), F32),
        ],
        scratch_shapes=[pltpu.VMEM((tm, d), BF16)],
        compiler_params=_params("parallel", "arbitrary"),
        name="qkv",
    )(x, w_qkv, *tables)


def _head_pair_operands(kv_chunk, odd):
    lane = lax.broadcasted_iota(I32, kv_chunk.shape, 1)
    own = jnp.where((lane >= HEAD_DIM) == odd, kv_chunk, 0.0)
    other = pltpu.roll(own, HEAD_DIM, 1)
    lo, hi = (other, own) if odd else (own, other)
    return lo.astype(BF16), hi.astype(BF16)


def _attend(q, kk, vv, valid, sinks_ref):
    heads = []
    for kh in range(N_KV_HEADS):
        c = (kh * HEAD_DIM) // LANES
        odd = bool((kh * HEAD_DIM) % LANES)
        k_ops = _head_pair_operands(kk[:, c * LANES:(c + 1) * LANES], odd)
        v_ops = _head_pair_operands(vv[:, c * LANES:(c + 1) * LANES], odd)
        for g in range(GROUP):
            h = kh * GROUP + g
            heads.append((h // 2, h, k_ops[h % 2], v_ops[h % 2]))
    scores = [jnp.where(valid, _dot_nt(q[:, ch * LANES:(ch + 1) * LANES], k_op), NEG_BIG)
              for ch, _, k_op, _ in heads]
    maxes = [jnp.maximum(jnp.max(s, axis=-1, keepdims=True), sinks_ref[h]) for s, (_, h, _, _) in zip(scores, heads)]
    probs = [jnp.exp(s - m) for s, m in zip(scores, maxes)]
    dens = [jnp.sum(p, axis=-1, keepdims=True) + jnp.exp(sinks_ref[h] - m)
            for p, m, (_, h, _, _) in zip(probs, maxes, heads)]
    outs = [_dot((p / den).astype(BF16), v_op) for p, den, (_, _, _, v_op) in zip(probs, dens, heads)]
    return jnp.concatenate([outs[2 * j] + outs[2 * j + 1] for j in range(N_HEADS // 2)], axis=1)


def _attn_prompt_body(sinks_ref, q_ref, kp_ref, kc_ref, vp_ref, vc_ref, o_ref):
    n = pl.program_id(1)
    w = WINDOW
    kk = jnp.concatenate([kp_ref[...], kc_ref[...]], axis=0)
    vv = jnp.concatenate([vp_ref[...], vc_ref[...]], axis=0)
    a = lax.broadcasted_iota(I32, (w, 2 * w), 0)
    c = lax.broadcasted_iota(I32, (w, 2 * w), 1)
    valid = (c > a) & (c <= a + w) & ((n > 0) | (c >= w))
    o_ref[...] = _attend(q_ref[...], kk, vv, valid, sinks_ref).astype(BF16)


def _attn_prompt_call(q, k, v, sinks, batch, seq):
    w = WINDOW
    nb = seq // w
    cur = lambda b, n: (b * nb + n, 0)
    prev = lambda b, n: (b * nb + jnp.maximum(n - 1, 0), 0)
    return pl.pallas_call(
        _attn_prompt_body,
        grid=(batch, nb),
        in_specs=[
            pl.BlockSpec(memory_space=pltpu.SMEM),
            pl.BlockSpec((w, Q_DIM), cur),
            pl.BlockSpec((w, KV_DIM), prev),
            pl.BlockSpec((w, KV_DIM), cur),
            pl.BlockSpec((w, KV_DIM), prev),
            pl.BlockSpec((w, KV_DIM), cur),
        ],
        out_specs=pl.BlockSpec((w, Q_DIM), cur),
        out_shape=jax.ShapeDtypeStruct((batch * seq, Q_DIM), BF16),
        compiler_params=_params("parallel", "parallel"),
        name="attn_prompt",
    )(sinks, q, k, k, v, v)


def _attn_sample_body(sinks_ref, q_ref, kc_ref, vc_ref, kn_ref, vn_ref, o_ref, kw_ref, vw_ref, *, bt, nkeys):
    n = bt * nkeys
    key_i = jnp.concatenate([lax.broadcasted_iota(I32, (nkeys, KV_DIM), 0)] * bt, axis=0)

    def window(cache_ref, new_ref):
        shifted = pltpu.roll(cache_ref[...].reshape(n, KV_DIM), n - 1, 0)
        new = jnp.broadcast_to(new_ref[...][:, None, :], (bt, nkeys, KV_DIM)).reshape(n, KV_DIM)
        return jnp.where(key_i == nkeys - 1, new, shifted)

    kk, vv = window(kc_ref, kn_ref), window(vc_ref, vn_ref)
    kw_ref[...] = kk.reshape(bt, nkeys, KV_DIM)
    vw_ref[...] = vv.reshape(bt, nkeys, KV_DIM)
    row_b = lax.broadcasted_iota(I32, (bt, n), 0)
    key_b = jnp.concatenate([jnp.full((bt, nkeys), b, I32) for b in range(bt)], axis=1)
    o_ref[...] = _attend(q_ref[...], kk, vv, row_b == key_b, sinks_ref).astype(BF16)


def _attn_sample_call(q, k_cache, v_cache, k_new, v_new, sinks):
    b, nkeys, _ = k_cache.shape
    bt = _tile(b, SUBLANES)
    cache = pl.BlockSpec((bt, nkeys, KV_DIM), lambda i: (i, 0, 0))
    new = pl.BlockSpec((bt, KV_DIM), lambda i: (i, 0))
    return pl.pallas_call(
        functools.partial(_attn_sample_body, bt=bt, nkeys=nkeys),
        grid=(b // bt,),
        in_specs=[pl.BlockSpec(memory_space=pltpu.SMEM), pl.BlockSpec((bt, Q_DIM), lambda i: (i, 0)),
                  cache, cache, new, new],
        out_specs=[pl.BlockSpec((bt, Q_DIM), lambda i: (i, 0)), cache, cache],
        out_shape=[jax.ShapeDtypeStruct((b, Q_DIM), BF16), jax.ShapeDtypeStruct(k_cache.shape, F32),
                   jax.ShapeDtypeStruct(v_cache.shape, F32)],
        compiler_params=_params("parallel"),
        name="attn_sample",
    )(sinks, q, k_cache, v_cache, k_new, v_new)


def _conv_body(*refs, decode, tiles_per_seq, tm, tail):
    if decode:
        x_ref, wb_ref, wc_ref, wh_ref, cw_ref, s0_ref, s1_ref, cb_ref, ut_ref, xb_ref = refs
    else:
        x_ref, wb_ref, wc_ref, wh_ref, cw_ref, cb_ref, ut_ref, xb_ref, carry_ref = refs
    i = pl.program_id(0)
    c = pl.program_id(1)

    @pl.when(c == 0)
    def _():
        xb_ref[...] = x_ref[...].astype(BF16)

    if not decode:
        @pl.when(i % tiles_per_seq == 0)
        def _():
            carry_ref[c] = jnp.zeros(carry_ref.shape[1:], F32)

    xb = xb_ref[...]
    tc = wb_ref.shape[1]
    sub = min(tc, 2 * LANES)
    cols = [slice(k * sub, (k + 1) * sub) for k in range(tc // sub)]
    b_gs = [_dot(xb, wb_ref[:, s]) for s in cols]
    us = [_dot(xb, wc_ref[:, s]) * _dot(xb, wh_ref[:, s]) for s in cols]
    cw = cw_ref[...]
    for s, b_g, u in zip(cols, b_gs, us):
        if decode:
            u_m1, u_m2 = s1_ref[:, s], s0_ref[:, s]
        else:
            prev = carry_ref[c]
            p_m2, p_m1 = prev[SUBLANES - 2:SUBLANES - 1, s], prev[SUBLANES - 1:SUBLANES, s]
            r = lax.broadcasted_iota(I32, u.shape, 0)
            u_m1 = jnp.where(r == 0, p_m1, pltpu.roll(u, 1, 0))
            u_m2 = jnp.where(r == 0, p_m2, jnp.where(r == 1, p_m1, pltpu.roll(u, 2, 0)))
        conv = cw[0:1, s] * u_m2 + cw[1:2, s] * u_m1 + cw[2:3, s] * u
        cb_ref[:, s] = (b_g * conv).astype(BF16)
        ut_ref[:, s] = u[tm - tail:]
    if not decode:
        carry_ref[c] = jnp.concatenate([u[tm - SUBLANES:] for u in us], axis=1)


def _conv_call(x, w_b, w_c, w_h, conv_w, tm, seq_rows, state=None):
    t, d = x.shape
    cdim = w_b.shape[1]
    tc = _tile(cdim, 4 * LANES)
    decode = state is not None
    tail = tm if decode else SUBLANES
    w_spec = pl.BlockSpec((d, tc), lambda i, c: (0, c))
    in_specs = [pl.BlockSpec((tm, d), lambda i, c: (i, 0)), w_spec, w_spec, w_spec,
                pl.BlockSpec((CONV_WIDTH, tc), lambda i, c: (0, c))]
    args = [x, w_b, w_c, w_h, conv_w]
    scratch = [pltpu.VMEM((tm, d), BF16)]
    if decode:
        in_specs += [pl.BlockSpec((tm, tc), lambda i, c: (i, c))] * 2
        args += list(state)
    else:
        scratch.append(pltpu.VMEM((cdim // tc, SUBLANES, tc), F32))
    return pl.pallas_call(
        functools.partial(_conv_body, decode=decode, tiles_per_seq=max(seq_rows // tm, 1), tm=tm, tail=tail),
        grid=(t // tm, cdim // tc),
        in_specs=in_specs,
        out_specs=[pl.BlockSpec((tm, tc), lambda i, c: (i, c)), pl.BlockSpec((tail, tc), lambda i, c: (i, c))],
        out_shape=[jax.ShapeDtypeStruct((t, cdim), BF16), jax.ShapeDtypeStruct((t // tm * tail, cdim), F32)],
        scratch_shapes=scratch,
        compiler_params=_params("arbitrary", "arbitrary"),
        name="conv",
    )(*args)


def _gate_body(x_ref, at_ref, cb_ref, wga_ref, wgc_ref, wa_ref, wco_ref, o_ref, xb_ref):
    @pl.when(pl.program_id(1) == 0)
    def _():
        xb_ref[...] = x_ref[...].astype(BF16)

    xb = xb_ref[...]
    g_a = _dot(xb, wga_ref[...])
    g_c = _dot(xb, wgc_ref[...])
    a = _dot(at_ref[...], wa_ref[...])
    c = _dot(cb_ref[...], wco_ref[...])
    o_ref[...] = (jax.nn.sigmoid(g_a) * a + jax.nn.sigmoid(g_c) * c).astype(BF16)


def _gate_call(x, attn, cb, w_ga, w_gc, w_a, w_co, tm):
    t, d = x.shape
    tn = _tile(d, 4 * LANES)
    row = lambda i, n: (i, 0)
    col = lambda i, n: (0, n)
    return pl.pallas_call(
        _gate_body,
        grid=(t // tm, d // tn),
        in_specs=[
            pl.BlockSpec((tm, d), row),
            pl.BlockSpec((tm, attn.shape[1]), row),
            pl.BlockSpec((tm, cb.shape[1]), row),
            pl.BlockSpec((d, tn), col),
            pl.BlockSpec((d, tn), col),
            pl.BlockSpec((w_a.shape[0], tn), col),
            pl.BlockSpec((w_co.shape[0], tn), col),
        ],
        out_specs=pl.BlockSpec((tm, tn), lambda i, n: (i, n)),
        out_shape=jax.ShapeDtypeStruct((t, d), BF16),
        scratch_shapes=[pltpu.VMEM((tm, d), BF16)],
        compiler_params=_params("parallel", "arbitrary"),
        name="gate",
    )(x, attn, cb, w_ga, w_gc, w_a, w_co)


def _post_body(x_ref, mp_ref, wo_ref, g_ref, b_ref, wr_ref, wsg_ref, wsu_ref, wsd_ref, *rest, alpha):
    hq_ref, base_ref, lg_ref = rest[-3:]
    tm = x_ref.shape[0]
    halves = 2 if tm % (2 * LANES) == 0 else 1
    hr = tm // halves
    ns = x_ref.shape[1] // (2 * LANES)
    parts = [slice(i * hr, (i + 1) * hr) for i in range(halves)]
    mixed = [_dot(mp_ref[p, :], wo_ref[...]) for p in parts]
    hs = [_layer_norm(alpha * x_ref[p, :] + m, g_ref[...], b_ref[...]) for p, m in zip(parts, mixed)]
    hbs = [h.astype(BF16) for h in hs]
    for p, hb in zip(parts, hbs):
        lg_ref[:, p] = _dot_nt(wr_ref[...], hb)
    gates = [_dot(hb, wsg_ref[...]) for hb in hbs]
    ups = [_dot(hb, wsu_ref[...]) for hb in hbs]
    acts = [(jax.nn.silu(g) * u).astype(BF16) for g, u in zip(gates, ups)]
    for p, h, a in zip(parts, hs, acts):
        base_ref[p, :] = alpha * h + _dot(a, wsd_ref[...])
    for i, hb in enumerate(hbs):
        _store_packed_rows(hq_ref, (), hb, hr, first_row=i * hr * ns)


def _post_call(x, mp, w_o, ln_g, ln_b, w_r_t, w_sg, w_su, w_sd, tm, alpha, hq_tokens, hq_buf, hq_offset):
    t, d = x.shape
    n_e, f = w_r_t.shape[0], w_sg.shape[1]
    ns = d // (2 * LANES)
    off = hq_offset // tm
    row = lambda i: (i, 0)
    whole = lambda shape: pl.BlockSpec(shape, lambda i: (0, 0), pipeline_mode=pl.Buffered(1))
    in_specs = [
        pl.BlockSpec((tm, d), row),
        pl.BlockSpec((tm, d), row),
        whole((d, d)), whole((1, d)), whole((1, d)), whole((n_e, d)), whole((d, f)), whole((d, f)), whole((f, d)),
    ]
    args = [x, mp, w_o, ln_g, ln_b, w_r_t, w_sg, w_su, w_sd]
    aliases = {}
    if hq_buf is not None:
        aliases = {len(args): 0}
        in_specs.append(pl.BlockSpec(memory_space=pl.ANY))
        args.append(hq_buf)
    return pl.pallas_call(
        functools.partial(_post_body, alpha=alpha),
        grid=(t // tm,),
        in_specs=in_specs,
        out_specs=[
            pl.BlockSpec((tm * ns, LANES), lambda i: (i + off, 0)),
            pl.BlockSpec((tm, d), row),
            pl.BlockSpec((n_e, tm), lambda i: (0, i)),
        ],
        out_shape=[
            jax.ShapeDtypeStruct((hq_tokens * ns, LANES), I32),
            jax.ShapeDtypeStruct((t, d), F32),
            jax.ShapeDtypeStruct((n_e, t), F32),
        ],
        input_output_aliases=aliases,
        compiler_params=_params("parallel"),
        name="post",
    )(*args)


def _route_body(bias_ref, lg_ref, idx_ref, w_ref, cnt_ref, *, n_tok):
    per_group = N_EXPERTS // N_GROUPS
    neg_inf = jnp.float32(-jnp.inf)
    i = pl.program_id(0)
    scores = [jax.nn.sigmoid(lg_ref[e]) for e in range(N_EXPERTS)]
    choice = [scores[e] + bias_ref[e] for e in range(N_EXPERTS)]

    group_score = []
    for g in range(N_GROUPS):
        vals = choice[g * per_group:(g + 1) * per_group]
        m1 = functools.reduce(jnp.maximum, vals)
        m2 = jnp.full_like(m1, neg_inf)
        found = jnp.zeros(m1.shape, jnp.bool_)
        for v in vals:
            eq = v == m1
            m2 = jnp.maximum(m2, jnp.where(eq & ~found, neg_inf, v))
            found = found | eq
        group_score.append(m1 + m2)

    masked = []
    for g in range(N_GROUPS):
        ahead = jnp.zeros(group_score[g].shape, I32)
        for o in range(N_GROUPS):
            if o == g:
                continue
            beats = group_score[o] > group_score[g]
            if o < g:
                beats = beats | (group_score[o] == group_score[g])
            ahead = ahead + beats.astype(I32)
        keep = ahead < TOPK_GROUPS
        masked += [jnp.where(keep, choice[e], neg_inf) for e in range(g * per_group, (g + 1) * per_group)]

    shape = masked[0].shape
    token = (i * SUBLANES + lax.broadcasted_iota(I32, shape, 0)) * LANES + lax.broadcasted_iota(I32, shape, 1)
    real = (token < n_tok).astype(I32)
    chosen = [jnp.zeros(shape, I32) for _ in range(N_EXPERTS)]
    picked_w = []
    for r in range(TOP_K):
        best = functools.reduce(jnp.maximum, masked)
        sel = jnp.full(shape, N_EXPERTS, I32)
        for e in reversed(range(N_EXPERTS)):
            sel = jnp.where(masked[e] == best, e, sel)
        w = jnp.zeros(shape, F32)
        for e in range(N_EXPERTS):
            hit = sel == e
            w = jnp.where(hit, scores[e], w)
            masked[e] = jnp.where(hit, neg_inf, masked[e])
            chosen[e] = jnp.where(hit, real, chosen[e])
        idx_ref[r] = sel
        picked_w.append(w)
    total = functools.reduce(lambda a, b: a + b, picked_w)
    for r in range(TOP_K):
        w_ref[r] = picked_w[r] / total * ROUTED_SCALE

    @pl.when(i == 0)
    def _():
        cnt_ref[...] = jnp.zeros(cnt_ref.shape, I32)

    for e in range(N_EXPERTS):
        cnt_ref[e] += chosen[e]


def _route_call(logits_t, bias, n_tok):
    n_e, t = logits_t.shape
    rows = t // LANES
    lg3 = logits_t.reshape(n_e, rows, LANES)
    pick = pl.BlockSpec((TOP_K, SUBLANES, LANES), lambda i: (0, i, 0))
    idx, w, cnt = pl.pallas_call(
        functools.partial(_route_body, n_tok=n_tok),
        grid=(rows // SUBLANES,),
        in_specs=[
            pl.BlockSpec(memory_space=pltpu.SMEM),
            pl.BlockSpec((n_e, SUBLANES, LANES), lambda i: (0, i, 0)),
        ],
        out_specs=[pick, pick, pl.BlockSpec((n_e, SUBLANES, LANES), lambda i: (0, 0, 0))],
        out_shape=[
            jax.ShapeDtypeStruct((TOP_K, rows, LANES), I32),
            jax.ShapeDtypeStruct((TOP_K, rows, LANES), F32),
            jax.ShapeDtypeStruct((n_e, SUBLANES, LANES), I32),
        ],
        compiler_params=_params("arbitrary"),
        name="route",
    )(bias, lg3)
    return idx.reshape(TOP_K, t), w.reshape(TOP_K, t), jnp.sum(cnt, axis=(1, 2))


def _moe_body(be_ref, nreal_ref, tok0_ref, tokn_ref, dstp_ref, hq_hbm, wg_ref, wu_ref, wd_ref, out_hbm,
              xb0, xb1, yb0, yb1, wgb, wub, wdb, gsem, ssem, *, rows):
    b = pl.program_id(0)
    nb = pl.num_programs(0)
    xbufs, ybufs = (xb0, xb1), (yb0, yb1)
    ns = xb0.shape[0] // rows
    grp = min(MOE_COPY_GROUP, rows)

    def n_real(k):
        return jnp.where((k >= 0) & (k < nb), nreal_ref[jnp.clip(k, 0, nb - 1)], 0)

    def token(ref, i):
        return ref.at[pl.ds(pl.multiple_of(i * ns, ns), ns), :]

    def start_groups(make_copy, n):
        for r in range(rows):
            @pl.when(n > (r // grp) * grp)
            def _(r=r):
                make_copy(r).start(priority=r % 2)

    def wait_groups(src, dst, sem, n):
        for g in range(rows // grp):
            @pl.when(n > g * grp)
            def _(g=g):
                part = pl.ds(g * grp * ns, grp * ns)
                pltpu.make_async_copy(src.at[part, :], dst.at[part, :], sem).wait()

    def gather(idx_ref, dst_buf, sem, n):
        start_groups(lambda r: pltpu.make_async_copy(token(hq_hbm, idx_ref[0, 0, r]), token(dst_buf, r), sem), n)

    @pl.when(b == 0)
    def _():
        xb0[...] = jnp.zeros(xb0.shape, I32)
        xb1[...] = jnp.zeros(xb1.shape, I32)
        gather(tok0_ref, xb0, gsem.at[0], n_real(0))

    @pl.when((b == 0) | (be_ref[b] != be_ref[jnp.maximum(b - 1, 0)]))
    def _():
        wgb[...] = wg_ref[...].astype(BF16)
        wub[...] = wu_ref[...].astype(BF16)
        wdb[...] = wd_ref[...].astype(BF16)

    def block(s):
        o = 1 - s
        wait_groups(hq_hbm, xbufs[s], gsem.at[s], n_real(b))
        gather(tokn_ref, xbufs[o], gsem.at[o], n_real(b + 1))
        start_groups(lambda r: pltpu.make_async_copy(token(ybufs[o], r), token(out_hbm, dstp_ref[0, 0, r]),
                                                     ssem.at[o]), n_real(b - 1))
        xb = _load_packed_rows(xbufs[s], (), rows, ns).astype(BF16)
        act = jax.nn.silu(_dot(xb, wgb[...])) * _dot(xb, wub[...])
        y = _dot(act.astype(BF16), wdb[...]).astype(BF16)
        wait_groups(ybufs[s], out_hbm, ssem.at[s], n_real(b - 2))
        _store_packed_rows(ybufs[s], (), y, rows)

    for s in range(2):
        @pl.when(b % 2 == s)
        def _(s=s):
            block(s)

    @pl.when(b == nb - 1)
    def _():
        wait_groups(yb0, out_hbm, ssem.at[0], n_real(b - 1))


def _moe_call(block_e, n_real, tok, dst, hq, w_gate, w_up, w_down, out_rows, rows):
    n_blocks = block_e.shape[0]
    assert n_blocks % 2 == 0 and rows % min(MOE_COPY_GROUP, rows) == 0
    n_e, d, f = w_gate.shape
    ns = d // (2 * LANES)
    shifted = lambda off: pl.BlockSpec(
        (1, 1, rows), lambda b, be, nr: (jnp.clip(b + off, 0, n_blocks - 1), 0, 0), memory_space=pltpu.SMEM)
    expert = lambda b, be, nr: (be[b], 0, 0)
    return pl.pallas_call(
        functools.partial(_moe_body, rows=rows),
        grid_spec=pltpu.PrefetchScalarGridSpec(
            num_scalar_prefetch=2,
            grid=(n_blocks,),
            in_specs=[
                pl.BlockSpec((1, 1, rows), lambda b, be, nr: (0, 0, 0), memory_space=pltpu.SMEM),
                shifted(1), shifted(-1),
                pl.BlockSpec(memory_space=pl.ANY),
                pl.BlockSpec((None, d, f), expert),
                pl.BlockSpec((None, d, f), expert),
                pl.BlockSpec((None, f, d), expert),
            ],
            out_specs=pl.BlockSpec(memory_space=pl.ANY),
            scratch_shapes=[pltpu.VMEM((rows * ns, LANES), I32)] * 4 + [
                pltpu.VMEM((d, f), BF16),
                pltpu.VMEM((d, f), BF16),
                pltpu.VMEM((f, d), BF16),
                pltpu.SemaphoreType.DMA((2,)),
                pltpu.SemaphoreType.DMA((2,)),
            ],
        ),
        out_shape=jax.ShapeDtypeStruct((out_rows * ns, LANES), I32),
        compiler_params=_params("arbitrary"),
        name="moe",
    )(block_e, n_real, tok, tok, dst, hq, w_gate, w_up, w_down)


def _dispatch_plan(idx, counts, n_tok, rows):
    m = n_tok * TOP_K
    n_real = (m + N_EXPERTS * (rows - 1) + rows - 1) // rows
    n_blocks = n_real + 1 + (n_real + 1) % 2
    p = n_blocks * rows
    n_spare = p - m
    plane = -(-(n_tok + -(-n_spare // TOP_K)) // SUBLANES) * SUBLANES
    flat_e = idx.T.reshape(m)
    order = jnp.argsort(flat_e).astype(I32)
    padded = (counts + rows - 1) // rows * rows
    pad_end = jnp.cumsum(padded)
    pad_start = pad_end - padded
    start = jnp.cumsum(counts) - counts
    block_first = jnp.arange(n_blocks, dtype=I32) * rows
    block_e = jnp.minimum(jnp.sum((pad_end[None, :] <= block_first[:, None]).astype(I32), axis=1), N_EXPERTS - 1)
    slot = block_first[:, None] + jnp.arange(rows, dtype=I32)[None, :]
    own = block_e[:, None] == jnp.arange(N_EXPERTS, dtype=I32)[None, :]
    lookup = lambda table: jnp.sum(jnp.where(own, table[None, :], 0), axis=1, keepdims=True)
    q = slot - lookup(pad_start)
    cnt_b = lookup(counts)
    start_b = lookup(start)
    real = q < cnt_b
    assign = order[jnp.clip(start_b + q, 0, m - 1)]
    t_of, j_of = assign // TOP_K, assign % TOP_K
    spare_rank = slot - (start_b + jnp.minimum(q, cnt_b))
    tok = jnp.where(real, t_of, 0)
    dst = jnp.where(real, j_of * plane + t_of, (spare_rank % TOP_K) * plane + n_tok + spare_rank // TOP_K)
    n_real_rows = jnp.clip(cnt_b[:, 0] - q[:, 0], 0, rows).astype(I32)
    shape = (n_blocks, 1, rows)
    return block_e.astype(I32), n_real_rows, tok.reshape(shape), dst.reshape(shape), plane


def _final_body(base_ref, y8_ref, w_ref, g_ref, b_ref, o_ref):
    w = w_ref[...]
    ffn = None
    tm = w.shape[0]
    ns = y8_ref.shape[1] // tm
    for j in range(TOP_K):
        yj = _load_packed_rows(y8_ref, (j,), tm, ns) * w[:, j:j + 1]
        ffn = yj if ffn is None else ffn + yj
    o_ref[...] = _layer_norm(base_ref[...] + ffn, g_ref[...], b_ref[...])


def _final_call(base, y8, w8, ln_g, ln_b, tm, row_offset):
    t, d = base.shape
    off = row_offset // tm
    ns = d // (2 * LANES)
    return pl.pallas_call(
        _final_body,
        grid=(t // tm,),
        in_specs=[
            pl.BlockSpec((tm, d), lambda i: (i, 0)),
            pl.BlockSpec((TOP_K, tm * ns, LANES), lambda i: (0, i + off, 0)),
            pl.BlockSpec((tm, TOP_K), lambda i: (i + off, 0)),
            pl.BlockSpec((1, d), lambda i: (0, 0)),
            pl.BlockSpec((1, d), lambda i: (0, 0)),
        ],
        out_specs=pl.BlockSpec((tm, d), lambda i: (i, 0)),
        out_shape=jax.ShapeDtypeStruct((t, d), F32),
        compiler_params=_params("parallel"),
        name="final",
    )(base, y8, w8, ln_g, ln_b)


def _mixer_and_post(x, pos_tables, w, tm, seq_rows, attend, conv_state, alpha, hq_tokens, hq_buf, hq_offset):
    tm_qkv = 2 * tm if pos_tables[0].shape[0] % (2 * tm) == 0 else tm
    q, k, v = _qkv_call(x, w['qkv'], pos_tables, tm_qkv)
    attn = attend(q, k, v)
    cb, u_tail = _conv_call(x, w['b'], w['c'], w['h'], w['conv'], tm, seq_rows, conv_state)
    mp = _gate_call(x, attn, cb, w['ga'], w['gc'], w['attn_out'], w['conv_out'], tm)
    hq, base, logits_t = _post_call(x, mp, w['o'], w['ln1_g'], w['ln1_b'], w['router_t'],
                                    w['sh_gate'], w['sh_up'], w['sh_down'], tm, alpha,
                                    hq_tokens, hq_buf, hq_offset)
    return k, v, u_tail, hq, base, logits_t


def kernel(x_prompt, x_sample, cache_k, cache_v, state_conv, w_in, attn_sinks, conv_w, w_attn_out, w_conv_out, w_o, ln1_g, ln1_b, w_router, router_bias, w_exp_gate, w_exp_up, w_exp_down, w_sh_gate, w_sh_up, w_sh_down, ln2_g, ln2_b):
    depth, d, _ = w_in.shape
    batch, seq, _ = x_prompt.shape
    dec_batch, dec_seq, _ = x_sample.shape
    win_buf = cache_k.shape[2]
    cdim = conv_w.shape[2]
    assert dec_seq == 1 and win_buf == WINDOW and seq % WINDOW == 0
    alpha = (2 * depth) ** 0.25
    t_p, t_s = batch * seq, dec_batch * dec_seq
    tm_p, tm_s = _tile(seq, 4 * LANES), _tile(t_s, LANES)
    assert t_p % tm_s == 0 and t_s % tm_s == 0
    tab_p = _rope_tables(jnp.arange(seq))
    tab_s = _rope_tables(jnp.full((tm_s,), PAST_LEN, I32))

    yp = x_prompt.reshape(t_p, d)
    ys = x_sample.reshape(t_s, d)
    p_k, p_v, p_c, s_k, s_v, s_c = [], [], [], [], [], []
    for l in range(depth):
        wl = w_in[l].astype(BF16)
        o = 0
        w = {}
        for name, width in (('qkv', Q_DIM + 2 * KV_DIM), ('b', cdim), ('c', cdim), ('h', cdim), ('ga', d), ('gc', d)):
            w[name] = wl[:, o:o + width]
            o += width
        w.update(
            conv=conv_w[l], attn_out=w_attn_out[l].astype(BF16), conv_out=w_conv_out[l].astype(BF16),
            o=w_o[l].astype(BF16), ln1_g=ln1_g[l][None], ln1_b=ln1_b[l][None],
            router_t=w_router[l].T.astype(BF16), sh_gate=w_sh_gate[l].astype(BF16),
            sh_up=w_sh_up[l].astype(BF16), sh_down=w_sh_down[l].astype(BF16))
        sinks = attn_sinks[l]

        n_tok = t_p + t_s
        k, v, u_tail, hq_p, base_p, lg_p = _mixer_and_post(
            yp, tab_p, w, tm_p, seq,
            lambda q, k, v: _attn_prompt_call(q, k, v, sinks, batch, seq), None, alpha, n_tok, None, 0)
        keep = min(WINDOW, seq)
        for kv, acc in ((k, p_k), (v, p_v)):
            tail_rows = kv.reshape(batch, seq, KV_DIM)[:, seq - keep:]
            acc.append(tail_rows.reshape(batch, keep, N_KV_HEADS, HEAD_DIM))
        tails = u_tail.reshape(batch, seq // tm_p, SUBLANES, cdim)
        p_c.append(tails[:, -1, SUBLANES - (CONV_WIDTH - 1):])

        new_kv = {}

        def attend_sample(q, k, v, l=l):
            attn, k_win, v_win = _attn_sample_call(q, cache_k[l].reshape(t_s, win_buf, KV_DIM),
                                                   cache_v[l].reshape(t_s, win_buf, KV_DIM), k, v, sinks)
            new_kv['k'] = k_win.reshape(t_s, win_buf, N_KV_HEADS, HEAD_DIM)
            new_kv['v'] = v_win.reshape(t_s, win_buf, N_KV_HEADS, HEAD_DIM)
            return attn

        state = (state_conv[l][:, 0], state_conv[l][:, 1])
        _, _, u_s, hq, base_s, lg_s = _mixer_and_post(ys, tab_s, w, tm_s, 1, attend_sample, state, alpha,
                                                      n_tok, hq_p, t_p)
        s_k.append(new_kv['k'])
        s_v.append(new_kv['v'])
        s_c.append(jnp.concatenate([state_conv[l][:, 1:], u_s[:, None]], axis=1))

        route_tile = SUBLANES * LANES
        t_pad = -(-n_tok // route_tile) * route_tile
        logits_t = jnp.concatenate([lg_p, lg_s, jnp.zeros((N_EXPERTS, t_pad - n_tok), F32)], axis=1)
        idx, wts, counts = _route_call(logits_t, router_bias[l], n_tok)
        idx, wts = idx[:, :n_tok], wts[:, :n_tok]
        block_e, n_real, tok, dst, plane = _dispatch_plan(idx, counts, n_tok, MOE_BLOCK_ROWS)
        y8 = _moe_call(block_e, n_real, tok, dst, hq, w_exp_gate[l], w_exp_up[l], w_exp_down[l],
                       TOP_K * plane, MOE_BLOCK_ROWS)
        y8 = y8.reshape(TOP_K, plane * (d // (2 * LANES)), LANES)
        w8 = wts.T
        yp = _final_call(base_p, y8, w8, ln2_g[l][None], ln2_b[l][None], min(tm_p, 2 * LANES), 0)
        ys = _final_call(base_s, y8, w8, ln2_g[l][None], ln2_b[l][None], tm_s, t_p)

    return (yp.reshape(batch, seq, d), ys.reshape(dec_batch, dec_seq, d), jnp.stack(p_k), jnp.stack(p_v),
            jnp.stack(p_c), jnp.stack(s_k), jnp.stack(s_v), jnp.stack(s_c))
```

```python
import functools

import jax
import jax.numpy as jnp
from jax import lax
from jax.experimental import pallas as pl
from jax.experimental.pallas import tpu as pltpu

N_HEADS = 16
N_KV_HEADS = 4
HEAD_DIM = 64
GROUP = N_HEADS // N_KV_HEADS
Q_DIM = N_HEADS * HEAD_DIM
KV_DIM = N_KV_HEADS * HEAD_DIM
ROT_DIM = HEAD_DIM // 4
ROPE_THETA = 500000.0
WINDOW = 128
CONV_WIDTH = 3
PAST_LEN = 16384
N_EXPERTS = 64
N_GROUPS = 8
TOPK_GROUPS = 4
TOP_K = 8
ROUTED_SCALE = 2.5
LN_EPS = 1e-5
MOE_BLOCK_ROWS = 256
MOE_COPY_GROUP = 32

LANES = 128
SUBLANES = 8
VMEM_LIMIT_BYTES = 48 * 1024 * 1024
NEG_BIG = -1e30

F32 = jnp.float32
BF16 = jnp.bfloat16
I32 = jnp.int32


def _tile(n, pref):
    t = pref
    while t > 1 and n % t:
        t //= 2
    return t


def _params(*sem):
    return pltpu.CompilerParams(dimension_semantics=sem, vmem_limit_bytes=VMEM_LIMIT_BYTES)


def _dot(a, b):
    return jnp.dot(a, b, preferred_element_type=F32)


def _dot_nt(a, b):
    return lax.dot_general(a, b, (((1,), (1,)), ((), ())), preferred_element_type=F32)


def _layer_norm(x, g, b):
    mu = jnp.mean(x, axis=-1, keepdims=True)
    xc = x - mu
    var = jnp.mean(xc * xc, axis=-1, keepdims=True)
    return xc * lax.rsqrt(var + LN_EPS) * g + b


def _pack_pair(lo, hi):
    lo32 = lax.bitcast_convert_type(lo.astype(F32), I32)
    hi32 = lax.bitcast_convert_type(hi.astype(F32), I32)
    return lax.shift_right_logical(lo32, 16) | (hi32 & -65536)


def _unpack_pair(p):
    lo = lax.bitcast_convert_type(lax.shift_left(p, 16), F32)
    hi = lax.bitcast_convert_type(p & -65536, F32)
    return lo, hi


def _store_packed_rows(ref, lead, x, n, first_row=0):
    ns = x.shape[1] // (2 * LANES)
    for c in range(ns):
        lo = x[:, 2 * c * LANES:(2 * c + 1) * LANES]
        hi = x[:, (2 * c + 1) * LANES:(2 * c + 2) * LANES]
        ref[lead + (pl.ds(first_row + c, n, stride=ns), slice(None))] = _pack_pair(lo, hi)


def _load_packed_rows(ref, lead, n, ns):
    parts = []
    for c in range(ns):
        parts += list(_unpack_pair(ref[lead + (pl.ds(c, n, stride=ns), slice(None))]))
    return jnp.concatenate(parts, axis=1)


def _rope_tables(pos):
    half = ROT_DIM // 2
    n = pos.shape[0]
    inv_freq = jnp.power(jnp.float32(ROPE_THETA), -jnp.arange(half, dtype=F32) * (2.0 / ROT_DIM))
    ang = pos.astype(F32)[:, None] * inv_freq[None, :]
    cos, sin = jnp.cos(ang), jnp.sin(ang)
    rest = HEAD_DIM - ROT_DIM
    cos_h = jnp.concatenate([cos, cos, jnp.ones((n, rest), F32)], axis=1)
    sa_h = jnp.concatenate([jnp.zeros((n, half), F32), sin, jnp.zeros((n, rest), F32)], axis=1)
    sb_h = jnp.concatenate([-sin, jnp.zeros((n, half + rest), F32)], axis=1)
    rep = LANES // HEAD_DIM
    return tuple(jnp.concatenate([t] * rep, axis=1) for t in (cos_h, sa_h, sb_h))


def _qkv_body(x_ref, w_ref, cos_ref, sa_ref, sb_ref, q_ref, k_ref, v_ref, xb_ref, *, nq):
    j = pl.program_id(1)

    @pl.when(j == 0)
    def _():
        xb_ref[...] = x_ref[...].astype(BF16)

    acc = _dot(xb_ref[...], w_ref[...])
    half = ROT_DIM // 2
    is_q = j < nq
    scale = jnp.where(is_q, jnp.float32(HEAD_DIM ** -0.5), jnp.float32(1.0))
    cos, sa, sb = cos_ref[...] * scale, sa_ref[...] * scale, sb_ref[...] * scale
    outs = []
    for c in range(acc.shape[1] // LANES):
        blk = acc[:, c * LANES:(c + 1) * LANES]
        roped = blk * cos + pltpu.roll(blk, half, 1) * sa + pltpu.roll(blk, LANES - half, 1) * sb
        outs.append(roped if c * LANES < KV_DIM else jnp.where(is_q, roped, blk))
    out = jnp.concatenate(outs, axis=1)

    @pl.when(is_q)
    def _():
        q_ref[...] = out.astype(BF16)

    @pl.when(j == nq)
    def _():
        k_ref[...] = out[:, :KV_DIM]
        v_ref[...] = out[:, KV_DIM:]


def _qkv_call(x, w_qkv, tables, tm):
    t, d = x.shape
    tn = 2 * KV_DIM
    assert Q_DIM % tn == 0
    nq = Q_DIM // tn
    tab_blocks = tables[0].shape[0] // tm
    tab_spec = pl.BlockSpec((tm, LANES), lambda i, j: (i % tab_blocks, 0))
    return pl.pallas_call(
        functools.partial(_qkv_body, nq=nq),
        grid=(t // tm, nq + 1),
        in_specs=[
            pl.BlockSpec((tm, d), lambda i, j: (i, 0)),
            pl.BlockSpec((d, tn), lambda i, j: (0, j)),
            tab_spec, tab_spec, tab_spec,
        ],
        out_specs=[
            pl.BlockSpec((tm, tn), lambda i, j: (i, jnp.minimum(j, nq - 1))),
            pl.BlockSpec((tm, KV_DIM), lambda i, j: (i, 0)),
            pl.BlockSpec((tm, KV_DIM), lambda i, j: (i, 0)),
        ],
        out_shape=[
            jax.ShapeDtypeStruct((t, Q_DIM), BF16),
            jax.ShapeDtypeStruct((t, KV_DIM), F32),
            jax.ShapeDtypeStruct((t, KV_DIM), F32),
        ],
        scratch_shapes=[pltpu.VMEM((tm, d), BF16)],
        compiler_params=_params("parallel", "arbitrary"),
        name="qkv",
    )(x, w_qkv, *tables)


def _head_pair_operands(kv_chunk, odd):
    lane = lax.broadcasted_iota(I32, kv_chunk.shape, 1)
    own = jnp.where((lane >= HEAD_DIM) == odd, kv_chunk, 0.0)
    other = pltpu.roll(own, HEAD_DIM, 1)
    lo, hi = (other, own) if odd else (own, other)
    return lo.astype(BF16), hi.astype(BF16)


def _attend(q, kk, vv, valid, sinks_ref):
    heads = []
    for kh in range(N_KV_HEADS):
        c = (kh * HEAD_DIM) // LANES
        odd = bool((kh * HEAD_DIM) % LANES)
        k_ops = _head_pair_operands(kk[:, c * LANES:(c + 1) * LANES], odd)
        v_ops = _head_pair_operands(vv[:, c * LANES:(c + 1) * LANES], odd)
        for g in range(GROUP):
            h = kh * GROUP + g
            heads.append((h // 2, h, k_ops[h % 2], v_ops[h % 2]))
    scores = [jnp.where(valid, _dot_nt(q[:, ch * LANES:(ch + 1) * LANES], k_op), NEG_BIG)
              for ch, _, k_op, _ in heads]
    maxes = [jnp.maximum(jnp.max(s, axis=-1, keepdims=True), sinks_ref[h]) for s, (_, h, _, _) in zip(scores, heads)]
    probs = [jnp.exp(s - m) for s, m in zip(scores, maxes)]
    dens = [jnp.sum(p, axis=-1, keepdims=True) + jnp.exp(sinks_ref[h] - m)
            for p, m, (_, h, _, _) in zip(probs, maxes, heads)]
    outs = [_dot((p / den).astype(BF16), v_op) for p, den, (_, _, _, v_op) in zip(probs, dens, heads)]
    return jnp.concatenate([outs[2 * j] + outs[2 * j + 1] for j in range(N_HEADS // 2)], axis=1)


def _attn_prompt_body(sinks_ref, q_ref, kp_ref, kc_ref, vp_ref, vc_ref, o_ref):
    n = pl.program_id(1)
    w = WINDOW
    kk = jnp.concatenate([kp_ref[...], kc_ref[...]], axis=0)
    vv = jnp.concatenate([vp_ref[...], vc_ref[...]], axis=0)
    a = lax.broadcasted_iota(I32, (w, 2 * w), 0)
    c = lax.broadcasted_iota(I32, (w, 2 * w), 1)
    valid = (c > a) & (c <= a + w) & ((n > 0) | (c >= w))
    o_ref[...] = _attend(q_ref[...], kk, vv, valid, sinks_ref).astype(BF16)


def _attn_prompt_call(q, k, v, sinks, batch, seq):
    w = WINDOW
    nb = seq // w
    cur = lambda b, n: (b * nb + n, 0)
    prev = lambda b, n: (b * nb + jnp.maximum(n - 1, 0), 0)
    return pl.pallas_call(
        _attn_prompt_body,
        grid=(batch, nb),
        in_specs=[
            pl.BlockSpec(memory_space=pltpu.SMEM),
            pl.BlockSpec((w, Q_DIM), cur),
            pl.BlockSpec((w, KV_DIM), prev),
            pl.BlockSpec((w, KV_DIM), cur),
            pl.BlockSpec((w, KV_DIM), prev),
            pl.BlockSpec((w, KV_DIM), cur),
        ],
        out_specs=pl.BlockSpec((w, Q_DIM), cur),
        out_shape=jax.ShapeDtypeStruct((batch * seq, Q_DIM), BF16),
        compiler_params=_params("parallel", "parallel"),
        name="attn_prompt",
    )(sinks, q, k, k, v, v)


def _attn_sample_body(sinks_ref, q_ref, kc_ref, vc_ref, kn_ref, vn_ref, o_ref, kw_ref, vw_ref, *, bt, nkeys):
    n = bt * nkeys
    key_i = jnp.concatenate([lax.broadcasted_iota(I32, (nkeys, KV_DIM), 0)] * bt, axis=0)

    def window(cache_ref, new_ref):
        shifted = pltpu.roll(cache_ref[...].reshape(n, KV_DIM), n - 1, 0)
        new = jnp.broadcast_to(new_ref[...][:, None, :], (bt, nkeys, KV_DIM)).reshape(n, KV_DIM)
        return jnp.where(key_i == nkeys - 1, new, shifted)

    kk, vv = window(kc_ref, kn_ref), window(vc_ref, vn_ref)
    kw_ref[...] = kk.reshape(bt, nkeys, KV_DIM)
    vw_ref[...] = vv.reshape(bt, nkeys, KV_DIM)
    row_b = lax.broadcasted_iota(I32, (bt, n), 0)
    key_b = jnp.concatenate([jnp.full((bt, nkeys), b, I32) for b in range(bt)], axis=1)
    o_ref[...] = _attend(q_ref[...], kk, vv, row_b == key_b, sinks_ref).astype(BF16)


def _attn_sample_call(q, k_cache, v_cache, k_new, v_new, sinks):
    b, nkeys, _ = k_cache.shape
    bt = _tile(b, SUBLANES)
    cache = pl.BlockSpec((bt, nkeys, KV_DIM), lambda i: (i, 0, 0))
    new = pl.BlockSpec((bt, KV_DIM), lambda i: (i, 0))
    return pl.pallas_call(
        functools.partial(_attn_sample_body, bt=bt, nkeys=nkeys),
        grid=(b // bt,),
        in_specs=[pl.BlockSpec(memory_space=pltpu.SMEM), pl.BlockSpec((bt, Q_DIM), lambda i: (i, 0)),
                  cache, cache, new, new],
        out_specs=[pl.BlockSpec((bt, Q_DIM), lambda i: (i, 0)), cache, cache],
        out_shape=[jax.ShapeDtypeStruct((b, Q_DIM), BF16), jax.ShapeDtypeStruct(k_cache.shape, F32),
                   jax.ShapeDtypeStruct(v_cache.shape, F32)],
        compiler_params=_params("parallel"),
        name="attn_sample",
    )(sinks, q, k_cache, v_cache, k_new, v_new)


def _conv_body(*refs, decode, tiles_per_seq, tm, tail):
    if decode:
        x_ref, wb_ref, wc_ref, wh_ref, cw_ref, s0_ref, s1_ref, cb_ref, ut_ref, xb_ref = refs
    else:
        x_ref, wb_ref, wc_ref, wh_ref, cw_ref, cb_ref, ut_ref, xb_ref, carry_ref = refs
    i = pl.program_id(0)
    c = pl.program_id(1)

    @pl.when(c == 0)
    def _():
        xb_ref[...] = x_ref[...].astype(BF16)

    if not decode:
        @pl.when(i % tiles_per_seq == 0)
        def _():
            carry_ref[c] = jnp.zeros(carry_ref.shape[1:], F32)

    xb = xb_ref[...]
    tc = wb_ref.shape[1]
    sub = min(tc, 2 * LANES)
    cols = [slice(k * sub, (k + 1) * sub) for k in range(tc // sub)]
    b_gs = [_dot(xb, wb_ref[:, s]) for s in cols]
    us = [_dot(xb, wc_ref[:, s]) * _dot(xb, wh_ref[:, s]) for s in cols]
    cw = cw_ref[...]
    for s, b_g, u in zip(cols, b_gs, us):
        if decode:
            u_m1, u_m2 = s1_ref[:, s], s0_ref[:, s]
        else:
            prev = carry_ref[c]
            p_m2, p_m1 = prev[SUBLANES - 2:SUBLANES - 1, s], prev[SUBLANES - 1:SUBLANES, s]
            r = lax.broadcasted_iota(I32, u.shape, 0)
            u_m1 = jnp.where(r == 0, p_m1, pltpu.roll(u, 1, 0))
            u_m2 = jnp.where(r == 0, p_m2, jnp.where(r == 1, p_m1, pltpu.roll(u, 2, 0)))
        conv = cw[0:1, s] * u_m2 + cw[1:2, s] * u_m1 + cw[2:3, s] * u
        cb_ref[:, s] = (b_g * conv).astype(BF16)
        ut_ref[:, s] = u[tm - tail:]
    if not decode:
        carry_ref[c] = jnp.concatenate([u[tm - SUBLANES:] for u in us], axis=1)


def _conv_call(x, w_b, w_c, w_h, conv_w, tm, seq_rows, state=None):
    t, d = x.shape
    cdim = w_b.shape[1]
    tc = _tile(cdim, 4 * LANES)
    decode = state is not None
    tail = tm if decode else SUBLANES
    w_spec = pl.BlockSpec((d, tc), lambda i, c: (0, c))
    in_specs = [pl.BlockSpec((tm, d), lambda i, c: (i, 0)), w_spec, w_spec, w_spec,
                pl.BlockSpec((CONV_WIDTH, tc), lambda i, c: (0, c))]
    args = [x, w_b, w_c, w_h, conv_w]
    scratch = [pltpu.VMEM((tm, d), BF16)]
    if decode:
        in_specs += [pl.BlockSpec((tm, tc), lambda i, c: (i, c))] * 2
        args += list(state)
    else:
        scratch.append(pltpu.VMEM((cdim // tc, SUBLANES, tc), F32))
    return pl.pallas_call(
        functools.partial(_conv_body, decode=decode, tiles_per_seq=max(seq_rows // tm, 1), tm=tm, tail=tail),
        grid=(t // tm, cdim // tc),
        in_specs=in_specs,
        out_specs=[pl.BlockSpec((tm, tc), lambda i, c: (i, c)), pl.BlockSpec((tail, tc), lambda i, c: (i, c))],
        out_shape=[jax.ShapeDtypeStruct((t, cdim), BF16), jax.ShapeDtypeStruct((t // tm * tail, cdim), F32)],
        scratch_shapes=scratch,
        compiler_params=_params("arbitrary", "arbitrary"),
        name="conv",
    )(*args)


def _gate_body(x_ref, at_ref, cb_ref, wga_ref, wgc_ref, wa_ref, wco_ref, o_ref, xb_ref):
    @pl.when(pl.program_id(1) == 0)
    def _():
        xb_ref[...] = x_ref[...].astype(BF16)

    xb = xb_ref[...]
    g_a = _dot(xb, wga_ref[...])
    g_c = _dot(xb, wgc_ref[...])
    a = _dot(at_ref[...], wa_ref[...])
    c = _dot(cb_ref[...], wco_ref[...])
    o_ref[...] = (jax.nn.sigmoid(g_a) * a + jax.nn.sigmoid(g_c) * c).astype(BF16)


def _gate_call(x, attn, cb, w_ga, w_gc, w_a, w_co, tm):
    t, d = x.shape
    tn = _tile(d, 4 * LANES)
    row = lambda i, n: (i, 0)
    col = lambda i, n: (0, n)
    return pl.pallas_call(
        _gate_body,
        grid=(t // tm, d // tn),
        in_specs=[
            pl.BlockSpec((tm, d), row),
            pl.BlockSpec((tm, attn.shape[1]), row),
            pl.BlockSpec((tm, cb.shape[1]), row),
            pl.BlockSpec((d, tn), col),
            pl.BlockSpec((d, tn), col),
            pl.BlockSpec((w_a.shape[0], tn), col),
            pl.BlockSpec((w_co.shape[0], tn), col),
        ],
        out_specs=pl.BlockSpec((tm, tn), lambda i, n: (i, n)),
        out_shape=jax.ShapeDtypeStruct((t, d), BF16),
        scratch_shapes=[pltpu.VMEM((tm, d), BF16)],
        compiler_params=_params("parallel", "arbitrary"),
        name="gate",
    )(x, attn, cb, w_ga, w_gc, w_a, w_co)


def _post_body(x_ref, mp_ref, wo_ref, g_ref, b_ref, wr_ref, wsg_ref, wsu_ref, wsd_ref, *rest, alpha):
    hq_ref, base_ref, lg_ref = rest[-3:]
    tm = x_ref.shape[0]
    halves = 2 if tm % (2 * LANES) == 0 else 1
    hr = tm // halves
    ns = x_ref.shape[1] // (2 * LANES)
    parts = [slice(i * hr, (i + 1) * hr) for i in range(halves)]
    mixed = [_dot(mp_ref[p, :], wo_ref[...]) for p in parts]
    hs = [_layer_norm(alpha * x_ref[p, :] + m, g_ref[...], b_ref[...]) for p, m in zip(parts, mixed)]
    hbs = [h.astype(BF16) for h in hs]
    for p, hb in zip(parts, hbs):
        lg_ref[:, p] = _dot_nt(wr_ref[...], hb)
    gates = [_dot(hb, wsg_ref[...]) for hb in hbs]
    ups = [_dot(hb, wsu_ref[...]) for hb in hbs]
    acts = [(jax.nn.silu(g) * u).astype(BF16) for g, u in zip(gates, ups)]
    for p, h, a in zip(parts, hs, acts):
        base_ref[p, :] = alpha * h + _dot(a, wsd_ref[...])
    for i, hb in enumerate(hbs):
        _store_packed_rows(hq_ref, (), hb, hr, first_row=i * hr * ns)


def _post_call(x, mp, w_o, ln_g, ln_b, w_r_t, w_sg, w_su, w_sd, tm, alpha, hq_tokens, hq_buf, hq_offset):
    t, d = x.shape
    n_e, f = w_r_t.shape[0], w_sg.shape[1]
    ns = d // (2 * LANES)
    off = hq_offset // tm
    row = lambda i: (i, 0)
    whole = lambda shape: pl.BlockSpec(shape, lambda i: (0, 0), pipeline_mode=pl.Buffered(1))
    in_specs = [
        pl.BlockSpec((tm, d), row),
        pl.BlockSpec((tm, d), row),
        whole((d, d)), whole((1, d)), whole((1, d)), whole((n_e, d)), whole((d, f)), whole((d, f)), whole((f, d)),
    ]
    args = [x, mp, w_o, ln_g, ln_b, w_r_t, w_sg, w_su, w_sd]
    aliases = {}
    if hq_buf is not None:
        aliases = {len(args): 0}
        in_specs.append(pl.BlockSpec(memory_space=pl.ANY))
        args.append(hq_buf)
    return pl.pallas_call(
        functools.partial(_post_body, alpha=alpha),
        grid=(t // tm,),
        in_specs=in_specs,
        out_specs=[
            pl.BlockSpec((tm * ns, LANES), lambda i: (i + off, 0)),
            pl.BlockSpec((tm, d), row),
            pl.BlockSpec((n_e, tm), lambda i: (0, i)),
        ],
        out_shape=[
            jax.ShapeDtypeStruct((hq_tokens * ns, LANES), I32),
            jax.ShapeDtypeStruct((t, d), F32),
            jax.ShapeDtypeStruct((n_e, t), F32),
        ],
        input_output_aliases=aliases,
        compiler_params=_params("parallel"),
        name="post",
    )(*args)


def _route_body(bias_ref, lg_ref, idx_ref, w_ref, cnt_ref, *, n_tok):
    per_group = N_EXPERTS // N_GROUPS
    neg_inf = jnp.float32(-jnp.inf)
    i = pl.program_id(0)
    scores = [jax.nn.sigmoid(lg_ref[e]) for e in range(N_EXPERTS)]
    choice = [scores[e] + bias_ref[e] for e in range(N_EXPERTS)]

    group_score = []
    for g in range(N_GROUPS):
        vals = choice[g * per_group:(g + 1) * per_group]
        m1 = functools.reduce(jnp.maximum, vals)
        m2 = jnp.full_like(m1, neg_inf)
        found = jnp.zeros(m1.shape, jnp.bool_)
        for v in vals:
            eq = v == m1
            m2 = jnp.maximum(m2, jnp.where(eq & ~found, neg_inf, v))
            found = found | eq
        group_score.append(m1 + m2)

    masked = []
    for g in range(N_GROUPS):
        ahead = jnp.zeros(group_score[g].shape, I32)
        for o in range(N_GROUPS):
            if o == g:
                continue
            beats = group_score[o] > group_score[g]
            if o < g:
                beats = beats | (group_score[o] == group_score[g])
            ahead = ahead + beats.astype(I32)
        keep = ahead < TOPK_GROUPS
        masked += [jnp.where(keep, choice[e], neg_inf) for e in range(g * per_group, (g + 1) * per_group)]

    shape = masked[0].shape
    token = (i * SUBLANES + lax.broadcasted_iota(I32, shape, 0)) * LANES + lax.broadcasted_iota(I32, shape, 1)
    real = (token < n_tok).astype(I32)
    chosen = [jnp.zeros(shape, I32) for _ in range(N_EXPERTS)]
    picked_w = []
    for r in range(TOP_K):
        best = functools.reduce(jnp.maximum, masked)
        sel = jnp.full(shape, N_EXPERTS, I32)
        for e in reversed(range(N_EXPERTS)):
            sel = jnp.where(masked[e] == best, e, sel)
        w = jnp.zeros(shape, F32)
        for e in range(N_EXPERTS):
            hit = sel == e
            w = jnp.where(hit, scores[e], w)
            masked[e] = jnp.where(hit, neg_inf, masked[e])
            chosen[e] = jnp.where(hit, real, chosen[e])
        idx_ref[r] = sel
        picked_w.append(w)
    total = functools.reduce(lambda a, b: a + b, picked_w)
    for r in range(TOP_K):
        w_ref[r] = picked_w[r] / total * ROUTED_SCALE

    @pl.when(i == 0)
    def _():
        cnt_ref[...] = jnp.zeros(cnt_ref.shape, I32)

    for e in range(N_EXPERTS):
        cnt_ref[e] += chosen[e]


def _route_call(logits_t, bias, n_tok):
    n_e, t = logits_t.shape
    rows = t // LANES
    lg3 = logits_t.reshape(n_e, rows, LANES)
    pick = pl.BlockSpec((TOP_K, SUBLANES, LANES), lambda i: (0, i, 0))
    idx, w, cnt = pl.pallas_call(
        functools.partial(_route_body, n_tok=n_tok),
        grid=(rows // SUBLANES,),
        in_specs=[
            pl.BlockSpec(memory_space=pltpu.SMEM),
            pl.BlockSpec((n_e, SUBLANES, LANES), lambda i: (0, i, 0)),
        ],
        out_specs=[pick, pick, pl.BlockSpec((n_e, SUBLANES, LANES), lambda i: (0, 0, 0))],
        out_shape=[
            jax.ShapeDtypeStruct((TOP_K, rows, LANES), I32),
            jax.ShapeDtypeStruct((TOP_K, rows, LANES), F32),
            jax.ShapeDtypeStruct((n_e, SUBLANES, LANES), I32),
        ],
        compiler_params=_params("arbitrary"),
        name="route",
    )(bias, lg3)
    return idx.reshape(TOP_K, t), w.reshape(TOP_K, t), jnp.sum(cnt, axis=(1, 2))


def _moe_body(be_ref, nreal_ref, tok0_ref, tokn_ref, dstp_ref, hq_hbm, wg_ref, wu_ref, wd_ref, out_hbm,
              xb0, xb1, yb0, yb1, wgb, wub, wdb, gsem, ssem, *, rows):
    b = pl.program_id(0)
    nb = pl.num_programs(0)
    xbufs, ybufs = (xb0, xb1), (yb0, yb1)
    ns = xb0.shape[0] // rows
    grp = min(MOE_COPY_GROUP, rows)

    def n_real(k):
        return jnp.where((k >= 0) & (k < nb), nreal_ref[jnp.clip(k, 0, nb - 1)], 0)

    def token(ref, i):
        return ref.at[pl.ds(pl.multiple_of(i * ns, ns), ns), :]

    def start_groups(make_copy, n):
        for r in range(rows):
            @pl.when(n > (r // grp) * grp)
            def _(r=r):
                make_copy(r).start(priority=r % 2)

    def wait_groups(src, dst, sem, n):
        for g in range(rows // grp):
            @pl.when(n > g * grp)
            def _(g=g):
                part = pl.ds(g * grp * ns, grp * ns)
                pltpu.make_async_copy(src.at[part, :], dst.at[part, :], sem).wait()

    def gather(idx_ref, dst_buf, sem, n):
        start_groups(lambda r: pltpu.make_async_copy(token(hq_hbm, idx_ref[0, 0, r]), token(dst_buf, r), sem), n)

    @pl.when(b == 0)
    def _():
        xb0[...] = jnp.zeros(xb0.shape, I32)
        xb1[...] = jnp.zeros(xb1.shape, I32)
        gather(tok0_ref, xb0, gsem.at[0], n_real(0))

    @pl.when((b == 0) | (be_ref[b] != be_ref[jnp.maximum(b - 1, 0)]))
    def _():
        wgb[...] = wg_ref[...].astype(BF16)
        wub[...] = wu_ref[...].astype(BF16)
        wdb[...] = wd_ref[...].astype(BF16)

    def scatter_previous(o):
        start_groups(lambda r: pltpu.make_async_copy(token(ybufs[o], r), token(out_hbm, dstp_ref[0, 0, r]),
                                                     ssem.at[o]), n_real(b - 1))

    def block(s):
        o = 1 - s
        wait_groups(hq_hbm, xbufs[s], gsem.at[s], n_real(b))
        gather(tokn_ref, xbufs[o], gsem.at[o], n_real(b + 1))
        scatter_previous(o)
        xb = _load_packed_rows(xbufs[s], (), rows, ns).astype(BF16)
        act = jax.nn.silu(_dot(xb, wgb[...])) * _dot(xb, wub[...])
        y = _dot(act.astype(BF16), wdb[...]).astype(BF16)
        wait_groups(ybufs[s], out_hbm, ssem.at[s], n_real(b - 2))
        _store_packed_rows(ybufs[s], (), y, rows)

    def empty_block(s):
        scatter_previous(1 - s)
        wait_groups(ybufs[s], out_hbm, ssem.at[s], n_real(b - 2))

    for s in range(2):
        @pl.when((b % 2 == s) & (n_real(b) > 0))
        def _(s=s):
            block(s)

        @pl.when((b % 2 == s) & (n_real(b) == 0))
        def _(s=s):
            empty_block(s)

    @pl.when(b == nb - 1)
    def _():
        wait_groups(yb0, out_hbm, ssem.at[0], n_real(b - 1))


def _moe_call(block_e, n_real, tok, dst, hq, w_gate, w_up, w_down, out_rows, rows):
    n_blocks = block_e.shape[0]
    assert n_blocks % 2 == 0 and rows % min(MOE_COPY_GROUP, rows) == 0
    n_e, d, f = w_gate.shape
    ns = d // (2 * LANES)
    shifted = lambda off: pl.BlockSpec(
        (1, 1, rows), lambda b, be, nr: (jnp.clip(b + off, 0, n_blocks - 1), 0, 0), memory_space=pltpu.SMEM)
    expert = lambda b, be, nr: (be[b], 0, 0)
    return pl.pallas_call(
        functools.partial(_moe_body, rows=rows),
        grid_spec=pltpu.PrefetchScalarGridSpec(
            num_scalar_prefetch=2,
            grid=(n_blocks,),
            in_specs=[
                pl.BlockSpec((1, 1, rows), lambda b, be, nr: (0, 0, 0), memory_space=pltpu.SMEM),
                shifted(1), shifted(-1),
                pl.BlockSpec(memory_space=pl.ANY),
                pl.BlockSpec((None, d, f), expert),
                pl.BlockSpec((None, d, f), expert),
                pl.BlockSpec((None, f, d), expert),
            ],
            out_specs=pl.BlockSpec(memory_space=pl.ANY),
            scratch_shapes=[pltpu.VMEM((rows * ns, LANES), I32)] * 4 + [
                pltpu.VMEM((d, f), BF16),
                pltpu.VMEM((d, f), BF16),
                pltpu.VMEM((f, d), BF16),
                pltpu.SemaphoreType.DMA((2,)),
                pltpu.SemaphoreType.DMA((2,)),
            ],
        ),
        out_shape=jax.ShapeDtypeStruct((out_rows * ns, LANES), I32),
        compiler_params=_params("arbitrary"),
        name="moe",
    )(block_e, n_real, tok, tok, dst, hq, w_gate, w_up, w_down)


def _dispatch_plan(idx, counts, n_tok, rows):
    m = n_tok * TOP_K
    n_real = (m + N_EXPERTS * (rows - 1) + rows - 1) // rows
    n_blocks = n_real + 1 + (n_real + 1) % 2
    p = n_blocks * rows
    n_spare = p - m
    plane = -(-(n_tok + -(-n_spare // TOP_K)) // SUBLANES) * SUBLANES
    flat_e = idx.T.reshape(m)
    order = jnp.argsort(flat_e).astype(I32)
    padded = (counts + rows - 1) // rows * rows
    pad_end = jnp.cumsum(padded)
    pad_start = pad_end - padded
    start = jnp.cumsum(counts) - counts
    block_first = jnp.arange(n_blocks, dtype=I32) * rows
    block_e = jnp.minimum(jnp.sum((pad_end[None, :] <= block_first[:, None]).astype(I32), axis=1), N_EXPERTS - 1)
    slot = block_first[:, None] + jnp.arange(rows, dtype=I32)[None, :]
    own = block_e[:, None] == jnp.arange(N_EXPERTS, dtype=I32)[None, :]
    lookup = lambda table: jnp.sum(jnp.where(own, table[None, :], 0), axis=1, keepdims=True)
    q = slot - lookup(pad_start)
    cnt_b = lookup(counts)
    start_b = lookup(start)
    real = q < cnt_b
    assign = order[jnp.clip(start_b + q, 0, m - 1)]
    t_of, j_of = assign // TOP_K, assign % TOP_K
    spare_rank = slot - (start_b + jnp.minimum(q, cnt_b))
    tok = jnp.where(real, t_of, 0)
    dst = jnp.where(real, j_of * plane + t_of, (spare_rank % TOP_K) * plane + n_tok + spare_rank // TOP_K)
    n_real_rows = jnp.clip(cnt_b[:, 0] - q[:, 0], 0, rows).astype(I32)
    shape = (n_blocks, 1, rows)
    return block_e.astype(I32), n_real_rows, tok.reshape(shape), dst.reshape(shape), plane


def _final_body(base_ref, y8_ref, w_ref, g_ref, b_ref, o_ref):
    w = w_ref[...]
    ffn = None
    tm = w.shape[0]
    ns = y8_ref.shape[1] // tm
    for j in range(TOP_K):
        yj = _load_packed_rows(y8_ref, (j,), tm, ns) * w[:, j:j + 1]
        ffn = yj if ffn is None else ffn + yj
    o_ref[...] = _layer_norm(base_ref[...] + ffn, g_ref[...], b_ref[...])


def _final_call(base, y8, w8, ln_g, ln_b, tm, row_offset):
    t, d = base.shape
    off = row_offset // tm
    ns = d // (2 * LANES)
    return pl.pallas_call(
        _final_body,
        grid=(t // tm,),
        in_specs=[
            pl.BlockSpec((tm, d), lambda i: (i, 0)),
            pl.BlockSpec((TOP_K, tm * ns, LANES), lambda i: (0, i + off, 0)),
            pl.BlockSpec((tm, TOP_K), lambda i: (i + off, 0)),
            pl.BlockSpec((1, d), lambda i: (0, 0)),
            pl.BlockSpec((1, d), lambda i: (0, 0)),
        ],
        out_specs=pl.BlockSpec((tm, d), lambda i: (i, 0)),
        out_shape=jax.ShapeDtypeStruct((t, d), F32),
        compiler_params=_params("parallel"),
        name="final",
    )(base, y8, w8, ln_g, ln_b)


def _mixer_and_post(x, pos_tables, w, tm, seq_rows, attend, conv_state, alpha, hq_tokens, hq_buf, hq_offset):
    tm_qkv = 2 * tm if pos_tables[0].shape[0] % (2 * tm) == 0 else tm
    q, k, v = _qkv_call(x, w['qkv'], pos_tables, tm_qkv)
    attn = attend(q, k, v)
    cb, u_tail = _conv_call(x, w['b'], w['c'], w['h'], w['conv'], tm, seq_rows, conv_state)
    mp = _gate_call(x, attn, cb, w['ga'], w['gc'], w['attn_out'], w['conv_out'], tm)
    hq, base, logits_t = _post_call(x, mp, w['o'], w['ln1_g'], w['ln1_b'], w['router_t'],
                                    w['sh_gate'], w['sh_up'], w['sh_down'], tm, alpha,
                                    hq_tokens, hq_buf, hq_offset)
    return k, v, u_tail, hq, base, logits_t


def kernel(x_prompt, x_sample, cache_k, cache_v, state_conv, w_in, attn_sinks, conv_w, w_attn_out, w_conv_out, w_o, ln1_g, ln1_b, w_router, router_bias, w_exp_gate, w_exp_up, w_exp_down, w_sh_gate, w_sh_up, w_sh_down, ln2_g, ln2_b):
    depth, d, _ = w_in.shape
    batch, seq, _ = x_prompt.shape
    dec_batch, dec_seq, _ = x_sample.shape
    win_buf = cache_k.shape[2]
    cdim = conv_w.shape[2]
    assert dec_seq == 1 and win_buf == WINDOW and seq % WINDOW == 0
    alpha = (2 * depth) ** 0.25
    t_p, t_s = batch * seq, dec_batch * dec_seq
    tm_p, tm_s = _tile(seq, 4 * LANES), _tile(t_s, LANES)
    assert t_p % tm_s == 0 and t_s % tm_s == 0
    tab_p = _rope_tables(jnp.arange(seq))
    tab_s = _rope_tables(jnp.full((tm_s,), PAST_LEN, I32))

    yp = x_prompt.reshape(t_p, d)
    ys = x_sample.reshape(t_s, d)
    p_k, p_v, p_c, s_k, s_v, s_c = [], [], [], [], [], []
    for l in range(depth):
        wl = w_in[l].astype(BF16)
        o = 0
        w = {}
        for name, width in (('qkv', Q_DIM + 2 * KV_DIM), ('b', cdim), ('c', cdim), ('h', cdim), ('ga', d), ('gc', d)):
            w[name] = wl[:, o:o + width]
            o += width
        w.update(
            conv=conv_w[l], attn_out=w_attn_out[l].astype(BF16), conv_out=w_conv_out[l].astype(BF16),
            o=w_o[l].astype(BF16), ln1_g=ln1_g[l][None], ln1_b=ln1_b[l][None],
            router_t=w_router[l].T.astype(BF16), sh_gate=w_sh_gate[l].astype(BF16),
            sh_up=w_sh_up[l].astype(BF16), sh_down=w_sh_down[l].astype(BF16))
        sinks = attn_sinks[l]

        n_tok = t_p + t_s
        k, v, u_tail, hq_p, base_p, lg_p = _mixer_and_post(
            yp, tab_p, w, tm_p, seq,
            lambda q, k, v: _attn_prompt_call(q, k, v, sinks, batch, seq), None, alpha, n_tok, None, 0)
        keep = min(WINDOW, seq)
        for kv, acc in ((k, p_k), (v, p_v)):
            tail_rows = kv.reshape(batch, seq, KV_DIM)[:, seq - keep:]
            acc.append(tail_rows.reshape(batch, keep, N_KV_HEADS, HEAD_DIM))
        tails = u_tail.reshape(batch, seq // tm_p, SUBLANES, cdim)
        p_c.append(tails[:, -1, SUBLANES - (CONV_WIDTH - 1):])

        new_kv = {}

        def attend_sample(q, k, v, l=l):
            attn, k_win, v_win = _attn_sample_call(q, cache_k[l].reshape(t_s, win_buf, KV_DIM),
                                                   cache_v[l].reshape(t_s, win_buf, KV_DIM), k, v, sinks)
            new_kv['k'] = k_win.reshape(t_s, win_buf, N_KV_HEADS, HEAD_DIM)
            new_kv['v'] = v_win.reshape(t_s, win_buf, N_KV_HEADS, HEAD_DIM)
            return attn

        state = (state_conv[l][:, 0], state_conv[l][:, 1])
        _, _, u_s, hq, base_s, lg_s = _mixer_and_post(ys, tab_s, w, tm_s, 1, attend_sample, state, alpha,
                                                      n_tok, hq_p, t_p)
        s_k.append(new_kv['k'])
        s_v.append(new_kv['v'])
        s_c.append(jnp.concatenate([state_conv[l][:, 1:], u_s[:, None]], axis=1))

        route_tile = SUBLANES * LANES
        t_pad = -(-n_tok // route_tile) * route_tile
        logits_t = jnp.concatenate([lg_p, lg_s, jnp.zeros((N_EXPERTS, t_pad - n_tok), F32)], axis=1)
        idx, wts, counts = _route_call(logits_t, router_bias[l], n_tok)
        idx, wts = idx[:, :n_tok], wts[:, :n_tok]
        block_e, n_real, tok, dst, plane = _dispatch_plan(idx, counts, n_tok, MOE_BLOCK_ROWS)
        y8 = _moe_call(block_e, n_real, tok, dst, hq, w_exp_gate[l], w_exp_up[l], w_exp_down[l],
                       TOP_K * plane, MOE_BLOCK_ROWS)
        y8 = y8.reshape(TOP_K, plane * (d // (2 * LANES)), LANES)
        w8 = wts.T
        yp = _final_call(base_p, y8, w8, ln2_g[l][None], ln2_b[l][None], min(tm_p, 2 * LANES), 0)
        ys = _final_call(base_s, y8, w8, ln2_g[l][None], ln2_b[l][None], tm_s, t_p)

    return (yp.reshape(batch, seq, d), ys.reshape(dec_batch, dec_seq, d), jnp.stack(p_k), jnp.stack(p_v),
            jnp.stack(p_c), jnp.stack(s_k), jnp.stack(s_v), jnp.stack(s_c))
```

```python
import functools

import jax
import jax.numpy as jnp
from jax import lax
from jax.experimental import pallas as pl
from jax.experimental.pallas import tpu as pltpu

N_HEADS = 16
N_KV_HEADS = 4
HEAD_DIM = 64
GROUP = N_HEADS // N_KV_HEADS
Q_DIM = N_HEADS * HEAD_DIM
KV_DIM = N_KV_HEADS * HEAD_DIM
ROT_DIM = HEAD_DIM // 4
ROPE_THETA = 500000.0
WINDOW = 128
CONV_WIDTH = 3
PAST_LEN = 16384
N_EXPERTS = 64
N_GROUPS = 8
TOPK_GROUPS = 4
TOP_K = 8
ROUTED_SCALE = 2.5
LN_EPS = 1e-5
MOE_BLOCK_ROWS = 256
MOE_COPY_GROUP = 32

LANES = 128
SUBLANES = 8
VMEM_LIMIT_BYTES = 48 * 1024 * 1024
NEG_BIG = -1e30

F32 = jnp.float32
BF16 = jnp.bfloat16
I32 = jnp.int32


def _tile(n, pref):
    t = pref
    while t > 1 and n % t:
        t //= 2
    return t


def _params(*sem):
    return pltpu.CompilerParams(dimension_semantics=sem, vmem_limit_bytes=VMEM_LIMIT_BYTES)


def _dot(a, b):
    return jnp.dot(a, b, preferred_element_type=F32)


def _dot_nt(a, b):
    return lax.dot_general(a, b, (((1,), (1,)), ((), ())), preferred_element_type=F32)


def _layer_norm(x, g, b):
    mu = jnp.mean(x, axis=-1, keepdims=True)
    xc = x - mu
    var = jnp.mean(xc * xc, axis=-1, keepdims=True)
    return xc * lax.rsqrt(var + LN_EPS) * g + b


def _pack_pair(lo, hi):
    lo32 = lax.bitcast_convert_type(lo.astype(F32), I32)
    hi32 = lax.bitcast_convert_type(hi.astype(F32), I32)
    return lax.shift_right_logical(lo32, 16) | (hi32 & -65536)


def _unpack_pair(p):
    lo = lax.bitcast_convert_type(lax.shift_left(p, 16), F32)
    hi = lax.bitcast_convert_type(p & -65536, F32)
    return lo, hi


def _store_packed_rows(ref, lead, x, n, first_row=0):
    ns = x.shape[1] // (2 * LANES)
    for c in range(ns):
        lo = x[:, 2 * c * LANES:(2 * c + 1) * LANES]
        hi = x[:, (2 * c + 1) * LANES:(2 * c + 2) * LANES]
        ref[lead + (pl.ds(first_row + c, n, stride=ns), slice(None))] = _pack_pair(lo, hi)


def _load_packed_rows(ref, lead, n, ns):
    parts = []
    for c in range(ns):
        parts += list(_unpack_pair(ref[lead + (pl.ds(c, n, stride=ns), slice(None))]))
    return jnp.concatenate(parts, axis=1)


def _rope_tables(pos):
    half = ROT_DIM // 2
    n = pos.shape[0]
    inv_freq = jnp.power(jnp.float32(ROPE_THETA), -jnp.arange(half, dtype=F32) * (2.0 / ROT_DIM))
    ang = pos.astype(F32)[:, None] * inv_freq[None, :]
    cos, sin = jnp.cos(ang), jnp.sin(ang)
    rest = HEAD_DIM - ROT_DIM
    cos_h = jnp.concatenate([cos, cos, jnp.ones((n, rest), F32)], axis=1)
    sa_h = jnp.concatenate([jnp.zeros((n, half), F32), sin, jnp.zeros((n, rest), F32)], axis=1)
    sb_h = jnp.concatenate([-sin, jnp.zeros((n, half + rest), F32)], axis=1)
    rep = LANES // HEAD_DIM
    return tuple(jnp.concatenate([t] * rep, axis=1) for t in (cos_h, sa_h, sb_h))


def _qkv_body(x_ref, w_ref, cos_ref, sa_ref, sb_ref, q_ref, k_ref, v_ref, xb_ref, *, nq):
    j = pl.program_id(1)

    @pl.when(j == 0)
    def _():
        xb_ref[...] = x_ref[...].astype(BF16)

    acc = _dot(xb_ref[...], w_ref[...])
    half = ROT_DIM // 2
    is_q = j < nq
    scale = jnp.where(is_q, jnp.float32(HEAD_DIM ** -0.5), jnp.float32(1.0))
    cos, sa, sb = cos_ref[...] * scale, sa_ref[...] * scale, sb_ref[...] * scale
    outs = []
    for c in range(acc.shape[1] // LANES):
        blk = acc[:, c * LANES:(c + 1) * LANES]
        roped = blk * cos + pltpu.roll(blk, half, 1) * sa + pltpu.roll(blk, LANES - half, 1) * sb
        outs.append(roped if c * LANES < KV_DIM else jnp.where(is_q, roped, blk))
    out = jnp.concatenate(outs, axis=1)

    @pl.when(is_q)
    def _():
        q_ref[...] = out.astype(BF16)

    @pl.when(j == nq)
    def _():
        k_ref[...] = out[:, :KV_DIM]
        v_ref[...] = out[:, KV_DIM:]


def _qkv_call(x, w_qkv, tables, tm):
    t, d = x.shape
    tn = 2 * KV_DIM
    assert Q_DIM % tn == 0
    nq = Q_DIM // tn
    tab_blocks = tables[0].shape[0] // tm
    tab_spec = pl.BlockSpec((tm, LANES), lambda i, j: (i % tab_blocks, 0))
    return pl.pallas_call(
        functools.partial(_qkv_body, nq=nq),
        grid=(t // tm, nq + 1),
        in_specs=[
            pl.BlockSpec((tm, d), lambda i, j: (i, 0)),
            pl.BlockSpec((d, tn), lambda i, j: (0, j)),
            tab_spec, tab_spec, tab_spec,
        ],
        out_specs=[
            pl.BlockSpec((tm, tn), lambda i, j: (i, jnp.minimum(j, nq - 1))),
            pl.BlockSpec((tm, KV_DIM), lambda i, j: (i, 0)),
            pl.BlockSpec((tm, KV_DIM), lambda i, j: (i, 0)),
        ],
        out_shape=[
            jax.ShapeDtypeStruct((t, Q_DIM), BF16),
            jax.ShapeDtypeStruct((t, KV_DIM), F32),
            jax.ShapeDtypeStruct((t, KV_DIM), F32),
        ],
        scratch_shapes=[pltpu.VMEM((tm, d), BF16)],
        compiler_params=_params("parallel", "arbitrary"),
        name="qkv",
    )(x, w_qkv, *tables)


def _head_pair_operands(kv_chunk, odd):
    lane = lax.broadcasted_iota(I32, kv_chunk.shape, 1)
    own = jnp.where((lane >= HEAD_DIM) == odd, kv_chunk, 0.0)
    other = pltpu.roll(own, HEAD_DIM, 1)
    lo, hi = (other, own) if odd else (own, other)
    return lo.astype(BF16), hi.astype(BF16)


def _attend(q, kk, vv, valid, sinks_ref):
    heads = []
    for kh in range(N_KV_HEADS):
        c = (kh * HEAD_DIM) // LANES
        odd = bool((kh * HEAD_DIM) % LANES)
        k_ops = _head_pair_operands(kk[:, c * LANES:(c + 1) * LANES], odd)
        v_ops = _head_pair_operands(vv[:, c * LANES:(c + 1) * LANES], odd)
        for g in range(GROUP):
            h = kh * GROUP + g
            heads.append((h // 2, h, k_ops[h % 2], v_ops[h % 2]))
    scores = [jnp.where(valid, _dot_nt(q[:, ch * LANES:(ch + 1) * LANES], k_op), NEG_BIG)
              for ch, _, k_op, _ in heads]
    maxes = [jnp.maximum(jnp.max(s, axis=-1, keepdims=True), sinks_ref[h]) for s, (_, h, _, _) in zip(scores, heads)]
    probs = [jnp.exp(s - m) for s, m in zip(scores, maxes)]
    dens = [jnp.sum(p, axis=-1, keepdims=True) + jnp.exp(sinks_ref[h] - m)
            for p, m, (_, h, _, _) in zip(probs, maxes, heads)]
    outs = [_dot((p / den).astype(BF16), v_op) for p, den, (_, _, _, v_op) in zip(probs, dens, heads)]
    return jnp.concatenate([outs[2 * j] + outs[2 * j + 1] for j in range(N_HEADS // 2)], axis=1)


def _attn_prompt_body(sinks_ref, q_ref, kp_ref, kc_ref, vp_ref, vc_ref, o_ref):
    n = pl.program_id(1)
    w = WINDOW
    kk = jnp.concatenate([kp_ref[...], kc_ref[...]], axis=0)
    vv = jnp.concatenate([vp_ref[...], vc_ref[...]], axis=0)
    a = lax.broadcasted_iota(I32, (w, 2 * w), 0)
    c = lax.broadcasted_iota(I32, (w, 2 * w), 1)
    valid = (c > a) & (c <= a + w) & ((n > 0) | (c >= w))
    o_ref[...] = _attend(q_ref[...], kk, vv, valid, sinks_ref).astype(BF16)


def _attn_prompt_call(q, k, v, sinks, batch, seq):
    w = WINDOW
    nb = seq // w
    cur = lambda b, n: (b * nb + n, 0)
    prev = lambda b, n: (b * nb + jnp.maximum(n - 1, 0), 0)
    return pl.pallas_call(
        _attn_prompt_body,
        grid=(batch, nb),
        in_specs=[
            pl.BlockSpec(memory_space=pltpu.SMEM),
            pl.BlockSpec((w, Q_DIM), cur),
            pl.BlockSpec((w, KV_DIM), prev),
            pl.BlockSpec((w, KV_DIM), cur),
            pl.BlockSpec((w, KV_DIM), prev),
            pl.BlockSpec((w, KV_DIM), cur),
        ],
        out_specs=pl.BlockSpec((w, Q_DIM), cur),
        out_shape=jax.ShapeDtypeStruct((batch * seq, Q_DIM), BF16),
        compiler_params=_params("parallel", "parallel"),
        name="attn_prompt",
    )(sinks, q, k, k, v, v)


def _attn_sample_body(sinks_ref, q_ref, kc_ref, vc_ref, kn_ref, vn_ref, o_ref, kw_ref, vw_ref, *, bt, nkeys):
    n = bt * nkeys
    key_i = jnp.concatenate([lax.broadcasted_iota(I32, (nkeys, KV_DIM), 0)] * bt, axis=0)

    def window(cache_ref, new_ref):
        shifted = pltpu.roll(cache_ref[...].reshape(n, KV_DIM), n - 1, 0)
        new = jnp.broadcast_to(new_ref[...][:, None, :], (bt, nkeys, KV_DIM)).reshape(n, KV_DIM)
        return jnp.where(key_i == nkeys - 1, new, shifted)

    kk, vv = window(kc_ref, kn_ref), window(vc_ref, vn_ref)
    kw_ref[...] = kk.reshape(bt, nkeys, KV_DIM)
    vw_ref[...] = vv.reshape(bt, nkeys, KV_DIM)
    row_b = lax.broadcasted_iota(I32, (bt, n), 0)
    key_b = jnp.concatenate([jnp.full((bt, nkeys), b, I32) for b in range(bt)], axis=1)
    o_ref[...] = _attend(q_ref[...], kk, vv, row_b == key_b, sinks_ref).astype(BF16)


def _attn_sample_call(q, k_cache, v_cache, k_new, v_new, sinks):
    b, nkeys, _ = k_cache.shape
    bt = _tile(b, SUBLANES)
    cache = pl.BlockSpec((bt, nkeys, KV_DIM), lambda i: (i, 0, 0))
    new = pl.BlockSpec((bt, KV_DIM), lambda i: (i, 0))
    return pl.pallas_call(
        functools.partial(_attn_sample_body, bt=bt, nkeys=nkeys),
        grid=(b // bt,),
        in_specs=[pl.BlockSpec(memory_space=pltpu.SMEM), pl.BlockSpec((bt, Q_DIM), lambda i: (i, 0)),
                  cache, cache, new, new],
        out_specs=[pl.BlockSpec((bt, Q_DIM), lambda i: (i, 0)), cache, cache],
        out_shape=[jax.ShapeDtypeStruct((b, Q_DIM), BF16), jax.ShapeDtypeStruct(k_cache.shape, F32),
                   jax.ShapeDtypeStruct(v_cache.shape, F32)],
        compiler_params=_params("parallel"),
        name="attn_sample",
    )(sinks, q, k_cache, v_cache, k_new, v_new)


def _conv_body(*refs, decode, tiles_per_seq, tm, tail):
    if decode:
        x_ref, wb_ref, wc_ref, wh_ref, cw_ref, s0_ref, s1_ref, cb_ref, ut_ref, xb_ref = refs
    else:
        x_ref, wb_ref, wc_ref, wh_ref, cw_ref, cb_ref, ut_ref, xb_ref, carry_ref = refs
    i = pl.program_id(0)
    c = pl.program_id(1)

    @pl.when(c == 0)
    def _():
        xb_ref[...] = x_ref[...].astype(BF16)

    if not decode:
        @pl.when(i % tiles_per_seq == 0)
        def _():
            carry_ref[c] = jnp.zeros(carry_ref.shape[1:], F32)

    xb = xb_ref[...]
    tc = wb_ref.shape[1]
    sub = min(tc, 2 * LANES)
    cols = [slice(k * sub, (k + 1) * sub) for k in range(tc // sub)]
    b_gs = [_dot(xb, wb_ref[:, s]) for s in cols]
    us = [_dot(xb, wc_ref[:, s]) * _dot(xb, wh_ref[:, s]) for s in cols]
    cw = cw_ref[...]
    for s, b_g, u in zip(cols, b_gs, us):
        if decode:
            u_m1, u_m2 = s1_ref[:, s], s0_ref[:, s]
        else:
            prev = carry_ref[c]
            p_m2, p_m1 = prev[SUBLANES - 2:SUBLANES - 1, s], prev[SUBLANES - 1:SUBLANES, s]
            r = lax.broadcasted_iota(I32, u.shape, 0)
            u_m1 = jnp.where(r == 0, p_m1, pltpu.roll(u, 1, 0))
            u_m2 = jnp.where(r == 0, p_m2, jnp.where(r == 1, p_m1, pltpu.roll(u, 2, 0)))
        conv = cw[0:1, s] * u_m2 + cw[1:2, s] * u_m1 + cw[2:3, s] * u
        cb_ref[:, s] = (b_g * conv).astype(BF16)
        ut_ref[:, s] = u[tm - tail:]
    if not decode:
        carry_ref[c] = jnp.concatenate([u[tm - SUBLANES:] for u in us], axis=1)


def _conv_call(x, w_b, w_c, w_h, conv_w, tm, seq_rows, state=None):
    t, d = x.shape
    cdim = w_b.shape[1]
    tc = _tile(cdim, 4 * LANES)
    decode = state is not None
    tail = tm if decode else SUBLANES
    w_spec = pl.BlockSpec((d, tc), lambda i, c: (0, c))
    in_specs = [pl.BlockSpec((tm, d), lambda i, c: (i, 0)), w_spec, w_spec, w_spec,
                pl.BlockSpec((CONV_WIDTH, tc), lambda i, c: (0, c))]
    args = [x, w_b, w_c, w_h, conv_w]
    scratch = [pltpu.VMEM((tm, d), BF16)]
    if decode:
        in_specs += [pl.BlockSpec((tm, tc), lambda i, c: (i, c))] * 2
        args += list(state)
    else:
        scratch.append(pltpu.VMEM((cdim // tc, SUBLANES, tc), F32))
    return pl.pallas_call(
        functools.partial(_conv_body, decode=decode, tiles_per_seq=max(seq_rows // tm, 1), tm=tm, tail=tail),
        grid=(t // tm, cdim // tc),
        in_specs=in_specs,
        out_specs=[pl.BlockSpec((tm, tc), lambda i, c: (i, c)), pl.BlockSpec((tail, tc), lambda i, c: (i, c))],
        out_shape=[jax.ShapeDtypeStruct((t, cdim), BF16), jax.ShapeDtypeStruct((t // tm * tail, cdim), F32)],
        scratch_shapes=scratch,
        compiler_params=_params("arbitrary", "arbitrary"),
        name="conv",
    )(*args)


def _gate_body(x_ref, at_ref, cb_ref, wga_ref, wgc_ref, wa_ref, wco_ref, o_ref, xb_ref):
    @pl.when(pl.program_id(1) == 0)
    def _():
        xb_ref[...] = x_ref[...].astype(BF16)

    xb = xb_ref[...]
    g_a = _dot(xb, wga_ref[...])
    g_c = _dot(xb, wgc_ref[...])
    a = _dot(at_ref[...], wa_ref[...])
    c = _dot(cb_ref[...], wco_ref[...])
    o_ref[...] = (jax.nn.sigmoid(g_a) * a + jax.nn.sigmoid(g_c) * c).astype(BF16)


def _gate_call(x, attn, cb, w_ga, w_gc, w_a, w_co, tm):
    t, d = x.shape
    tn = _tile(d, 4 * LANES)
    row = lambda i, n: (i, 0)
    col = lambda i, n: (0, n)
    return pl.pallas_call(
        _gate_body,
        grid=(t // tm, d // tn),
        in_specs=[
            pl.BlockSpec((tm, d), row),
            pl.BlockSpec((tm, attn.shape[1]), row),
            pl.BlockSpec((tm, cb.shape[1]), row),
            pl.BlockSpec((d, tn), col),
            pl.BlockSpec((d, tn), col),
            pl.BlockSpec((w_a.shape[0], tn), col),
            pl.BlockSpec((w_co.shape[0], tn), col),
        ],
        out_specs=pl.BlockSpec((tm, tn), lambda i, n: (i, n)),
        out_shape=jax.ShapeDtypeStruct((t, d), BF16),
        scratch_shapes=[pltpu.VMEM((tm, d), BF16)],
        compiler_params=_params("parallel", "arbitrary"),
        name="gate",
    )(x, attn, cb, w_ga, w_gc, w_a, w_co)


def _post_body(x_ref, mp_ref, wo_ref, g_ref, b_ref, wr_ref, wsg_ref, wsu_ref, wsd_ref, *rest, alpha):
    hq_ref, base_ref, lg_ref = rest[-3:]
    tm = x_ref.shape[0]
    halves = 2 if tm % (2 * LANES) == 0 else 1
    hr = tm // halves
    ns = x_ref.shape[1] // (2 * LANES)
    parts = [slice(i * hr, (i + 1) * hr) for i in range(halves)]
    mixed = [_dot(mp_ref[p, :], wo_ref[...]) for p in parts]
    hs = [_layer_norm(alpha * x_ref[p, :] + m, g_ref[...], b_ref[...]) for p, m in zip(parts, mixed)]
    hbs = [h.astype(BF16) for h in hs]
    for p, hb in zip(parts, hbs):
        lg_ref[:, p] = _dot_nt(wr_ref[...], hb)
    gates = [_dot(hb, wsg_ref[...]) for hb in hbs]
    ups = [_dot(hb, wsu_ref[...]) for hb in hbs]
    acts = [(jax.nn.silu(g) * u).astype(BF16) for g, u in zip(gates, ups)]
    for p, h, a in zip(parts, hs, acts):
        base_ref[p, :] = alpha * h + _dot(a, wsd_ref[...])
    for i, hb in enumerate(hbs):
        _store_packed_rows(hq_ref, (), hb, hr, first_row=i * hr * ns)


def _post_call(x, mp, w_o, ln_g, ln_b, w_r_t, w_sg, w_su, w_sd, tm, alpha, hq_tokens, hq_buf, hq_offset):
    t, d = x.shape
    n_e, f = w_r_t.shape[0], w_sg.shape[1]
    ns = d // (2 * LANES)
    off = hq_offset // tm
    row = lambda i: (i, 0)
    whole = lambda shape: pl.BlockSpec(shape, lambda i: (0, 0), pipeline_mode=pl.Buffered(1))
    in_specs = [
        pl.BlockSpec((tm, d), row),
        pl.BlockSpec((tm, d), row),
        whole((d, d)), whole((1, d)), whole((1, d)), whole((n_e, d)), whole((d, f)), whole((d, f)), whole((f, d)),
    ]
    args = [x, mp, w_o, ln_g, ln_b, w_r_t, w_sg, w_su, w_sd]
    aliases = {}
    if hq_buf is not None:
        aliases = {len(args): 0}
        in_specs.append(pl.BlockSpec(memory_space=pl.ANY))
        args.append(hq_buf)
    return pl.pallas_call(
        functools.partial(_post_body, alpha=alpha),
        grid=(t // tm,),
        in_specs=in_specs,
        out_specs=[
            pl.BlockSpec((tm * ns, LANES), lambda i: (i + off, 0)),
            pl.BlockSpec((tm, d), row),
            pl.BlockSpec((n_e, tm), lambda i: (0, i)),
        ],
        out_shape=[
            jax.ShapeDtypeStruct((hq_tokens * ns, LANES), I32),
            jax.ShapeDtypeStruct((t, d), F32),
            jax.ShapeDtypeStruct((n_e, t), F32),
        ],
        input_output_aliases=aliases,
        compiler_params=_params("parallel"),
        name="post",
    )(*args)


def _route_body(bias_ref, lg_ref, idx_ref, w_ref, cnt_ref, *, n_tok):
    per_group = N_EXPERTS // N_GROUPS
    neg_inf = jnp.float32(-jnp.inf)
    i = pl.program_id(0)
    scores = [jax.nn.sigmoid(lg_ref[e]) for e in range(N_EXPERTS)]
    choice = [scores[e] + bias_ref[e] for e in range(N_EXPERTS)]

    group_score = []
    for g in range(N_GROUPS):
        vals = choice[g * per_group:(g + 1) * per_group]
        m1 = functools.reduce(jnp.maximum, vals)
        m2 = jnp.full_like(m1, neg_inf)
        found = jnp.zeros(m1.shape, jnp.bool_)
        for v in vals:
            eq = v == m1
            m2 = jnp.maximum(m2, jnp.where(eq & ~found, neg_inf, v))
            found = found | eq
        group_score.append(m1 + m2)

    masked = []
    for g in range(N_GROUPS):
        ahead = jnp.zeros(group_score[g].shape, I32)
        for o in range(N_GROUPS):
            if o == g:
                continue
            beats = group_score[o] > group_score[g]
            if o < g:
                beats = beats | (group_score[o] == group_score[g])
            ahead = ahead + beats.astype(I32)
        keep = ahead < TOPK_GROUPS
        masked += [jnp.where(keep, choice[e], neg_inf) for e in range(g * per_group, (g + 1) * per_group)]

    shape = masked[0].shape
    token = (i * SUBLANES + lax.broadcasted_iota(I32, shape, 0)) * LANES + lax.broadcasted_iota(I32, shape, 1)
    real = (token < n_tok).astype(I32)
    chosen = [jnp.zeros(shape, I32) for _ in range(N_EXPERTS)]
    picked_w = []
    for r in range(TOP_K):
        best = functools.reduce(jnp.maximum, masked)
        sel = jnp.full(shape, N_EXPERTS, I32)
        for e in reversed(range(N_EXPERTS)):
            sel = jnp.where(masked[e] == best, e, sel)
        w = jnp.zeros(shape, F32)
        for e in range(N_EXPERTS):
            hit = sel == e
            w = jnp.where(hit, scores[e], w)
            masked[e] = jnp.where(hit, neg_inf, masked[e])
            chosen[e] = jnp.where(hit, real, chosen[e])
        idx_ref[r] = sel
        picked_w.append(w)
    total = functools.reduce(lambda a, b: a + b, picked_w)
    for r in range(TOP_K):
        w_ref[r] = picked_w[r] / total * ROUTED_SCALE

    @pl.when(i == 0)
    def _():
        cnt_ref[...] = jnp.zeros(cnt_ref.shape, I32)

    for e in range(N_EXPERTS):
        cnt_ref[e] += chosen[e]


def _route_call(logits_t, bias, n_tok):
    n_e, t = logits_t.shape
    rows = t // LANES
    lg3 = logits_t.reshape(n_e, rows, LANES)
    pick = pl.BlockSpec((TOP_K, SUBLANES, LANES), lambda i: (0, i, 0))
    idx, w, cnt = pl.pallas_call(
        functools.partial(_route_body, n_tok=n_tok),
        grid=(rows // SUBLANES,),
        in_specs=[
            pl.BlockSpec(memory_space=pltpu.SMEM),
            pl.BlockSpec((n_e, SUBLANES, LANES), lambda i: (0, i, 0)),
        ],
        out_specs=[pick, pick, pl.BlockSpec((n_e, SUBLANES, LANES), lambda i: (0, 0, 0))],
        out_shape=[
            jax.ShapeDtypeStruct((TOP_K, rows, LANES), I32),
            jax.ShapeDtypeStruct((TOP_K, rows, LANES), F32),
            jax.ShapeDtypeStruct((n_e, SUBLANES, LANES), I32),
        ],
        compiler_params=_params("arbitrary"),
        name="route",
    )(bias, lg3)
    return idx.reshape(TOP_K, t), w.reshape(TOP_K, t), jnp.sum(cnt, axis=(1, 2))


def _moe_body(be_ref, nreal_ref, tok0_ref, tokn_ref, dstp_ref, hq_hbm, wg_ref, wu_ref, wd_ref, out_hbm,
              xb0, xb1, yb0, yb1, wgb, wub, wdb, gsem, ssem, *, rows):
    b = pl.program_id(0)
    nb = pl.num_programs(0)
    xbufs, ybufs = (xb0, xb1), (yb0, yb1)
    ns = xb0.shape[0] // rows
    grp = min(MOE_COPY_GROUP, rows)

    def n_real(k):
        return jnp.where((k >= 0) & (k < nb), nreal_ref[jnp.clip(k, 0, nb - 1)], 0)

    def token(ref, i):
        return ref.at[pl.ds(pl.multiple_of(i * ns, ns), ns), :]

    def start_groups(make_copy, n, priority):
        for r in range(rows):
            @pl.when(n > (r // grp) * grp)
            def _(r=r):
                make_copy(r).start(priority=priority)

    def wait_groups(src, dst, sem, n):
        for g in range(rows // grp):
            @pl.when(n > g * grp)
            def _(g=g):
                part = pl.ds(g * grp * ns, grp * ns)
                pltpu.make_async_copy(src.at[part, :], dst.at[part, :], sem).wait()

    def gather(idx_ref, dst_buf, sem, n):
        start_groups(lambda r: pltpu.make_async_copy(token(hq_hbm, idx_ref[0, 0, r]), token(dst_buf, r), sem), n, 0)

    @pl.when(b == 0)
    def _():
        xb0[...] = jnp.zeros(xb0.shape, I32)
        xb1[...] = jnp.zeros(xb1.shape, I32)
        gather(tok0_ref, xb0, gsem.at[0], n_real(0))

    @pl.when((b == 0) | (be_ref[b] != be_ref[jnp.maximum(b - 1, 0)]))
    def _():
        wgb[...] = wg_ref[...].astype(BF16)
        wub[...] = wu_ref[...].astype(BF16)
        wdb[...] = wd_ref[...].astype(BF16)

    def scatter_previous(o):
        start_groups(lambda r: pltpu.make_async_copy(token(ybufs[o], r), token(out_hbm, dstp_ref[0, 0, r]),
                                                     ssem.at[o]), n_real(b - 1), 1)

    def block(s):
        o = 1 - s
        wait_groups(hq_hbm, xbufs[s], gsem.at[s], n_real(b))
        gather(tokn_ref, xbufs[o], gsem.at[o], n_real(b + 1))
        scatter_previous(o)
        xb = _load_packed_rows(xbufs[s], (), rows, ns).astype(BF16)
        act = jax.nn.silu(_dot(xb, wgb[...])) * _dot(xb, wub[...])
        y = _dot(act.astype(BF16), wdb[...]).astype(BF16)
        wait_groups(ybufs[s], out_hbm, ssem.at[s], n_real(b - 2))
        _store_packed_rows(ybufs[s], (), y, rows)

    def empty_block(s):
        scatter_previous(1 - s)
        wait_groups(ybufs[s], out_hbm, ssem.at[s], n_real(b - 2))

    for s in range(2):
        @pl.when((b % 2 == s) & (n_real(b) > 0))
        def _(s=s):
            block(s)

        @pl.when((b % 2 == s) & (n_real(b) == 0))
        def _(s=s):
            empty_block(s)

    @pl.when(b == nb - 1)
    def _():
        wait_groups(yb0, out_hbm, ssem.at[0], n_real(b - 1))


def _moe_call(block_e, n_real, tok, dst, hq, w_gate, w_up, w_down, out_rows, rows):
    n_blocks = block_e.shape[0]
    assert n_blocks % 2 == 0 and rows % min(MOE_COPY_GROUP, rows) == 0
    n_e, d, f = w_gate.shape
    ns = d // (2 * LANES)
    shifted = lambda off: pl.BlockSpec(
        (1, 1, rows), lambda b, be, nr: (jnp.clip(b + off, 0, n_blocks - 1), 0, 0), memory_space=pltpu.SMEM)
    expert = lambda b, be, nr: (be[b], 0, 0)
    return pl.pallas_call(
        functools.partial(_moe_body, rows=rows),
        grid_spec=pltpu.PrefetchScalarGridSpec(
            num_scalar_prefetch=2,
            grid=(n_blocks,),
            in_specs=[
                pl.BlockSpec((1, 1, rows), lambda b, be, nr: (0, 0, 0), memory_space=pltpu.SMEM),
                shifted(1), shifted(-1),
                pl.BlockSpec(memory_space=pl.ANY),
                pl.BlockSpec((None, d, f), expert),
                pl.BlockSpec((None, d, f), expert),
                pl.BlockSpec((None, f, d), expert),
            ],
            out_specs=pl.BlockSpec(memory_space=pl.ANY),
            scratch_shapes=[pltpu.VMEM((rows * ns, LANES), I32)] * 4 + [
                pltpu.VMEM((d, f), BF16),
                pltpu.VMEM((d, f), BF16),
                pltpu.VMEM((f, d), BF16),
                pltpu.SemaphoreType.DMA((2,)),
                pltpu.SemaphoreType.DMA((2,)),
            ],
        ),
        out_shape=jax.ShapeDtypeStruct((out_rows * ns, LANES), I32),
        compiler_params=_params("arbitrary"),
        name="moe",
    )(block_e, n_real, tok, tok, dst, hq, w_gate, w_up, w_down)


def _dispatch_plan(idx, counts, n_tok, rows):
    m = n_tok * TOP_K
    n_real = (m + N_EXPERTS * (rows - 1) + rows - 1) // rows
    n_blocks = n_real + 1 + (n_real + 1) % 2
    p = n_blocks * rows
    n_spare = p - m
    plane = -(-(n_tok + -(-n_spare // TOP_K)) // SUBLANES) * SUBLANES
    flat_e = idx.T.reshape(m)
    order = jnp.argsort(flat_e).astype(I32)
    padded = (counts + rows - 1) // rows * rows
    pad_end = jnp.cumsum(padded)
    pad_start = pad_end - padded
    start = jnp.cumsum(counts) - counts
    block_first = jnp.arange(n_blocks, dtype=I32) * rows
    block_e = jnp.minimum(jnp.sum((pad_end[None, :] <= block_first[:, None]).astype(I32), axis=1), N_EXPERTS - 1)
    slot = block_first[:, None] + jnp.arange(rows, dtype=I32)[None, :]
    own = block_e[:, None] == jnp.arange(N_EXPERTS, dtype=I32)[None, :]
    lookup = lambda table: jnp.sum(jnp.where(own, table[None, :], 0), axis=1, keepdims=True)
    q = slot - lookup(pad_start)
    cnt_b = lookup(counts)
    start_b = lookup(start)
    real = q < cnt_b
    assign = order[jnp.clip(start_b + q, 0, m - 1)]
    t_of, j_of = assign // TOP_K, assign % TOP_K
    spare_rank = slot - (start_b + jnp.minimum(q, cnt_b))
    tok = jnp.where(real, t_of, 0)
    dst = jnp.where(real, j_of * plane + t_of, (spare_rank % TOP_K) * plane + n_tok + spare_rank // TOP_K)
    n_real_rows = jnp.clip(cnt_b[:, 0] - q[:, 0], 0, rows).astype(I32)
    shape = (n_blocks, 1, rows)
    return block_e.astype(I32), n_real_rows, tok.reshape(shape), dst.reshape(shape), plane


def _final_body(base_ref, y8_ref, w_ref, g_ref, b_ref, o_ref):
    w = w_ref[...]
    ffn = None
    tm = w.shape[0]
    ns = y8_ref.shape[1] // tm
    for j in range(TOP_K):
        yj = _load_packed_rows(y8_ref, (j,), tm, ns) * w[:, j:j + 1]
        ffn = yj if ffn is None else ffn + yj
    o_ref[...] = _layer_norm(base_ref[...] + ffn, g_ref[...], b_ref[...])


def _final_call(base, y8, w8, ln_g, ln_b, tm, row_offset):
    t, d = base.shape
    off = row_offset // tm
    ns = d // (2 * LANES)
    return pl.pallas_call(
        _final_body,
        grid=(t // tm,),
        in_specs=[
            pl.BlockSpec((tm, d), lambda i: (i, 0)),
            pl.BlockSpec((TOP_K, tm * ns, LANES), lambda i: (0, i + off, 0)),
            pl.BlockSpec((tm, TOP_K), lambda i: (i + off, 0)),
            pl.BlockSpec((1, d), lambda i: (0, 0)),
            pl.BlockSpec((1, d), lambda i: (0, 0)),
        ],
        out_specs=pl.BlockSpec((tm, d), lambda i: (i, 0)),
        out_shape=jax.ShapeDtypeStruct((t, d), F32),
        compiler_params=_params("parallel"),
        name="final",
    )(base, y8, w8, ln_g, ln_b)


def _mixer_and_post(x, pos_tables, w, tm, seq_rows, attend, conv_state, alpha, hq_tokens, hq_buf, hq_offset):
    tm_qkv = 2 * tm if pos_tables[0].shape[0] % (2 * tm) == 0 else tm
    q, k, v = _qkv_call(x, w['qkv'], pos_tables, tm_qkv)
    attn = attend(q, k, v)
    cb, u_tail = _conv_call(x, w['b'], w['c'], w['h'], w['conv'], tm, seq_rows, conv_state)
    mp = _gate_call(x, attn, cb, w['ga'], w['gc'], w['attn_out'], w['conv_out'], tm)
    hq, base, logits_t = _post_call(x, mp, w['o'], w['ln1_g'], w['ln1_b'], w['router_t'],
                                    w['sh_gate'], w['sh_up'], w['sh_down'], tm, alpha,
                                    hq_tokens, hq_buf, hq_offset)
    return k, v, u_tail, hq, base, logits_t


def kernel(x_prompt, x_sample, cache_k, cache_v, state_conv, w_in, attn_sinks, conv_w, w_attn_out, w_conv_out, w_o, ln1_g, ln1_b, w_router, router_bias, w_exp_gate, w_exp_up, w_exp_down, w_sh_gate, w_sh_up, w_sh_down, ln2_g, ln2_b):
    depth, d, _ = w_in.shape
    batch, seq, _ = x_prompt.shape
    dec_batch, dec_seq, _ = x_sample.shape
    win_buf = cache_k.shape[2]
    cdim = conv_w.shape[2]
    assert dec_seq == 1 and win_buf == WINDOW and seq % WINDOW == 0
    alpha = (2 * depth) ** 0.25
    t_p, t_s = batch * seq, dec_batch * dec_seq
    tm_p, tm_s = _tile(seq, 4 * LANES), _tile(t_s, LANES)
    assert t_p % tm_s == 0 and t_s % tm_s == 0
    tab_p = _rope_tables(jnp.arange(seq))
    tab_s = _rope_tables(jnp.full((tm_s,), PAST_LEN, I32))

    yp = x_prompt.reshape(t_p, d)
    ys = x_sample.reshape(t_s, d)
    p_k, p_v, p_c, s_k, s_v, s_c = [], [], [], [], [], []
    for l in range(depth):
        wl = w_in[l].astype(BF16)
        o = 0
        w = {}
        for name, width in (('qkv', Q_DIM + 2 * KV_DIM), ('b', cdim), ('c', cdim), ('h', cdim), ('ga', d), ('gc', d)):
            w[name] = wl[:, o:o + width]
            o += width
        w.update(
            conv=conv_w[l], attn_out=w_attn_out[l].astype(BF16), conv_out=w_conv_out[l].astype(BF16),
            o=w_o[l].astype(BF16), ln1_g=ln1_g[l][None], ln1_b=ln1_b[l][None],
            router_t=w_router[l].T.astype(BF16), sh_gate=w_sh_gate[l].astype(BF16),
            sh_up=w_sh_up[l].astype(BF16), sh_down=w_sh_down[l].astype(BF16))
        sinks = attn_sinks[l]

        n_tok = t_p + t_s
        k, v, u_tail, hq_p, base_p, lg_p = _mixer_and_post(
            yp, tab_p, w, tm_p, seq,
            lambda q, k, v: _attn_prompt_call(q, k, v, sinks, batch, seq), None, alpha, n_tok, None, 0)
        keep = min(WINDOW, seq)
        for kv, acc in ((k, p_k), (v, p_v)):
            tail_rows = kv.reshape(batch, seq, KV_DIM)[:, seq - keep:]
            acc.append(tail_rows.reshape(batch, keep, N_KV_HEADS, HEAD_DIM))
        tails = u_tail.reshape(batch, seq // tm_p, SUBLANES, cdim)
        p_c.append(tails[:, -1, SUBLANES - (CONV_WIDTH - 1):])

        new_kv = {}

        def attend_sample(q, k, v, l=l):
            attn, k_win, v_win = _attn_sample_call(q, cache_k[l].reshape(t_s, win_buf, KV_DIM),
                                                   cache_v[l].reshape(t_s, win_buf, KV_DIM), k, v, sinks)
            new_kv['k'] = k_win.reshape(t_s, win_buf, N_KV_HEADS, HEAD_DIM)
            new_kv['v'] = v_win.reshape(t_s, win_buf, N_KV_HEADS, HEAD_DIM)
            return attn

        state = (state_conv[l][:, 0], state_conv[l][:, 1])
        _, _, u_s, hq, base_s, lg_s = _mixer_and_post(ys, tab_s, w, tm_s, 1, attend_sample, state, alpha,
                                                      n_tok, hq_p, t_p)
        s_k.append(new_kv['k'])
        s_v.append(new_kv['v'])
        s_c.append(jnp.concatenate([state_conv[l][:, 1:], u_s[:, None]], axis=1))

        route_tile = SUBLANES * LANES
        t_pad = -(-n_tok // route_tile) * route_tile
        logits_t = jnp.concatenate([lg_p, lg_s, jnp.zeros((N_EXPERTS, t_pad - n_tok), F32)], axis=1)
        idx, wts, counts = _route_call(logits_t, router_bias[l], n_tok)
        idx, wts = idx[:, :n_tok], wts[:, :n_tok]
        block_e, n_real, tok, dst, plane = _dispatch_plan(idx, counts, n_tok, MOE_BLOCK_ROWS)
        y8 = _moe_call(block_e, n_real, tok, dst, hq, w_exp_gate[l], w_exp_up[l], w_exp_down[l],
                       TOP_K * plane, MOE_BLOCK_ROWS)
        y8 = y8.reshape(TOP_K, plane * (d // (2 * LANES)), LANES)
        w8 = wts.T
        yp = _final_call(base_p, y8, w8, ln2_g[l][None], ln2_b[l][None], min(tm_p, 2 * LANES), 0)
        ys = _final_call(base_s, y8, w8, ln2_g[l][None], ln2_b[l][None], tm_s, t_p)

    return (yp.reshape(batch, seq, d), ys.reshape(dec_batch, dec_seq, d), jnp.stack(p_k), jnp.stack(p_v),
            jnp.stack(p_c), jnp.stack(s_k), jnp.stack(s_v), jnp.stack(s_c))
```

```python
import functools

import jax
import jax.numpy as jnp
from jax import lax
from jax.experimental import pallas as pl
from jax.experimental.pallas import tpu as pltpu

N_HEADS = 16
N_KV_HEADS = 4
HEAD_DIM = 64
GROUP = N_HEADS // N_KV_HEADS
Q_DIM = N_HEADS * HEAD_DIM
KV_DIM = N_KV_HEADS * HEAD_DIM
ROT_DIM = HEAD_DIM // 4
ROPE_THETA = 500000.0
WINDOW = 128
CONV_WIDTH = 3
PAST_LEN = 16384
N_EXPERTS = 64
N_GROUPS = 8
TOPK_GROUPS = 4
TOP_K = 8
ROUTED_SCALE = 2.5
LN_EPS = 1e-5
MOE_BLOCK_ROWS = 256
MOE_COPY_GROUP = 32

LANES = 128
SUBLANES = 8
VMEM_LIMIT_BYTES = 48 * 1024 * 1024
NEG_BIG = -1e30

F32 = jnp.float32
BF16 = jnp.bfloat16
I32 = jnp.int32


def _tile(n, pref):
    t = pref
    while t > 1 and n % t:
        t //= 2
    return t


def _params(*sem):
    return pltpu.CompilerParams(dimension_semantics=sem, vmem_limit_bytes=VMEM_LIMIT_BYTES)


def _dot(a, b):
    return jnp.dot(a, b, preferred_element_type=F32)


def _dot_nt(a, b):
    return lax.dot_general(a, b, (((1,), (1,)), ((), ())), preferred_element_type=F32)


def _layer_norm(x, g, b):
    mu = jnp.mean(x, axis=-1, keepdims=True)
    xc = x - mu
    var = jnp.mean(xc * xc, axis=-1, keepdims=True)
    return xc * lax.rsqrt(var + LN_EPS) * g + b


def _pack_pair(lo, hi):
    lo32 = lax.bitcast_convert_type(lo.astype(F32), I32)
    hi32 = lax.bitcast_convert_type(hi.astype(F32), I32)
    return lax.shift_right_logical(lo32, 16) | (hi32 & -65536)


def _unpack_pair(p):
    lo = lax.bitcast_convert_type(lax.shift_left(p, 16), F32)
    hi = lax.bitcast_convert_type(p & -65536, F32)
    return lo, hi


def _store_packed_rows(ref, lead, x, n, first_row=0):
    ns = x.shape[1] // (2 * LANES)
    for c in range(ns):
        lo = x[:, 2 * c * LANES:(2 * c + 1) * LANES]
        hi = x[:, (2 * c + 1) * LANES:(2 * c + 2) * LANES]
        ref[lead + (pl.ds(first_row + c, n, stride=ns), slice(None))] = _pack_pair(lo, hi)


def _load_packed_rows(ref, lead, n, ns):
    parts = []
    for c in range(ns):
        parts += list(_unpack_pair(ref[lead + (pl.ds(c, n, stride=ns), slice(None))]))
    return jnp.concatenate(parts, axis=1)


def _rope_tables(pos):
    half = ROT_DIM // 2
    n = pos.shape[0]
    inv_freq = jnp.power(jnp.float32(ROPE_THETA), -jnp.arange(half, dtype=F32) * (2.0 / ROT_DIM))
    ang = pos.astype(F32)[:, None] * inv_freq[None, :]
    cos, sin = jnp.cos(ang), jnp.sin(ang)
    rest = HEAD_DIM - ROT_DIM
    cos_h = jnp.concatenate([cos, cos, jnp.ones((n, rest), F32)], axis=1)
    sa_h = jnp.concatenate([jnp.zeros((n, half), F32), sin, jnp.zeros((n, rest), F32)], axis=1)
    sb_h = jnp.concatenate([-sin, jnp.zeros((n, half + rest), F32)], axis=1)
    rep = LANES // HEAD_DIM
    return tuple(jnp.concatenate([t] * rep, axis=1) for t in (cos_h, sa_h, sb_h))


def _qkv_body(x_ref, w_ref, cos_ref, sa_ref, sb_ref, q_ref, k_ref, v_ref, xb_ref, *, nq):
    j = pl.program_id(1)

    @pl.when(j == 0)
    def _():
        xb_ref[...] = x_ref[...].astype(BF16)

    acc = _dot(xb_ref[...], w_ref[...])
    half = ROT_DIM // 2
    is_q = j < nq
    scale = jnp.where(is_q, jnp.float32(HEAD_DIM ** -0.5), jnp.float32(1.0))
    cos, sa, sb = cos_ref[...] * scale, sa_ref[...] * scale, sb_ref[...] * scale
    outs = []
    for c in range(acc.shape[1] // LANES):
        blk = acc[:, c * LANES:(c + 1) * LANES]
        roped = blk * cos + pltpu.roll(blk, half, 1) * sa + pltpu.roll(blk, LANES - half, 1) * sb
        outs.append(roped if c * LANES < KV_DIM else jnp.where(is_q, roped, blk))
    out = jnp.concatenate(outs, axis=1)

    @pl.when(is_q)
    def _():
        q_ref[...] = out.astype(BF16)

    @pl.when(j == nq)
    def _():
        k_ref[...] = out[:, :KV_DIM]
        v_ref[...] = out[:, KV_DIM:]


def _qkv_call(x, w_qkv, tables, tm):
    t, d = x.shape
    tn = 2 * KV_DIM
    assert Q_DIM % tn == 0
    nq = Q_DIM // tn
    tab_blocks = tables[0].shape[0] // tm
    tab_spec = pl.BlockSpec((tm, LANES), lambda i, j: (i % tab_blocks, 0))
    return pl.pallas_call(
        functools.partial(_qkv_body, nq=nq),
        grid=(t // tm, nq + 1),
        in_specs=[
            pl.BlockSpec((tm, d), lambda i, j: (i, 0)),
            pl.BlockSpec((d, tn), lambda i, j: (0, j)),
            tab_spec, tab_spec, tab_spec,
        ],
        out_specs=[
            pl.BlockSpec((tm, tn), lambda i, j: (i, jnp.minimum(j, nq - 1))),
            pl.BlockSpec((tm, KV_DIM), lambda i, j: (i, 0)),
            pl.BlockSpec((tm, KV_DIM), lambda i, j: (i, 0)),
        ],
        out_shape=[
            jax.ShapeDtypeStruct((t, Q_DIM), BF16),
            jax.ShapeDtypeStruct((t, KV_DIM), F32),
            jax.ShapeDtypeStruct((t, KV_DIM), F32),
        ],
        scratch_shapes=[pltpu.VMEM((tm, d), BF16)],
        compiler_params=_params("parallel", "arbitrary"),
        name="qkv",
    )(x, w_qkv, *tables)


def _head_pair_operands(kv_chunk, odd):
    lane = lax.broadcasted_iota(I32, kv_chunk.shape, 1)
    own = jnp.where((lane >= HEAD_DIM) == odd, kv_chunk, 0.0)
    other = pltpu.roll(own, HEAD_DIM, 1)
    lo, hi = (other, own) if odd else (own, other)
    return lo.astype(BF16), hi.astype(BF16)


def _attend(q, kk, vv, valid, sinks_ref):
    heads = []
    for kh in range(N_KV_HEADS):
        c = (kh * HEAD_DIM) // LANES
        odd = bool((kh * HEAD_DIM) % LANES)
        k_ops = _head_pair_operands(kk[:, c * LANES:(c + 1) * LANES], odd)
        v_ops = _head_pair_operands(vv[:, c * LANES:(c + 1) * LANES], odd)
        for g in range(GROUP):
            h = kh * GROUP + g
            heads.append((h // 2, h, k_ops[h % 2], v_ops[h % 2]))
    scores = [jnp.where(valid, _dot_nt(q[:, ch * LANES:(ch + 1) * LANES], k_op), NEG_BIG)
              for ch, _, k_op, _ in heads]
    maxes = [jnp.maximum(jnp.max(s, axis=-1, keepdims=True), sinks_ref[h]) for s, (_, h, _, _) in zip(scores, heads)]
    probs = [jnp.exp(s - m) for s, m in zip(scores, maxes)]
    dens = [jnp.sum(p, axis=-1, keepdims=True) + jnp.exp(sinks_ref[h] - m)
            for p, m, (_, h, _, _) in zip(probs, maxes, heads)]
    outs = [_dot((p / den).astype(BF16), v_op) for p, den, (_, _, _, v_op) in zip(probs, dens, heads)]
    return jnp.concatenate([outs[2 * j] + outs[2 * j + 1] for j in range(N_HEADS // 2)], axis=1)


def _attn_prompt_body(sinks_ref, q_ref, kp_ref, kc_ref, vp_ref, vc_ref, o_ref):
    n = pl.program_id(1)
    w = WINDOW
    kk = jnp.concatenate([kp_ref[...], kc_ref[...]], axis=0)
    vv = jnp.concatenate([vp_ref[...], vc_ref[...]], axis=0)
    a = lax.broadcasted_iota(I32, (w, 2 * w), 0)
    c = lax.broadcasted_iota(I32, (w, 2 * w), 1)
    valid = (c > a) & (c <= a + w) & ((n > 0) | (c >= w))
    o_ref[...] = _attend(q_ref[...], kk, vv, valid, sinks_ref).astype(BF16)


def _attn_prompt_call(q, k, v, sinks, batch, seq):
    w = WINDOW
    nb = seq // w
    cur = lambda b, n: (b * nb + n, 0)
    prev = lambda b, n: (b * nb + jnp.maximum(n - 1, 0), 0)
    return pl.pallas_call(
        _attn_prompt_body,
        grid=(batch, nb),
        in_specs=[
            pl.BlockSpec(memory_space=pltpu.SMEM),
            pl.BlockSpec((w, Q_DIM), cur),
            pl.BlockSpec((w, KV_DIM), prev),
            pl.BlockSpec((w, KV_DIM), cur),
            pl.BlockSpec((w, KV_DIM), prev),
            pl.BlockSpec((w, KV_DIM), cur),
        ],
        out_specs=pl.BlockSpec((w, Q_DIM), cur),
        out_shape=jax.ShapeDtypeStruct((batch * seq, Q_DIM), BF16),
        compiler_params=_params("parallel", "parallel"),
        name="attn_prompt",
    )(sinks, q, k, k, v, v)


def _attn_sample_body(sinks_ref, q_ref, kc_ref, vc_ref, kn_ref, vn_ref, o_ref, kw_ref, vw_ref, *, bt, nkeys):
    n = bt * nkeys
    key_i = jnp.concatenate([lax.broadcasted_iota(I32, (nkeys, KV_DIM), 0)] * bt, axis=0)

    def window(cache_ref, new_ref):
        shifted = pltpu.roll(cache_ref[...].reshape(n, KV_DIM), n - 1, 0)
        new = jnp.broadcast_to(new_ref[...][:, None, :], (bt, nkeys, KV_DIM)).reshape(n, KV_DIM)
        return jnp.where(key_i == nkeys - 1, new, shifted)

    kk, vv = window(kc_ref, kn_ref), window(vc_ref, vn_ref)
    kw_ref[...] = kk.reshape(bt, nkeys, KV_DIM)
    vw_ref[...] = vv.reshape(bt, nkeys, KV_DIM)
    row_b = lax.broadcasted_iota(I32, (bt, n), 0)
    key_b = jnp.concatenate([jnp.full((bt, nkeys), b, I32) for b in range(bt)], axis=1)
    o_ref[...] = _attend(q_ref[...], kk, vv, row_b == key_b, sinks_ref).astype(BF16)


def _attn_sample_call(q, k_cache, v_cache, k_new, v_new, sinks):
    b, nkeys, _ = k_cache.shape
    bt = _tile(b, SUBLANES)
    cache = pl.BlockSpec((bt, nkeys, KV_DIM), lambda i: (i, 0, 0))
    new = pl.BlockSpec((bt, KV_DIM), lambda i: (i, 0))
    return pl.pallas_call(
        functools.partial(_attn_sample_body, bt=bt, nkeys=nkeys),
        grid=(b // bt,),
        in_specs=[pl.BlockSpec(memory_space=pltpu.SMEM), pl.BlockSpec((bt, Q_DIM), lambda i: (i, 0)),
                  cache, cache, new, new],
        out_specs=[pl.BlockSpec((bt, Q_DIM), lambda i: (i, 0)), cache, cache],
        out_shape=[jax.ShapeDtypeStruct((b, Q_DIM), BF16), jax.ShapeDtypeStruct(k_cache.shape, F32),
                   jax.ShapeDtypeStruct(v_cache.shape, F32)],
        compiler_params=_params("parallel"),
        name="attn_sample",
    )(sinks, q, k_cache, v_cache, k_new, v_new)


def _conv_body(*refs, decode, tiles_per_seq, tm, tail):
    if decode:
        x_ref, wb_ref, wc_ref, wh_ref, cw_ref, s0_ref, s1_ref, cb_ref, ut_ref, xb_ref = refs
    else:
        x_ref, wb_ref, wc_ref, wh_ref, cw_ref, cb_ref, ut_ref, xb_ref, carry_ref = refs
    i = pl.program_id(0)
    c = pl.program_id(1)

    @pl.when(c == 0)
    def _():
        xb_ref[...] = x_ref[...].astype(BF16)

    if not decode:
        @pl.when(i % tiles_per_seq == 0)
        def _():
            carry_ref[c] = jnp.zeros(carry_ref.shape[1:], F32)

    xb = xb_ref[...]
    tc = wb_ref.shape[1]
    sub = min(tc, 2 * LANES)
    cols = [slice(k * sub, (k + 1) * sub) for k in range(tc // sub)]
    b_gs = [_dot(xb, wb_ref[:, s]) for s in cols]
    us = [_dot(xb, wc_ref[:, s]) * _dot(xb, wh_ref[:, s]) for s in cols]
    cw = cw_ref[...]
    for s, b_g, u in zip(cols, b_gs, us):
        if decode:
            u_m1, u_m2 = s1_ref[:, s], s0_ref[:, s]
        else:
            prev = carry_ref[c]
            p_m2, p_m1 = prev[SUBLANES - 2:SUBLANES - 1, s], prev[SUBLANES - 1:SUBLANES, s]
            r = lax.broadcasted_iota(I32, u.shape, 0)
            u_m1 = jnp.where(r == 0, p_m1, pltpu.roll(u, 1, 0))
            u_m2 = jnp.where(r == 0, p_m2, jnp.where(r == 1, p_m1, pltpu.roll(u, 2, 0)))
        conv = cw[0:1, s] * u_m2 + cw[1:2, s] * u_m1 + cw[2:3, s] * u
        cb_ref[:, s] = (b_g * conv).astype(BF16)
        ut_ref[:, s] = u[tm - tail:]
    if not decode:
        carry_ref[c] = jnp.concatenate([u[tm - SUBLANES:] for u in us], axis=1)


def _conv_call(x, w_b, w_c, w_h, conv_w, tm, seq_rows, state=None):
    t, d = x.shape
    cdim = w_b.shape[1]
    tc = _tile(cdim, 4 * LANES)
    decode = state is not None
    tail = tm if decode else SUBLANES
    w_spec = pl.BlockSpec((d, tc), lambda i, c: (0, c))
    in_specs = [pl.BlockSpec((tm, d), lambda i, c: (i, 0)), w_spec, w_spec, w_spec,
                pl.BlockSpec((CONV_WIDTH, tc), lambda i, c: (0, c))]
    args = [x, w_b, w_c, w_h, conv_w]
    scratch = [pltpu.VMEM((tm, d), BF16)]
    if decode:
        in_specs += [pl.BlockSpec((tm, tc), lambda i, c: (i, c))] * 2
        args += list(state)
    else:
        scratch.append(pltpu.VMEM((cdim // tc, SUBLANES, tc), F32))
    return pl.pallas_call(
        functools.partial(_conv_body, decode=decode, tiles_per_seq=max(seq_rows // tm, 1), tm=tm, tail=tail),
        grid=(t // tm, cdim // tc),
        in_specs=in_specs,
        out_specs=[pl.BlockSpec((tm, tc), lambda i, c: (i, c)), pl.BlockSpec((tail, tc), lambda i, c: (i, c))],
        out_shape=[jax.ShapeDtypeStruct((t, cdim), BF16), jax.ShapeDtypeStruct((t // tm * tail, cdim), F32)],
        scratch_shapes=scratch,
        compiler_params=_params("arbitrary", "arbitrary"),
        name="conv",
    )(*args)


def _gate_body(x_ref, at_ref, cb_ref, wga_ref, wgc_ref, wa_ref, wco_ref, o_ref, xb_ref):
    @pl.when(pl.program_id(1) == 0)
    def _():
        xb_ref[...] = x_ref[...].astype(BF16)

    xb = xb_ref[...]
    g_a = _dot(xb, wga_ref[...])
    g_c = _dot(xb, wgc_ref[...])
    a = _dot(at_ref[...], wa_ref[...])
    c = _dot(cb_ref[...], wco_ref[...])
    o_ref[...] = (jax.nn.sigmoid(g_a) * a + jax.nn.sigmoid(g_c) * c).astype(BF16)


def _gate_call(x, attn, cb, w_ga, w_gc, w_a, w_co, tm):
    t, d = x.shape
    tn = _tile(d, 4 * LANES)
    row = lambda i, n: (i, 0)
    col = lambda i, n: (0, n)
    return pl.pallas_call(
        _gate_body,
        grid=(t // tm, d // tn),
        in_specs=[
            pl.BlockSpec((tm, d), row),
            pl.BlockSpec((tm, attn.shape[1]), row),
            pl.BlockSpec((tm, cb.shape[1]), row),
            pl.BlockSpec((d, tn), col),
            pl.BlockSpec((d, tn), col),
            pl.BlockSpec((w_a.shape[0], tn), col),
            pl.BlockSpec((w_co.shape[0], tn), col),
        ],
        out_specs=pl.BlockSpec((tm, tn), lambda i, n: (i, n)),
        out_shape=jax.ShapeDtypeStruct((t, d), BF16),
        scratch_shapes=[pltpu.VMEM((tm, d), BF16)],
        compiler_params=_params("parallel", "arbitrary"),
        name="gate",
    )(x, attn, cb, w_ga, w_gc, w_a, w_co)


def _post_body(x_ref, mp_ref, wo_ref, g_ref, b_ref, wr_ref, wsg_ref, wsu_ref, wsd_ref, *rest, alpha):
    hq_ref, base_ref, lg_ref = rest[-3:]
    tm = x_ref.shape[0]
    halves = 2 if tm % (2 * LANES) == 0 else 1
    hr = tm // halves
    ns = x_ref.shape[1] // (2 * LANES)
    parts = [slice(i * hr, (i + 1) * hr) for i in range(halves)]
    mixed = [_dot(mp_ref[p, :], wo_ref[...]) for p in parts]
    hs = [_layer_norm(alpha * x_ref[p, :] + m, g_ref[...], b_ref[...]) for p, m in zip(parts, mixed)]
    hbs = [h.astype(BF16) for h in hs]
    for p, hb in zip(parts, hbs):
        lg_ref[:, p] = _dot_nt(wr_ref[...], hb)
    gates = [_dot(hb, wsg_ref[...]) for hb in hbs]
    ups = [_dot(hb, wsu_ref[...]) for hb in hbs]
    acts = [(jax.nn.silu(g) * u).astype(BF16) for g, u in zip(gates, ups)]
    for p, h, a in zip(parts, hs, acts):
        base_ref[p, :] = alpha * h + _dot(a, wsd_ref[...])
    for i, hb in enumerate(hbs):
        _store_packed_rows(hq_ref, (), hb, hr, first_row=i * hr * ns)


def _post_call(x, mp, w_o, ln_g, ln_b, w_r_t, w_sg, w_su, w_sd, tm, alpha, hq_tokens, hq_buf, hq_offset):
    t, d = x.shape
    n_e, f = w_r_t.shape[0], w_sg.shape[1]
    ns = d // (2 * LANES)
    off = hq_offset // tm
    row = lambda i: (i, 0)
    whole = lambda shape: pl.BlockSpec(shape, lambda i: (0, 0), pipeline_mode=pl.Buffered(1))
    in_specs = [
        pl.BlockSpec((tm, d), row),
        pl.BlockSpec((tm, d), row),
        whole((d, d)), whole((1, d)), whole((1, d)), whole((n_e, d)), whole((d, f)), whole((d, f)), whole((f, d)),
    ]
    args = [x, mp, w_o, ln_g, ln_b, w_r_t, w_sg, w_su, w_sd]
    aliases = {}
    if hq_buf is not None:
        aliases = {len(args): 0}
        in_specs.append(pl.BlockSpec(memory_space=pl.ANY))
        args.append(hq_buf)
    return pl.pallas_call(
        functools.partial(_post_body, alpha=alpha),
        grid=(t // tm,),
        in_specs=in_specs,
        out_specs=[
            pl.BlockSpec((tm * ns, LANES), lambda i: (i + off, 0)),
            pl.BlockSpec((tm, d), row),
            pl.BlockSpec((n_e, tm), lambda i: (0, i)),
        ],
        out_shape=[
            jax.ShapeDtypeStruct((hq_tokens * ns, LANES), I32),
            jax.ShapeDtypeStruct((t, d), F32),
            jax.ShapeDtypeStruct((n_e, t), F32),
        ],
        input_output_aliases=aliases,
        compiler_params=_params("parallel"),
        name="post",
    )(*args)


def _route_body(bias_ref, lg_ref, idx_ref, w_ref, cnt_ref, *, n_tok):
    per_group = N_EXPERTS // N_GROUPS
    neg_inf = jnp.float32(-jnp.inf)
    i = pl.program_id(0)
    scores = [jax.nn.sigmoid(lg_ref[e]) for e in range(N_EXPERTS)]
    choice = [scores[e] + bias_ref[e] for e in range(N_EXPERTS)]

    group_score = []
    for g in range(N_GROUPS):
        vals = choice[g * per_group:(g + 1) * per_group]
        m1 = functools.reduce(jnp.maximum, vals)
        m2 = jnp.full_like(m1, neg_inf)
        found = jnp.zeros(m1.shape, jnp.bool_)
        for v in vals:
            eq = v == m1
            m2 = jnp.maximum(m2, jnp.where(eq & ~found, neg_inf, v))
            found = found | eq
        group_score.append(m1 + m2)

    masked = []
    for g in range(N_GROUPS):
        ahead = jnp.zeros(group_score[g].shape, I32)
        for o in range(N_GROUPS):
            if o == g:
                continue
            beats = group_score[o] > group_score[g]
            if o < g:
                beats = beats | (group_score[o] == group_score[g])
            ahead = ahead + beats.astype(I32)
        keep = ahead < TOPK_GROUPS
        masked += [jnp.where(keep, choice[e], neg_inf) for e in range(g * per_group, (g + 1) * per_group)]

    shape = masked[0].shape
    token = (i * SUBLANES + lax.broadcasted_iota(I32, shape, 0)) * LANES + lax.broadcasted_iota(I32, shape, 1)
    real = (token < n_tok).astype(I32)
    chosen = [jnp.zeros(shape, I32) for _ in range(N_EXPERTS)]
    picked_w = []
    for r in range(TOP_K):
        best = functools.reduce(jnp.maximum, masked)
        sel = jnp.full(shape, N_EXPERTS, I32)
        for e in reversed(range(N_EXPERTS)):
            sel = jnp.where(masked[e] == best, e, sel)
        w = jnp.zeros(shape, F32)
        for e in range(N_EXPERTS):
            hit = sel == e
            w = jnp.where(hit, scores[e], w)
            masked[e] = jnp.where(hit, neg_inf, masked[e])
            chosen[e] = jnp.where(hit, real, chosen[e])
        idx_ref[r] = sel
        picked_w.append(w)
    total = functools.reduce(lambda a, b: a + b, picked_w)
    for r in range(TOP_K):
        w_ref[r] = picked_w[r] / total * ROUTED_SCALE

    @pl.when(i == 0)
    def _():
        cnt_ref[...] = jnp.zeros(cnt_ref.shape, I32)

    for e in range(N_EXPERTS):
        cnt_ref[e] += chosen[e]


def _route_call(logits_t, bias, n_tok):
    n_e, t = logits_t.shape
    rows = t // LANES
    lg3 = logits_t.reshape(n_e, rows, LANES)
    pick = pl.BlockSpec((TOP_K, SUBLANES, LANES), lambda i: (0, i, 0))
    idx, w, cnt = pl.pallas_call(
        functools.partial(_route_body, n_tok=n_tok),
        grid=(rows // SUBLANES,),
        in_specs=[
            pl.BlockSpec(memory_space=pltpu.SMEM),
            pl.BlockSpec((n_e, SUBLANES, LANES), lambda i: (0, i, 0)),
        ],
        out_specs=[pick, pick, pl.BlockSpec((n_e, SUBLANES, LANES), lambda i: (0, 0, 0))],
        out_shape=[
            jax.ShapeDtypeStruct((TOP_K, rows, LANES), I32),
            jax.ShapeDtypeStruct((TOP_K, rows, LANES), F32),
            jax.ShapeDtypeStruct((n_e, SUBLANES, LANES), I32),
        ],
        compiler_params=_params("arbitrary"),
        name="route",
    )(bias, lg3)
    return idx.reshape(TOP_K, t), w.reshape(TOP_K, t), jnp.sum(cnt, axis=(1, 2))


def _moe_body(be_ref, nreal_ref, tok0_ref, tokn_ref, dstp_ref, hq_hbm, wg_ref, wu_ref, wd_ref, out_hbm,
              xb0, xb1, yb0, yb1, wgb, wub, wdb, gsem, ssem, *, rows):
    b = pl.program_id(0)
    nb = pl.num_programs(0)
    xbufs, ybufs = (xb0, xb1), (yb0, yb1)
    ns = xb0.shape[0] // rows
    grp = min(MOE_COPY_GROUP, rows)

    def n_real(k):
        return jnp.where((k >= 0) & (k < nb), nreal_ref[jnp.clip(k, 0, nb - 1)], 0)

    def token(ref, i):
        return ref.at[pl.ds(pl.multiple_of(i * ns, ns), ns), :]

    def start_groups(make_copy, n):
        for r in range(rows):
            @pl.when(n > (r // grp) * grp)
            def _(r=r):
                make_copy(r).start(priority=r % 2)

    def wait_groups(src, dst, sem, n):
        for g in range(rows // grp):
            @pl.when(n > g * grp)
            def _(g=g):
                part = pl.ds(g * grp * ns, grp * ns)
                pltpu.make_async_copy(src.at[part, :], dst.at[part, :], sem).wait()

    def gather(idx_ref, dst_buf, sem, n):
        start_groups(lambda r: pltpu.make_async_copy(token(hq_hbm, idx_ref[0, 0, r]), token(dst_buf, r), sem), n)

    @pl.when(b == 0)
    def _():
        xb0[...] = jnp.zeros(xb0.shape, I32)
        xb1[...] = jnp.zeros(xb1.shape, I32)
        gather(tok0_ref, xb0, gsem.at[0], n_real(0))

    @pl.when((b == 0) | (be_ref[b] != be_ref[jnp.maximum(b - 1, 0)]))
    def _():
        wgb[...] = wg_ref[...].astype(BF16)
        wub[...] = wu_ref[...].astype(BF16)
        wdb[...] = wd_ref[...].astype(BF16)

    def scatter_previous(o):
        start_groups(lambda r: pltpu.make_async_copy(token(ybufs[o], r), token(out_hbm, dstp_ref[0, 0, r]),
                                                     ssem.at[o]), n_real(b - 1))

    def block(s):
        o = 1 - s
        wait_groups(hq_hbm, xbufs[s], gsem.at[s], n_real(b))
        gather(tokn_ref, xbufs[o], gsem.at[o], n_real(b + 1))
        scatter_previous(o)
        xb = _load_packed_rows(xbufs[s], (), rows, ns).astype(BF16)
        act = jax.nn.silu(_dot(xb, wgb[...])) * _dot(xb, wub[...])
        y = _dot(act.astype(BF16), wdb[...]).astype(BF16)
        wait_groups(ybufs[s], out_hbm, ssem.at[s], n_real(b - 2))
        _store_packed_rows(ybufs[s], (), y, rows)

    def empty_block(s):
        scatter_previous(1 - s)
        wait_groups(ybufs[s], out_hbm, ssem.at[s], n_real(b - 2))

    for s in range(2):
        @pl.when((b % 2 == s) & (n_real(b) > 0))
        def _(s=s):
            block(s)

        @pl.when((b % 2 == s) & (n_real(b) == 0))
        def _(s=s):
            empty_block(s)

    @pl.when(b == nb - 1)
    def _():
        wait_groups(yb0, out_hbm, ssem.at[0], n_real(b - 1))


def _moe_call(block_e, n_real, tok, dst, hq, w_gate, w_up, w_down, out_rows, rows):
    n_blocks = block_e.shape[0]
    assert n_blocks % 2 == 0 and rows % min(MOE_COPY_GROUP, rows) == 0
    n_e, d, f = w_gate.shape
    ns = d // (2 * LANES)
    shifted = lambda off: pl.BlockSpec(
        (1, 1, rows), lambda b, be, nr: (jnp.clip(b + off, 0, n_blocks - 1), 0, 0), memory_space=pltpu.SMEM)
    expert = lambda b, be, nr: (be[b], 0, 0)
    return pl.pallas_call(
        functools.partial(_moe_body, rows=rows),
        grid_spec=pltpu.PrefetchScalarGridSpec(
            num_scalar_prefetch=2,
            grid=(n_blocks,),
            in_specs=[
                pl.BlockSpec((1, 1, rows), lambda b, be, nr: (0, 0, 0), memory_space=pltpu.SMEM),
                shifted(1), shifted(-1),
                pl.BlockSpec(memory_space=pl.ANY),
                pl.BlockSpec((None, d, f), expert),
                pl.BlockSpec((None, d, f), expert),
                pl.BlockSpec((None, f, d), expert),
            ],
            out_specs=pl.BlockSpec(memory_space=pl.ANY),
            scratch_shapes=[pltpu.VMEM((rows * ns, LANES), I32)] * 4 + [
                pltpu.VMEM((d, f), BF16),
                pltpu.VMEM((d, f), BF16),
                pltpu.VMEM((f, d), BF16),
                pltpu.SemaphoreType.DMA((2,)),
                pltpu.SemaphoreType.DMA((2,)),
            ],
        ),
        out_shape=jax.ShapeDtypeStruct((out_rows * ns, LANES), I32),
        compiler_params=_params("arbitrary"),
        name="moe",
    )(block_e, n_real, tok, tok, dst, hq, w_gate, w_up, w_down)


def _dispatch_plan(idx, counts, n_tok, rows):
    m = n_tok * TOP_K
    n_real = (m + N_EXPERTS * (rows - 1) + rows - 1) // rows
    n_blocks = n_real + 1 + (n_real + 1) % 2
    p = n_blocks * rows
    n_spare = p - m
    plane = -(-(n_tok + -(-n_spare // TOP_K)) // SUBLANES) * SUBLANES
    flat_e = idx.T.reshape(m)
    order = jnp.argsort(flat_e).astype(I32)
    padded = (counts + rows - 1) // rows * rows
    pad_end = jnp.cumsum(padded)
    pad_start = pad_end - padded
    start = jnp.cumsum(counts) - counts
    block_first = jnp.arange(n_blocks, dtype=I32) * rows
    block_e = jnp.minimum(jnp.sum((pad_end[None, :] <= block_first[:, None]).astype(I32), axis=1), N_EXPERTS - 1)
    slot = block_first[:, None] + jnp.arange(rows, dtype=I32)[None, :]
    own = block_e[:, None] == jnp.arange(N_EXPERTS, dtype=I32)[None, :]
    lookup = lambda table: jnp.sum(jnp.where(own, table[None, :], 0), axis=1, keepdims=True)
    q = slot - lookup(pad_start)
    cnt_b = lookup(counts)
    start_b = lookup(start)
    real = q < cnt_b
    assign = order[jnp.clip(start_b + q, 0, m - 1)]
    t_of, j_of = assign // TOP_K, assign % TOP_K
    spare_rank = slot - (start_b + jnp.minimum(q, cnt_b))
    tok = jnp.where(real, t_of, 0)
    dst = jnp.where(real, j_of * plane + t_of, (spare_rank % TOP_K) * plane + n_tok + spare_rank // TOP_K)
    n_real_rows = jnp.clip(cnt_b[:, 0] - q[:, 0], 0, rows).astype(I32)
    shape = (n_blocks, 1, rows)
    return block_e.astype(I32), n_real_rows, tok.reshape(shape), dst.reshape(shape), plane


def _final_body(base_ref, y8_ref, w_ref, g_ref, b_ref, o_ref):
    w = w_ref[...]
    ffn = None
    tm = w.shape[0]
    ns = y8_ref.shape[1] // tm
    for j in range(TOP_K):
        yj = _load_packed_rows(y8_ref, (j,), tm, ns) * w[:, j:j + 1]
        ffn = yj if ffn is None else ffn + yj
    o_ref[...] = _layer_norm(base_ref[...] + ffn, g_ref[...], b_ref[...])


def _final_call(base, y8, w8, ln_g, ln_b, tm, row_offset):
    t, d = base.shape
    off = row_offset // tm
    ns = d // (2 * LANES)
    return pl.pallas_call(
        _final_body,
        grid=(t // tm,),
        in_specs=[
            pl.BlockSpec((tm, d), lambda i: (i, 0)),
            pl.BlockSpec((TOP_K, tm * ns, LANES), lambda i: (0, i + off, 0)),
            pl.BlockSpec((tm, TOP_K), lambda i: (i + off, 0)),
            pl.BlockSpec((1, d), lambda i: (0, 0)),
            pl.BlockSpec((1, d), lambda i: (0, 0)),
        ],
        out_specs=pl.BlockSpec((tm, d), lambda i: (i, 0)),
        out_shape=jax.ShapeDtypeStruct((t, d), F32),
        compiler_params=_params("parallel"),
        name="final",
    )(base, y8, w8, ln_g, ln_b)


def _mixer_and_post(x, pos_tables, w, tm, seq_rows, attend, conv_state, alpha, hq_tokens, hq_buf, hq_offset):
    q, k, v = _qkv_call(x, w['qkv'], pos_tables, tm)
    attn = attend(q, k, v)
    cb, u_tail = _conv_call(x, w['b'], w['c'], w['h'], w['conv'], tm, seq_rows, conv_state)
    mp = _gate_call(x, attn, cb, w['ga'], w['gc'], w['attn_out'], w['conv_out'], tm)
    hq, base, logits_t = _post_call(x, mp, w['o'], w['ln1_g'], w['ln1_b'], w['router_t'],
                                    w['sh_gate'], w['sh_up'], w['sh_down'], min(tm, 4 * LANES), alpha,
                                    hq_tokens, hq_buf, hq_offset)
    return k, v, u_tail, hq, base, logits_t


def kernel(x_prompt, x_sample, cache_k, cache_v, state_conv, w_in, attn_sinks, conv_w, w_attn_out, w_conv_out, w_o, ln1_g, ln1_b, w_router, router_bias, w_exp_gate, w_exp_up, w_exp_down, w_sh_gate, w_sh_up, w_sh_down, ln2_g, ln2_b):
    depth, d, _ = w_in.shape
    batch, seq, _ = x_prompt.shape
    dec_batch, dec_seq, _ = x_sample.shape
    win_buf = cache_k.shape[2]
    cdim = conv_w.shape[2]
    assert dec_seq == 1 and win_buf == WINDOW and seq % WINDOW == 0
    alpha = (2 * depth) ** 0.25
    t_p, t_s = batch * seq, dec_batch * dec_seq
    tm_p, tm_s = _tile(seq, 8 * LANES), _tile(t_s, LANES)
    assert t_p % tm_s == 0 and t_s % tm_s == 0
    tab_p = _rope_tables(jnp.arange(seq))
    tab_s = _rope_tables(jnp.full((tm_s,), PAST_LEN, I32))

    yp = x_prompt.reshape(t_p, d)
    ys = x_sample.reshape(t_s, d)
    p_k, p_v, p_c, s_k, s_v, s_c = [], [], [], [], [], []
    for l in range(depth):
        wl = w_in[l].astype(BF16)
        o = 0
        w = {}
        for name, width in (('qkv', Q_DIM + 2 * KV_DIM), ('b', cdim), ('c', cdim), ('h', cdim), ('ga', d), ('gc', d)):
            w[name] = wl[:, o:o + width]
            o += width
        w.update(
            conv=conv_w[l], attn_out=w_attn_out[l].astype(BF16), conv_out=w_conv_out[l].astype(BF16),
            o=w_o[l].astype(BF16), ln1_g=ln1_g[l][None], ln1_b=ln1_b[l][None],
            router_t=w_router[l].T.astype(BF16), sh_gate=w_sh_gate[l].astype(BF16),
            sh_up=w_sh_up[l].astype(BF16), sh_down=w_sh_down[l].astype(BF16))
        sinks = attn_sinks[l]

        n_tok = t_p + t_s
        k, v, u_tail, hq_p, base_p, lg_p = _mixer_and_post(
            yp, tab_p, w, tm_p, seq,
            lambda q, k, v: _attn_prompt_call(q, k, v, sinks, batch, seq), None, alpha, n_tok, None, 0)
        keep = min(WINDOW, seq)
        for kv, acc in ((k, p_k), (v, p_v)):
            tail_rows = kv.reshape(batch, seq, KV_DIM)[:, seq - keep:]
            acc.append(tail_rows.reshape(batch, keep, N_KV_HEADS, HEAD_DIM))
        tails = u_tail.reshape(batch, seq // tm_p, SUBLANES, cdim)
        p_c.append(tails[:, -1, SUBLANES - (CONV_WIDTH - 1):])

        new_kv = {}

        def attend_sample(q, k, v, l=l):
            attn, k_win, v_win = _attn_sample_call(q, cache_k[l].reshape(t_s, win_buf, KV_DIM),
                                                   cache_v[l].reshape(t_s, win_buf, KV_DIM), k, v, sinks)
            new_kv['k'] = k_win.reshape(t_s, win_buf, N_KV_HEADS, HEAD_DIM)
            new_kv['v'] = v_win.reshape(t_s, win_buf, N_KV_HEADS, HEAD_DIM)
            return attn

        state = (state_conv[l][:, 0], state_conv[l][:, 1])
        _, _, u_s, hq, base_s, lg_s = _mixer_and_post(ys, tab_s, w, tm_s, 1, attend_sample, state, alpha,
                                                      n_tok, hq_p, t_p)
        s_k.append(new_kv['k'])
        s_v.append(new_kv['v'])
        s_c.append(jnp.concatenate([state_conv[l][:, 1:], u_s[:, None]], axis=1))

        route_tile = SUBLANES * LANES
        t_pad = -(-n_tok // route_tile) * route_tile
        logits_t = jnp.concatenate([lg_p, lg_s, jnp.zeros((N_EXPERTS, t_pad - n_tok), F32)], axis=1)
        idx, wts, counts = _route_call(logits_t, router_bias[l], n_tok)
        idx, wts = idx[:, :n_tok], wts[:, :n_tok]
        block_e, n_real, tok, dst, plane = _dispatch_plan(idx, counts, n_tok, MOE_BLOCK_ROWS)
        y8 = _moe_call(block_e, n_real, tok, dst, hq, w_exp_gate[l], w_exp_up[l], w_exp_down[l],
                       TOP_K * plane, MOE_BLOCK_ROWS)
        y8 = y8.reshape(TOP_K, plane * (d // (2 * LANES)), LANES)
        w8 = wts.T
        yp = _final_call(base_p, y8, w8, ln2_g[l][None], ln2_b[l][None], min(tm_p, 2 * LANES), 0)
        ys = _final_call(base_s, y8, w8, ln2_g[l][None], ln2_b[l][None], tm_s, t_p)

    return (yp.reshape(batch, seq, d), ys.reshape(dec_batch, dec_seq, d), jnp.stack(p_k), jnp.stack(p_v),
            jnp.stack(p_c), jnp.stack(s_k), jnp.stack(s_v), jnp.stack(s_c))
```

```python
import functools

import jax
import jax.numpy as jnp
from jax import lax
from jax.experimental import pallas as pl
from jax.experimental.pallas import tpu as pltpu

N_HEADS = 16
N_KV_HEADS = 4
HEAD_DIM = 64
GROUP = N_HEADS // N_KV_HEADS
Q_DIM = N_HEADS * HEAD_DIM
KV_DIM = N_KV_HEADS * HEAD_DIM
ROT_DIM = HEAD_DIM // 4
ROPE_THETA = 500000.0
WINDOW = 128
CONV_WIDTH = 3
PAST_LEN = 16384
N_EXPERTS = 64
N_GROUPS = 8
TOPK_GROUPS = 4
TOP_K = 8
ROUTED_SCALE = 2.5
LN_EPS = 1e-5
MOE_BLOCK_ROWS = 256
MOE_COPY_GROUP = 32

LANES = 128
SUBLANES = 8
VMEM_LIMIT_BYTES = 48 * 1024 * 1024
NEG_BIG = -1e30

F32 = jnp.float32
BF16 = jnp.bfloat16
I32 = jnp.int32


def _tile(n, pref):
    t = pref
    while t > 1 and n % t:
        t //= 2
    return t


def _params(*sem):
    return pltpu.CompilerParams(dimension_semantics=sem, vmem_limit_bytes=VMEM_LIMIT_BYTES)


def _dot(a, b):
    return jnp.dot(a, b, preferred_element_type=F32)


def _dot_nt(a, b):
    return lax.dot_general(a, b, (((1,), (1,)), ((), ())), preferred_element_type=F32)


def _layer_norm(x, g, b):
    mu = jnp.mean(x, axis=-1, keepdims=True)
    xc = x - mu
    var = jnp.mean(xc * xc, axis=-1, keepdims=True)
    return xc * lax.rsqrt(var + LN_EPS) * g + b


def _pack_pair(lo, hi):
    lo32 = lax.bitcast_convert_type(lo.astype(F32), I32)
    hi32 = lax.bitcast_convert_type(hi.astype(F32), I32)
    return lax.shift_right_logical(lo32, 16) | (hi32 & -65536)


def _unpack_pair(p):
    lo = lax.bitcast_convert_type(lax.shift_left(p, 16), F32)
    hi = lax.bitcast_convert_type(p & -65536, F32)
    return lo, hi


def _store_packed_rows(ref, lead, x, n, first_row=0):
    ns = x.shape[1] // (2 * LANES)
    for c in range(ns):
        lo = x[:, 2 * c * LANES:(2 * c + 1) * LANES]
        hi = x[:, (2 * c + 1) * LANES:(2 * c + 2) * LANES]
        ref[lead + (pl.ds(first_row + c, n, stride=ns), slice(None))] = _pack_pair(lo, hi)


def _load_packed_rows(ref, lead, n, ns):
    parts = []
    for c in range(ns):
        parts += list(_unpack_pair(ref[lead + (pl.ds(c, n, stride=ns), slice(None))]))
    return jnp.concatenate(parts, axis=1)


def _rope_tables(pos):
    half = ROT_DIM // 2
    n = pos.shape[0]
    inv_freq = jnp.power(jnp.float32(ROPE_THETA), -jnp.arange(half, dtype=F32) * (2.0 / ROT_DIM))
    ang = pos.astype(F32)[:, None] * inv_freq[None, :]
    cos, sin = jnp.cos(ang), jnp.sin(ang)
    rest = HEAD_DIM - ROT_DIM
    cos_h = jnp.concatenate([cos, cos, jnp.ones((n, rest), F32)], axis=1)
    sa_h = jnp.concatenate([jnp.zeros((n, half), F32), sin, jnp.zeros((n, rest), F32)], axis=1)
    sb_h = jnp.concatenate([-sin, jnp.zeros((n, half + rest), F32)], axis=1)
    rep = LANES // HEAD_DIM
    return tuple(jnp.concatenate([t] * rep, axis=1) for t in (cos_h, sa_h, sb_h))


def _qkv_body(x_ref, w_ref, cos_ref, sa_ref, sb_ref, q_ref, k_ref, v_ref, xb_ref, *, nq):
    j = pl.program_id(1)

    @pl.when(j == 0)
    def _():
        xb_ref[...] = x_ref[...].astype(BF16)

    tm = x_ref.shape[0]
    part = min(tm, 2 * LANES)
    rows = [slice(p * part, (p + 1) * part) for p in range(tm // part)]
    accs = [_dot(xb_ref[r, :], w_ref[...]) for r in rows]
    half = ROT_DIM // 2
    is_q = j < nq
    scale = jnp.where(is_q, jnp.float32(HEAD_DIM ** -0.5), jnp.float32(1.0))
    outs = []
    for r, acc in zip(rows, accs):
        cos, sa, sb = cos_ref[r, :] * scale, sa_ref[r, :] * scale, sb_ref[r, :] * scale
        chunks = []
        for c in range(acc.shape[1] // LANES):
            blk = acc[:, c * LANES:(c + 1) * LANES]
            roped = blk * cos + pltpu.roll(blk, half, 1) * sa + pltpu.roll(blk, LANES - half, 1) * sb
            chunks.append(roped if c * LANES < KV_DIM else jnp.where(is_q, roped, blk))
        outs.append(jnp.concatenate(chunks, axis=1))

    @pl.when(is_q)
    def _():
        for r, out in zip(rows, outs):
            q_ref[r, :] = out.astype(BF16)

    @pl.when(j == nq)
    def _():
        for r, out in zip(rows, outs):
            k_ref[r, :] = out[:, :KV_DIM]
            v_ref[r, :] = out[:, KV_DIM:]


def _qkv_call(x, w_qkv, tables, tm):
    t, d = x.shape
    tn = 2 * KV_DIM
    assert Q_DIM % tn == 0
    nq = Q_DIM // tn
    tab_blocks = tables[0].shape[0] // tm
    tab_spec = pl.BlockSpec((tm, LANES), lambda i, j: (i % tab_blocks, 0))
    return pl.pallas_call(
        functools.partial(_qkv_body, nq=nq),
        grid=(t // tm, nq + 1),
        in_specs=[
            pl.BlockSpec((tm, d), lambda i, j: (i, 0)),
            pl.BlockSpec((d, tn), lambda i, j: (0, j)),
            tab_spec, tab_spec, tab_spec,
        ],
        out_specs=[
            pl.BlockSpec((tm, tn), lambda i, j: (i, jnp.minimum(j, nq - 1))),
            pl.BlockSpec((tm, KV_DIM), lambda i, j: (i, 0)),
            pl.BlockSpec((tm, KV_DIM), lambda i, j: (i, 0)),
        ],
        out_shape=[
            jax.ShapeDtypeStruct((t, Q_DIM), BF16),
            jax.ShapeDtypeStruct((t, KV_DIM), F32),
            jax.ShapeDtypeStruct((t, KV_DIM), F32),
        ],
        scratch_shapes=[pltpu.VMEM((tm, d), BF16)],
        compiler_params=_params("parallel", "arbitrary"),
        name="qkv",
    )(x, w_qkv, *tables)


def _head_pair_operands(kv_chunk, odd):
    lane = lax.broadcasted_iota(I32, kv_chunk.shape, 1)
    own = jnp.where((lane >= HEAD_DIM) == odd, kv_chunk, 0.0)
    other = pltpu.roll(own, HEAD_DIM, 1)
    lo, hi = (other, own) if odd else (own, other)
    return lo.astype(BF16), hi.astype(BF16)


def _attend(q, kk, vv, valid, sinks_ref):
    heads = []
    for kh in range(N_KV_HEADS):
        c = (kh * HEAD_DIM) // LANES
        odd = bool((kh * HEAD_DIM) % LANES)
        k_ops = _head_pair_operands(kk[:, c * LANES:(c + 1) * LANES], odd)
        v_ops = _head_pair_operands(vv[:, c * LANES:(c + 1) * LANES], odd)
        for g in range(GROUP):
            h = kh * GROUP + g
            heads.append((h // 2, h, k_ops[h % 2], v_ops[h % 2]))
    scores = [jnp.where(valid, _dot_nt(q[:, ch * LANES:(ch + 1) * LANES], k_op), NEG_BIG)
              for ch, _, k_op, _ in heads]
    maxes = [jnp.maximum(jnp.max(s, axis=-1, keepdims=True), sinks_ref[h]) for s, (_, h, _, _) in zip(scores, heads)]
    probs = [jnp.exp(s - m) for s, m in zip(scores, maxes)]
    dens = [jnp.sum(p, axis=-1, keepdims=True) + jnp.exp(sinks_ref[h] - m)
            for p, m, (_, h, _, _) in zip(probs, maxes, heads)]
    outs = [_dot((p / den).astype(BF16), v_op) for p, den, (_, _, _, v_op) in zip(probs, dens, heads)]
    return jnp.concatenate([outs[2 * j] + outs[2 * j + 1] for j in range(N_HEADS // 2)], axis=1)


def _attn_prompt_body(sinks_ref, q_ref, kp_ref, kc_ref, vp_ref, vc_ref, o_ref):
    n = pl.program_id(1)
    w = WINDOW
    kk = jnp.concatenate([kp_ref[...], kc_ref[...]], axis=0)
    vv = jnp.concatenate([vp_ref[...], vc_ref[...]], axis=0)
    a = lax.broadcasted_iota(I32, (w, 2 * w), 0)
    c = lax.broadcasted_iota(I32, (w, 2 * w), 1)
    valid = (c > a) & (c <= a + w) & ((n > 0) | (c >= w))
    o_ref[...] = _attend(q_ref[...], kk, vv, valid, sinks_ref).astype(BF16)


def _attn_prompt_call(q, k, v, sinks, batch, seq):
    w = WINDOW
    nb = seq // w
    cur = lambda b, n: (b * nb + n, 0)
    prev = lambda b, n: (b * nb + jnp.maximum(n - 1, 0), 0)
    return pl.pallas_call(
        _attn_prompt_body,
        grid=(batch, nb),
        in_specs=[
            pl.BlockSpec(memory_space=pltpu.SMEM),
            pl.BlockSpec((w, Q_DIM), cur),
            pl.BlockSpec((w, KV_DIM), prev),
            pl.BlockSpec((w, KV_DIM), cur),
            pl.BlockSpec((w, KV_DIM), prev),
            pl.BlockSpec((w, KV_DIM), cur),
        ],
        out_specs=pl.BlockSpec((w, Q_DIM), cur),
        out_shape=jax.ShapeDtypeStruct((batch * seq, Q_DIM), BF16),
        compiler_params=_params("parallel", "parallel"),
        name="attn_prompt",
    )(sinks, q, k, k, v, v)


def _attn_sample_body(sinks_ref, q_ref, kc_ref, vc_ref, kn_ref, vn_ref, o_ref, kw_ref, vw_ref, *, bt, nkeys):
    n = bt * nkeys
    key_i = jnp.concatenate([lax.broadcasted_iota(I32, (nkeys, KV_DIM), 0)] * bt, axis=0)

    def window(cache_ref, new_ref):
        shifted = pltpu.roll(cache_ref[...].reshape(n, KV_DIM), n - 1, 0)
        new = jnp.broadcast_to(new_ref[...][:, None, :], (bt, nkeys, KV_DIM)).reshape(n, KV_DIM)
        return jnp.where(key_i == nkeys - 1, new, shifted)

    kk, vv = window(kc_ref, kn_ref), window(vc_ref, vn_ref)
    kw_ref[...] = kk.reshape(bt, nkeys, KV_DIM)
    vw_ref[...] = vv.reshape(bt, nkeys, KV_DIM)
    row_b = lax.broadcasted_iota(I32, (bt, n), 0)
    key_b = jnp.concatenate([jnp.full((bt, nkeys), b, I32) for b in range(bt)], axis=1)
    o_ref[...] = _attend(q_ref[...], kk, vv, row_b == key_b, sinks_ref).astype(BF16)


def _attn_sample_call(q, k_cache, v_cache, k_new, v_new, sinks):
    b, nkeys, _ = k_cache.shape
    bt = _tile(b, SUBLANES)
    cache = pl.BlockSpec((bt, nkeys, KV_DIM), lambda i: (i, 0, 0))
    new = pl.BlockSpec((bt, KV_DIM), lambda i: (i, 0))
    return pl.pallas_call(
        functools.partial(_attn_sample_body, bt=bt, nkeys=nkeys),
        grid=(b // bt,),
        in_specs=[pl.BlockSpec(memory_space=pltpu.SMEM), pl.BlockSpec((bt, Q_DIM), lambda i: (i, 0)),
                  cache, cache, new, new],
        out_specs=[pl.BlockSpec((bt, Q_DIM), lambda i: (i, 0)), cache, cache],
        out_shape=[jax.ShapeDtypeStruct((b, Q_DIM), BF16), jax.ShapeDtypeStruct(k_cache.shape, F32),
                   jax.ShapeDtypeStruct(v_cache.shape, F32)],
        compiler_params=_params("parallel"),
        name="attn_sample",
    )(sinks, q, k_cache, v_cache, k_new, v_new)


def _conv_body(*refs, decode, tiles_per_seq, tm, tail):
    if decode:
        x_ref, wb_ref, wc_ref, wh_ref, cw_ref, s0_ref, s1_ref, cb_ref, ut_ref, xb_ref = refs
    else:
        x_ref, wb_ref, wc_ref, wh_ref, cw_ref, cb_ref, ut_ref, xb_ref, carry_ref = refs
    i = pl.program_id(0)
    c = pl.program_id(1)

    @pl.when(c == 0)
    def _():
        xb_ref[...] = x_ref[...].astype(BF16)

    if not decode:
        @pl.when(i % tiles_per_seq == 0)
        def _():
            carry_ref[c] = jnp.zeros(carry_ref.shape[1:], F32)

    xb = xb_ref[...]
    tc = wb_ref.shape[1]
    sub = min(tc, 2 * LANES)
    cols = [slice(k * sub, (k + 1) * sub) for k in range(tc // sub)]
    b_gs = [_dot(xb, wb_ref[:, s]) for s in cols]
    us = [_dot(xb, wc_ref[:, s]) * _dot(xb, wh_ref[:, s]) for s in cols]
    cw = cw_ref[...]
    for s, b_g, u in zip(cols, b_gs, us):
        if decode:
            u_m1, u_m2 = s1_ref[:, s], s0_ref[:, s]
        else:
            prev = carry_ref[c]
            p_m2, p_m1 = prev[SUBLANES - 2:SUBLANES - 1, s], prev[SUBLANES - 1:SUBLANES, s]
            r = lax.broadcasted_iota(I32, u.shape, 0)
            u_m1 = jnp.where(r == 0, p_m1, pltpu.roll(u, 1, 0))
            u_m2 = jnp.where(r == 0, p_m2, jnp.where(r == 1, p_m1, pltpu.roll(u, 2, 0)))
        conv = cw[0:1, s] * u_m2 + cw[1:2, s] * u_m1 + cw[2:3, s] * u
        cb_ref[:, s] = (b_g * conv).astype(BF16)
        ut_ref[:, s] = u[tm - tail:]
    if not decode:
        carry_ref[c] = jnp.concatenate([u[tm - SUBLANES:] for u in us], axis=1)


def _conv_call(x, w_b, w_c, w_h, conv_w, tm, seq_rows, state=None):
    t, d = x.shape
    cdim = w_b.shape[1]
    tc = _tile(cdim, 4 * LANES)
    decode = state is not None
    tail = tm if decode else SUBLANES
    w_spec = pl.BlockSpec((d, tc), lambda i, c: (0, c))
    in_specs = [pl.BlockSpec((tm, d), lambda i, c: (i, 0)), w_spec, w_spec, w_spec,
                pl.BlockSpec((CONV_WIDTH, tc), lambda i, c: (0, c))]
    args = [x, w_b, w_c, w_h, conv_w]
    scratch = [pltpu.VMEM((tm, d), BF16)]
    if decode:
        in_specs += [pl.BlockSpec((tm, tc), lambda i, c: (i, c))] * 2
        args += list(state)
    else:
        scratch.append(pltpu.VMEM((cdim // tc, SUBLANES, tc), F32))
    return pl.pallas_call(
        functools.partial(_conv_body, decode=decode, tiles_per_seq=max(seq_rows // tm, 1), tm=tm, tail=tail),
        grid=(t // tm, cdim // tc),
        in_specs=in_specs,
        out_specs=[pl.BlockSpec((tm, tc), lambda i, c: (i, c)), pl.BlockSpec((tail, tc), lambda i, c: (i, c))],
        out_shape=[jax.ShapeDtypeStruct((t, cdim), BF16), jax.ShapeDtypeStruct((t // tm * tail, cdim), F32)],
        scratch_shapes=scratch,
        compiler_params=_params("arbitrary", "arbitrary"),
        name="conv",
    )(*args)


def _gate_body(x_ref, at_ref, cb_ref, wga_ref, wgc_ref, wa_ref, wco_ref, o_ref, xb_ref):
    @pl.when(pl.program_id(1) == 0)
    def _():
        xb_ref[...] = x_ref[...].astype(BF16)

    xb = xb_ref[...]
    g_a = _dot(xb, wga_ref[...])
    g_c = _dot(xb, wgc_ref[...])
    a = _dot(at_ref[...], wa_ref[...])
    c = _dot(cb_ref[...], wco_ref[...])
    o_ref[...] = (jax.nn.sigmoid(g_a) * a + jax.nn.sigmoid(g_c) * c).astype(BF16)


def _gate_call(x, attn, cb, w_ga, w_gc, w_a, w_co, tm):
    t, d = x.shape
    tn = _tile(d, 4 * LANES)
    row = lambda i, n: (i, 0)
    col = lambda i, n: (0, n)
    return pl.pallas_call(
        _gate_body,
        grid=(t // tm, d // tn),
        in_specs=[
            pl.BlockSpec((tm, d), row),
            pl.BlockSpec((tm, attn.shape[1]), row),
            pl.BlockSpec((tm, cb.shape[1]), row),
            pl.BlockSpec((d, tn), col),
            pl.BlockSpec((d, tn), col),
            pl.BlockSpec((w_a.shape[0], tn), col),
            pl.BlockSpec((w_co.shape[0], tn), col),
        ],
        out_specs=pl.BlockSpec((tm, tn), lambda i, n: (i, n)),
        out_shape=jax.ShapeDtypeStruct((t, d), BF16),
        scratch_shapes=[pltpu.VMEM((tm, d), BF16)],
        compiler_params=_params("parallel", "arbitrary"),
        name="gate",
    )(x, attn, cb, w_ga, w_gc, w_a, w_co)


def _post_body(x_ref, mp_ref, wo_ref, g_ref, b_ref, wr_ref, wsg_ref, wsu_ref, wsd_ref, *rest, alpha):
    hq_ref, base_ref, lg_ref = rest[-3:]
    tm = x_ref.shape[0]
    halves = 2 if tm % (2 * LANES) == 0 else 1
    hr = tm // halves
    ns = x_ref.shape[1] // (2 * LANES)
    parts = [slice(i * hr, (i + 1) * hr) for i in range(halves)]
    mixed = [_dot(mp_ref[p, :], wo_ref[...]) for p in parts]
    hs = [_layer_norm(alpha * x_ref[p, :] + m, g_ref[...], b_ref[...]) for p, m in zip(parts, mixed)]
    hbs = [h.astype(BF16) for h in hs]
    for p, hb in zip(parts, hbs):
        lg_ref[:, p] = _dot_nt(wr_ref[...], hb)
    gates = [_dot(hb, wsg_ref[...]) for hb in hbs]
    ups = [_dot(hb, wsu_ref[...]) for hb in hbs]
    acts = [(jax.nn.silu(g) * u).astype(BF16) for g, u in zip(gates, ups)]
    for p, h, a in zip(parts, hs, acts):
        base_ref[p, :] = alpha * h + _dot(a, wsd_ref[...])
    for i, hb in enumerate(hbs):
        _store_packed_rows(hq_ref, (), hb, hr, first_row=i * hr * ns)


def _post_call(x, mp, w_o, ln_g, ln_b, w_r_t, w_sg, w_su, w_sd, tm, alpha, hq_tokens, hq_buf, hq_offset):
    t, d = x.shape
    n_e, f = w_r_t.shape[0], w_sg.shape[1]
    ns = d // (2 * LANES)
    off = hq_offset // tm
    row = lambda i: (i, 0)
    whole = lambda shape: pl.BlockSpec(shape, lambda i: (0, 0), pipeline_mode=pl.Buffered(1))
    in_specs = [
        pl.BlockSpec((tm, d), row),
        pl.BlockSpec((tm, d), row),
        whole((d, d)), whole((1, d)), whole((1, d)), whole((n_e, d)), whole((d, f)), whole((d, f)), whole((f, d)),
    ]
    args = [x, mp, w_o, ln_g, ln_b, w_r_t, w_sg, w_su, w_sd]
    aliases = {}
    if hq_buf is not None:
        aliases = {len(args): 0}
        in_specs.append(pl.BlockSpec(memory_space=pl.ANY))
        args.append(hq_buf)
    return pl.pallas_call(
        functools.partial(_post_body, alpha=alpha),
        grid=(t // tm,),
        in_specs=in_specs,
        out_specs=[
            pl.BlockSpec((tm * ns, LANES), lambda i: (i + off, 0)),
            pl.BlockSpec((tm, d), row),
            pl.BlockSpec((n_e, tm), lambda i: (0, i)),
        ],
        out_shape=[
            jax.ShapeDtypeStruct((hq_tokens * ns, LANES), I32),
            jax.ShapeDtypeStruct((t, d), F32),
            jax.ShapeDtypeStruct((n_e, t), F32),
        ],
        input_output_aliases=aliases,
        compiler_params=_params("parallel"),
        name="post",
    )(*args)


def _route_body(bias_ref, lg_ref, idx_ref, w_ref, cnt_ref, *, n_tok):
    per_group = N_EXPERTS // N_GROUPS
    neg_inf = jnp.float32(-jnp.inf)
    i = pl.program_id(0)
    scores = [jax.nn.sigmoid(lg_ref[e]) for e in range(N_EXPERTS)]
    choice = [scores[e] + bias_ref[e] for e in range(N_EXPERTS)]

    group_score = []
    for g in range(N_GROUPS):
        vals = choice[g * per_group:(g + 1) * per_group]
        m1 = functools.reduce(jnp.maximum, vals)
        m2 = jnp.full_like(m1, neg_inf)
        found = jnp.zeros(m1.shape, jnp.bool_)
        for v in vals:
            eq = v == m1
            m2 = jnp.maximum(m2, jnp.where(eq & ~found, neg_inf, v))
            found = found | eq
        group_score.append(m1 + m2)

    masked = []
    for g in range(N_GROUPS):
        ahead = jnp.zeros(group_score[g].shape, I32)
        for o in range(N_GROUPS):
            if o == g:
                continue
            beats = group_score[o] > group_score[g]
            if o < g:
                beats = beats | (group_score[o] == group_score[g])
            ahead = ahead + beats.astype(I32)
        keep = ahead < TOPK_GROUPS
        masked += [jnp.where(keep, choice[e], neg_inf) for e in range(g * per_group, (g + 1) * per_group)]

    shape = masked[0].shape
    token = (i * SUBLANES + lax.broadcasted_iota(I32, shape, 0)) * LANES + lax.broadcasted_iota(I32, shape, 1)
    real = (token < n_tok).astype(I32)
    chosen = [jnp.zeros(shape, I32) for _ in range(N_EXPERTS)]
    picked_w = []
    for r in range(TOP_K):
        best = functools.reduce(jnp.maximum, masked)
        sel = jnp.full(shape, N_EXPERTS, I32)
        for e in reversed(range(N_EXPERTS)):
            sel = jnp.where(masked[e] == best, e, sel)
        w = jnp.zeros(shape, F32)
        for e in range(N_EXPERTS):
            hit = sel == e
            w = jnp.where(hit, scores[e], w)
            masked[e] = jnp.where(hit, neg_inf, masked[e])
            chosen[e] = jnp.where(hit, real, chosen[e])
        idx_ref[r] = sel
        picked_w.append(w)
    total = functools.reduce(lambda a, b: a + b, picked_w)
    for r in range(TOP_K):
        w_ref[r] = picked_w[r] / total * ROUTED_SCALE

    @pl.when(i == 0)
    def _():
        cnt_ref[...] = jnp.zeros(cnt_ref.shape, I32)

    for e in range(N_EXPERTS):
        cnt_ref[e] += chosen[e]


def _route_call(logits_t, bias, n_tok):
    n_e, t = logits_t.shape
    rows = t // LANES
    lg3 = logits_t.reshape(n_e, rows, LANES)
    pick = pl.BlockSpec((TOP_K, SUBLANES, LANES), lambda i: (0, i, 0))
    idx, w, cnt = pl.pallas_call(
        functools.partial(_route_body, n_tok=n_tok),
        grid=(rows // SUBLANES,),
        in_specs=[
            pl.BlockSpec(memory_space=pltpu.SMEM),
            pl.BlockSpec((n_e, SUBLANES, LANES), lambda i: (0, i, 0)),
        ],
        out_specs=[pick, pick, pl.BlockSpec((n_e, SUBLANES, LANES), lambda i: (0, 0, 0))],
        out_shape=[
            jax.ShapeDtypeStruct((TOP_K, rows, LANES), I32),
            jax.ShapeDtypeStruct((TOP_K, rows, LANES), F32),
            jax.ShapeDtypeStruct((n_e, SUBLANES, LANES), I32),
        ],
        compiler_params=_params("arbitrary"),
        name="route",
    )(bias, lg3)
    return idx.reshape(TOP_K, t), w.reshape(TOP_K, t), jnp.sum(cnt, axis=(1, 2))


def _moe_body(be_ref, nreal_ref, tok0_ref, tokn_ref, dstp_ref, hq_hbm, wg_ref, wu_ref, wd_ref, out_hbm,
              xb0, xb1, yb0, yb1, wgb, wub, wdb, gsem, ssem, *, rows):
    b = pl.program_id(0)
    nb = pl.num_programs(0)
    xbufs, ybufs = (xb0, xb1), (yb0, yb1)
    ns = xb0.shape[0] // rows
    grp = min(MOE_COPY_GROUP, rows)

    def n_real(k):
        return jnp.where((k >= 0) & (k < nb), nreal_ref[jnp.clip(k, 0, nb - 1)], 0)

    def token(ref, i):
        return ref.at[pl.ds(pl.multiple_of(i * ns, ns), ns), :]

    def start_groups(make_copy, n):
        for r in range(rows):
            @pl.when(n > (r // grp) * grp)
            def _(r=r):
                make_copy(r).start(priority=r % 2)

    def wait_groups(src, dst, sem, n):
        for g in range(rows // grp):
            @pl.when(n > g * grp)
            def _(g=g):
                part = pl.ds(g * grp * ns, grp * ns)
                pltpu.make_async_copy(src.at[part, :], dst.at[part, :], sem).wait()

    def gather(idx_ref, dst_buf, sem, n):
        start_groups(lambda r: pltpu.make_async_copy(token(hq_hbm, idx_ref[0, 0, r]), token(dst_buf, r), sem), n)

    @pl.when(b == 0)
    def _():
        xb0[...] = jnp.zeros(xb0.shape, I32)
        xb1[...] = jnp.zeros(xb1.shape, I32)
        gather(tok0_ref, xb0, gsem.at[0], n_real(0))

    @pl.when((b == 0) | (be_ref[b] != be_ref[jnp.maximum(b - 1, 0)]))
    def _():
        wgb[...] = wg_ref[...].astype(BF16)
        wub[...] = wu_ref[...].astype(BF16)
        wdb[...] = wd_ref[...].astype(BF16)

    def scatter_previous(o):
        start_groups(lambda r: pltpu.make_async_copy(token(ybufs[o], r), token(out_hbm, dstp_ref[0, 0, r]),
                                                     ssem.at[o]), n_real(b - 1))

    def block(s):
        o = 1 - s
        wait_groups(hq_hbm, xbufs[s], gsem.at[s], n_real(b))
        gather(tokn_ref, xbufs[o], gsem.at[o], n_real(b + 1))
        scatter_previous(o)
        xb = _load_packed_rows(xbufs[s], (), rows, ns).astype(BF16)
        act = jax.nn.silu(_dot(xb, wgb[...])) * _dot(xb, wub[...])
        y = _dot(act.astype(BF16), wdb[...]).astype(BF16)
        wait_groups(ybufs[s], out_hbm, ssem.at[s], n_real(b - 2))
        _store_packed_rows(ybufs[s], (), y, rows)

    def empty_block(s):
        scatter_previous(1 - s)
        wait_groups(ybufs[s], out_hbm, ssem.at[s], n_real(b - 2))

    for s in range(2):
        @pl.when((b % 2 == s) & (n_real(b) > 0))
        def _(s=s):
            block(s)

        @pl.when((b % 2 == s) & (n_real(b) == 0))
        def _(s=s):
            empty_block(s)

    @pl.when(b == nb - 1)
    def _():
        wait_groups(yb0, out_hbm, ssem.at[0], n_real(b - 1))


def _moe_call(block_e, n_real, tok, dst, hq, w_gate, w_up, w_down, out_rows, rows):
    n_blocks = block_e.shape[0]
    assert n_blocks % 2 == 0 and rows % min(MOE_COPY_GROUP, rows) == 0
    n_e, d, f = w_gate.shape
    ns = d // (2 * LANES)
    shifted = lambda off: pl.BlockSpec(
        (1, 1, rows), lambda b, be, nr: (jnp.clip(b + off, 0, n_blocks - 1), 0, 0), memory_space=pltpu.SMEM)
    expert = lambda b, be, nr: (be[b], 0, 0)
    return pl.pallas_call(
        functools.partial(_moe_body, rows=rows),
        grid_spec=pltpu.PrefetchScalarGridSpec(
            num_scalar_prefetch=2,
            grid=(n_blocks,),
            in_specs=[
                pl.BlockSpec((1, 1, rows), lambda b, be, nr: (0, 0, 0), memory_space=pltpu.SMEM),
                shifted(1), shifted(-1),
                pl.BlockSpec(memory_space=pl.ANY),
                pl.BlockSpec((None, d, f), expert),
                pl.BlockSpec((None, d, f), expert),
                pl.BlockSpec((None, f, d), expert),
            ],
            out_specs=pl.BlockSpec(memory_space=pl.ANY),
            scratch_shapes=[pltpu.VMEM((rows * ns, LANES), I32)] * 4 + [
                pltpu.VMEM((d, f), BF16),
                pltpu.VMEM((d, f), BF16),
                pltpu.VMEM((f, d), BF16),
                pltpu.SemaphoreType.DMA((2,)),
                pltpu.SemaphoreType.DMA((2,)),
            ],
        ),
        out_shape=jax.ShapeDtypeStruct((out_rows * ns, LANES), I32),
        compiler_params=_params("arbitrary"),
        name="moe",
    )(block_e, n_real, tok, tok, dst, hq, w_gate, w_up, w_down)


def _dispatch_plan(idx, counts, n_tok, rows):
    m = n_tok * TOP_K
    n_real = (m + N_EXPERTS * (rows - 1) + rows - 1) // rows
    n_blocks = n_real + 1 + (n_real + 1) % 2
    p = n_blocks * rows
    n_spare = p - m
    plane = -(-(n_tok + -(-n_spare // TOP_K)) // SUBLANES) * SUBLANES
    flat_e = idx.T.reshape(m)
    order = jnp.argsort(flat_e).astype(I32)
    padded = (counts + rows - 1) // rows * rows
    pad_end = jnp.cumsum(padded)
    pad_start = pad_end - padded
    start = jnp.cumsum(counts) - counts
    block_first = jnp.arange(n_blocks, dtype=I32) * rows
    block_e = jnp.minimum(jnp.sum((pad_end[None, :] <= block_first[:, None]).astype(I32), axis=1), N_EXPERTS - 1)
    slot = block_first[:, None] + jnp.arange(rows, dtype=I32)[None, :]
    own = block_e[:, None] == jnp.arange(N_EXPERTS, dtype=I32)[None, :]
    lookup = lambda table: jnp.sum(jnp.where(own, table[None, :], 0), axis=1, keepdims=True)
    q = slot - lookup(pad_start)
    cnt_b = lookup(counts)
    start_b = lookup(start)
    real = q < cnt_b
    assign = order[jnp.clip(start_b + q, 0, m - 1)]
    t_of, j_of = assign // TOP_K, assign % TOP_K
    spare_rank = slot - (start_b + jnp.minimum(q, cnt_b))
    tok = jnp.where(real, t_of, 0)
    dst = jnp.where(real, j_of * plane + t_of, (spare_rank % TOP_K) * plane + n_tok + spare_rank // TOP_K)
    n_real_rows = jnp.clip(cnt_b[:, 0] - q[:, 0], 0, rows).astype(I32)
    shape = (n_blocks, 1, rows)
    return block_e.astype(I32), n_real_rows, tok.reshape(shape), dst.reshape(shape), plane


def _final_body(base_ref, y8_ref, w_ref, g_ref, b_ref, o_ref):
    w = w_ref[...]
    ffn = None
    tm = w.shape[0]
    ns = y8_ref.shape[1] // tm
    for j in range(TOP_K):
        yj = _load_packed_rows(y8_ref, (j,), tm, ns) * w[:, j:j + 1]
        ffn = yj if ffn is None else ffn + yj
    o_ref[...] = _layer_norm(base_ref[...] + ffn, g_ref[...], b_ref[...])


def _final_call(base, y8, w8, ln_g, ln_b, tm, row_offset):
    t, d = base.shape
    off = row_offset // tm
    ns = d // (2 * LANES)
    return pl.pallas_call(
        _final_body,
        grid=(t // tm,),
        in_specs=[
            pl.BlockSpec((tm, d), lambda i: (i, 0)),
            pl.BlockSpec((TOP_K, tm * ns, LANES), lambda i: (0, i + off, 0)),
            pl.BlockSpec((tm, TOP_K), lambda i: (i + off, 0)),
            pl.BlockSpec((1, d), lambda i: (0, 0)),
            pl.BlockSpec((1, d), lambda i: (0, 0)),
        ],
        out_specs=pl.BlockSpec((tm, d), lambda i: (i, 0)),
        out_shape=jax.ShapeDtypeStruct((t, d), F32),
        compiler_params=_params("parallel"),
        name="final",
    )(base, y8, w8, ln_g, ln_b)


def _mixer_and_post(x, pos_tables, w, tm, seq_rows, attend, conv_state, alpha, hq_tokens, hq_buf, hq_offset):
    q, k, v = _qkv_call(x, w['qkv'], pos_tables, tm)
    attn = attend(q, k, v)
    cb, u_tail = _conv_call(x, w['b'], w['c'], w['h'], w['conv'], tm, seq_rows, conv_state)
    mp = _gate_call(x, attn, cb, w['ga'], w['gc'], w['attn_out'], w['conv_out'], tm)
    hq, base, logits_t = _post_call(x, mp, w['o'], w['ln1_g'], w['ln1_b'], w['router_t'],
                                    w['sh_gate'], w['sh_up'], w['sh_down'], min(tm, 4 * LANES), alpha,
                                    hq_tokens, hq_buf, hq_offset)
    return k, v, u_tail, hq, base, logits_t


def kernel(x_prompt, x_sample, cache_k, cache_v, state_conv, w_in, attn_sinks, conv_w, w_attn_out, w_conv_out, w_o, ln1_g, ln1_b, w_router, router_bias, w_exp_gate, w_exp_up, w_exp_down, w_sh_gate, w_sh_up, w_sh_down, ln2_g, ln2_b):
    depth, d, _ = w_in.shape
    batch, seq, _ = x_prompt.shape
    dec_batch, dec_seq, _ = x_sample.shape
    win_buf = cache_k.shape[2]
    cdim = conv_w.shape[2]
    assert dec_seq == 1 and win_buf == WINDOW and seq % WINDOW == 0
    alpha = (2 * depth) ** 0.25
    t_p, t_s = batch * seq, dec_batch * dec_seq
    tm_p, tm_s = _tile(seq, 8 * LANES), _tile(t_s, LANES)
    assert t_p % tm_s == 0 and t_s % tm_s == 0
    tab_p = _rope_tables(jnp.arange(seq))
    tab_s = _rope_tables(jnp.full((tm_s,), PAST_LEN, I32))

    yp = x_prompt.reshape(t_p, d)
    ys = x_sample.reshape(t_s, d)
    p_k, p_v, p_c, s_k, s_v, s_c = [], [], [], [], [], []
    for l in range(depth):
        wl = w_in[l].astype(BF16)
        o = 0
        w = {}
        for name, width in (('qkv', Q_DIM + 2 * KV_DIM), ('b', cdim), ('c', cdim), ('h', cdim), ('ga', d), ('gc', d)):
            w[name] = wl[:, o:o + width]
            o += width
        w.update(
            conv=conv_w[l], attn_out=w_attn_out[l].astype(BF16), conv_out=w_conv_out[l].astype(BF16),
            o=w_o[l].astype(BF16), ln1_g=ln1_g[l][None], ln1_b=ln1_b[l][None],
            router_t=w_router[l].T.astype(BF16), sh_gate=w_sh_gate[l].astype(BF16),
            sh_up=w_sh_up[l].astype(BF16), sh_down=w_sh_down[l].astype(BF16))
        sinks = attn_sinks[l]

        n_tok = t_p + t_s
        k, v, u_tail, hq_p, base_p, lg_p = _mixer_and_post(
            yp, tab_p, w, tm_p, seq,
            lambda q, k, v: _attn_prompt_call(q, k, v, sinks, batch, seq), None, alpha, n_tok, None, 0)
        keep = min(WINDOW, seq)
        for kv, acc in ((k, p_k), (v, p_v)):
            tail_rows = kv.reshape(batch, seq, KV_DIM)[:, seq - keep:]
            acc.append(tail_rows.reshape(batch, keep, N_KV_HEADS, HEAD_DIM))
        tails = u_tail.reshape(batch, seq // tm_p, SUBLANES, cdim)
        p_c.append(tails[:, -1, SUBLANES - (CONV_WIDTH - 1):])

        new_kv = {}

        def attend_sample(q, k, v, l=l):
            attn, k_win, v_win = _attn_sample_call(q, cache_k[l].reshape(t_s, win_buf, KV_DIM),
                                                   cache_v[l].reshape(t_s, win_buf, KV_DIM), k, v, sinks)
            new_kv['k'] = k_win.reshape(t_s, win_buf, N_KV_HEADS, HEAD_DIM)
            new_kv['v'] = v_win.reshape(t_s, win_buf, N_KV_HEADS, HEAD_DIM)
            return attn

        state = (state_conv[l][:, 0], state_conv[l][:, 1])
        _, _, u_s, hq, base_s, lg_s = _mixer_and_post(ys, tab_s, w, tm_s, 1, attend_sample, state, alpha,
                                                      n_tok, hq_p, t_p)
        s_k.append(new_kv['k'])
        s_v.append(new_kv['v'])
        s_c.append(jnp.concatenate([state_conv[l][:, 1:], u_s[:, None]], axis=1))

        route_tile = SUBLANES * LANES
        t_pad = -(-n_tok // route_tile) * route_tile
        logits_t = jnp.concatenate([lg_p, lg_s, jnp.zeros((N_EXPERTS, t_pad - n_tok), F32)], axis=1)
        idx, wts, counts = _route_call(logits_t, router_bias[l], n_tok)
        idx, wts = idx[:, :n_tok], wts[:, :n_tok]
        block_e, n_real, tok, dst, plane = _dispatch_plan(idx, counts, n_tok, MOE_BLOCK_ROWS)
        y8 = _moe_call(block_e, n_real, tok, dst, hq, w_exp_gate[l], w_exp_up[l], w_exp_down[l],
                       TOP_K * plane, MOE_BLOCK_ROWS)
        y8 = y8.reshape(TOP_K, plane * (d // (2 * LANES)), LANES)
        w8 = wts.T
        yp = _final_call(base_p, y8, w8, ln2_g[l][None], ln2_b[l][None], min(tm_p, 2 * LANES), 0)
        ys = _final_call(base_s, y8, w8, ln2_g[l][None], ln2_b[l][None], tm_s, t_p)

    return (yp.reshape(batch, seq, d), ys.reshape(dec_batch, dec_seq, d), jnp.stack(p_k), jnp.stack(p_v),
            jnp.stack(p_c), jnp.stack(s_k), jnp.stack(s_v), jnp.stack(s_c))
```
